```python
import math
import jax, jax.numpy as jnp
from jax import lax
import numpy as np

D_MODEL = 4096
BATCH = 1
SEQ = 8192
DEPTH = 1

HEAD_DIM = 128
ATTN_GROUPS = ((128, 1), (512, 4), (2048, 16))
N_GROUPS = len(ATTN_GROUPS)
HEADS_PER_GROUP = 8
N_ATTN_HEADS = N_GROUPS * HEADS_PER_GROUP
ATTN_WIDTH = N_ATTN_HEADS * HEAD_DIM
ATTN_OUT_WIDTH = HEADS_PER_GROUP * HEAD_DIM
CONV_WIDTH = D_MODEL // 2
CONV_KERNEL = 31
N_BRANCHES = 2
IN_WIDTH = 3 * ATTN_WIDTH + 2 * CONV_WIDTH + N_BRANCHES * D_MODEL
REL_BUCKETS = 32
REL_MAX_DISTANCE = 1024
N_EXPERTS = 32
TOP_K = 4
EXPERT_FF = 1536
SWIGLU_ALPHA = 1.702
SWIGLU_LIMIT = 7.0
EXPERT_BLOCK = 256
NORM_EPS = 1e-6
NEG_INF = -1e30

kernel_name = "hybrid_gated_conv_dilated_attn_moe_block"


def rms_norm(x, gain):
    xf = x.astype(jnp.float32)
    y = xf * lax.rsqrt(jnp.mean(xf * xf, axis=-1, keepdims=True) + NORM_EPS)
    return (y * gain.astype(jnp.float32)).astype(x.dtype)


def rel_bucket(rel):
    nb = REL_BUCKETS // 2
    max_exact = nb // 2
    n = jnp.abs(rel)
    side = jnp.where(rel > 0, nb, 0)
    nf = jnp.maximum(n, 1).astype(jnp.float32)
    large = max_exact + (jnp.log(nf / max_exact) / math.log(REL_MAX_DISTANCE / max_exact)
                         * (nb - max_exact)).astype(jnp.int32)
    large = jnp.minimum(large, nb - 1)
    return side + jnp.where(n < max_exact, n, large)


def dilated_band_attention(q, k, v, bias_table, window, dilation):
    b, s, h, e = q.shape
    n_side = window // (2 * dilation)
    blk = n_side
    seg = s // dilation
    nblk = -(-seg // blk)
    seg_p = nblk * blk

    def split(t):
        return t.reshape(b, seg, dilation, h, e)

    qs = jnp.pad(split(q), ((0, 0), (0, seg_p - seg), (0, 0), (0, 0), (0, 0)))
    qs = qs.reshape(b, nblk, blk, dilation, h, e)

    def windows(t):
        tp = jnp.pad(split(t), ((0, 0), (blk, seg_p - seg + blk), (0, 0), (0, 0), (0, 0)))
        tp = tp.reshape(b, nblk + 2, blk, dilation, h, e)
        return jnp.concatenate([tp[:, :-2], tp[:, 1:-1], tp[:, 2:]], axis=2)

    kw, vw = windows(k), windows(v)
    scores = jnp.einsum('bnqrhe,bnkrhe->bnrhqk', qs, kw).astype(jnp.float32) * (HEAD_DIM ** -0.5)

    rel = jnp.arange(3 * blk)[None, :] - blk - jnp.arange(blk)[:, None]
    key_pos = jnp.arange(nblk)[:, None] * blk + jnp.arange(3 * blk)[None, :] - blk
    valid = (jnp.abs(rel) <= n_side)[None] & ((key_pos >= 0) & (key_pos < seg))[:, None, :]
    bias = jnp.transpose(bias_table.astype(jnp.float32)[rel_bucket(rel * dilation)], (2, 0, 1))
    scores = jnp.where(valid[None, :, None, None], scores + bias, NEG_INF)

    m = jnp.max(scores, axis=-1)
    p = jnp.exp(scores - m[..., None])
    l = jnp.sum(p, axis=-1)
    o = jnp.einsum('bnrhqk,bnkrhe->bnqrhe', p, vw.astype(jnp.float32))

    o = o.reshape(b, seg_p, dilation, h, e)[:, :seg].reshape(b, s, h, e)

    def back(t):
        t = jnp.moveaxis(t, 4, 2).reshape(b, seg_p, dilation, h)
        return t[:, :seg].reshape(b, s, h)

    return o, back(m), back(l)


def mixer_sublayer(h, w_in, q_norm_g, k_norm_g, rel_bias, conv_w, conv_b, conv_norm_g,
                   w_attn_out, w_conv_out, w_o):
    b, s, _ = h.shape
    proj = h @ w_in
    q, k, v, conv_in, gate_logits = jnp.split(
        proj, [ATTN_WIDTH, 2 * ATTN_WIDTH, 3 * ATTN_WIDTH, 3 * ATTN_WIDTH + 2 * CONV_WIDTH], axis=-1)

    q = rms_norm(q.reshape(b, s, N_GROUPS, HEADS_PER_GROUP, HEAD_DIM), q_norm_g)
    k = rms_norm(k.reshape(b, s, N_GROUPS, HEADS_PER_GROUP, HEAD_DIM), k_norm_g)
    v = v.reshape(b, s, N_GROUPS, HEADS_PER_GROUP, HEAD_DIM)
    outs, maxes, sums = [], [], []
    for g, (window, dilation) in enumerate(ATTN_GROUPS):
        tbl = rel_bias[:, g * HEADS_PER_GROUP:(g + 1) * HEADS_PER_GROUP]
        o, m, l = dilated_band_attention(q[:, :, g], k[:, :, g], v[:, :, g], tbl, window, dilation)
        outs.append(o)
        maxes.append(m)
        sums.append(l)
    m_all = jnp.maximum(jnp.maximum(maxes[0], maxes[1]), maxes[2])
    num = sum(jnp.exp(mg - m_all)[..., None] * og for og, mg in zip(outs, maxes))
    den = sum(jnp.exp(mg - m_all) * lg for lg, mg in zip(sums, maxes))
    attn = (num / den[..., None]).astype(h.dtype).reshape(b, s, ATTN_OUT_WIDTH)
    attn_branch = attn @ w_attn_out

    a, a_gate = jnp.split(conv_in, 2, axis=-1)
    u = a * jax.nn.sigmoid(a_gate)
    u = lax.conv_general_dilated(u, conv_w[:, None, :], window_strides=(1,),
                                 padding=[(CONV_KERNEL // 2, CONV_KERNEL // 2)],
                                 dimension_numbers=('NWC', 'WIO', 'NWC'),
                                 feature_group_count=CONV_WIDTH) + conv_b
    u = jax.nn.silu(rms_norm(u, conv_norm_g))
    conv_branch = u @ w_conv_out

    gates = jax.nn.sigmoid(gate_logits.reshape(b, s, N_BRANCHES, D_MODEL))
    merged = gates[:, :, 0] * attn_branch + gates[:, :, 1] * conv_branch
    return merged @ w_o


def moe_ffn(h, layer, w_router, b_router, w_up, b_up, w_down, b_down):
    b, s, d = h.shape
    t = b * s
    xf = h.reshape(t, d)
    logits = (xf @ w_router + b_router).astype(jnp.float32)
    top_vals, top_idx = lax.top_k(logits, TOP_K)
    top_w = jax.nn.softmax(top_vals, axis=-1)

    n_assign = t * TOP_K
    e_flat = top_idx.reshape(-1)
    tok_flat = jnp.repeat(jnp.arange(t, dtype=jnp.int32), TOP_K)
    w_flat = top_w.reshape(-1)
    order = jnp.argsort(e_flat, stable=True)
    e_sorted, tok_sorted, w_sorted = e_flat[order], tok_flat[order], w_flat[order]

    counts = jnp.bincount(e_flat, length=N_EXPERTS)
    starts = jnp.cumsum(counts) - counts
    padded = (counts + EXPERT_BLOCK - 1) // EXPERT_BLOCK * EXPERT_BLOCK
    padded_end = jnp.cumsum(padded)
    padded_start = padded_end - padded
    dest = padded_start[e_sorted] + jnp.arange(n_assign) - starts[e_sorted]

    n_slots = -(-(n_assign + N_EXPERTS * (EXPERT_BLOCK - 1)) // EXPERT_BLOCK) * EXPERT_BLOCK
    n_blocks = n_slots // EXPERT_BLOCK
    slot_tok = jnp.zeros((n_slots,), jnp.int32).at[dest].set(tok_sorted)
    slot_w = jnp.zeros((n_slots,), jnp.float32).at[dest].set(w_sorted)
    block_expert = jnp.minimum(
        jnp.searchsorted(padded_end, jnp.arange(n_blocks) * EXPERT_BLOCK, side='right'), N_EXPERTS - 1)

    def expert_block(args):
        tok, e = args
        xb = xf[tok]
        gu = xb @ w_up[layer, e] + b_up[layer, e]
        glu = jnp.minimum(gu[:, 0::2], SWIGLU_LIMIT)
        lin = jnp.clip(gu[:, 1::2], -SWIGLU_LIMIT, SWIGLU_LIMIT)
        act = glu * jax.nn.sigmoid(SWIGLU_ALPHA * glu) * (lin + 1)
        return act @ w_down[layer, e] + b_down[layer, e]

    y_slots = lax.map(expert_block, (slot_tok.reshape(n_blocks, EXPERT_BLOCK), block_expert))
    y = jax.ops.segment_sum(y_slots.reshape(n_slots, d).astype(jnp.float32) * slot_w[:, None],
                            slot_tok, num_segments=t)
    return y.astype(h.dtype).reshape(b, s, d)


def _normal(k, shape, scale):
    return jax.random.normal(k, shape, jnp.float32) * scale


def setup_inputs(seed: int = 0) -> dict:
    key = jax.random.key(seed)
    ks = jax.random.split(key, 24)
    L = DEPTH
    return {
        "x": _normal(ks[0], (BATCH, SEQ, D_MODEL), 1.0),
        "c": _normal(ks[1], (BATCH, D_MODEL), 1.0),
        "norm1_g": 1.0 + _normal(ks[2], (L, D_MODEL), 0.02),
        "norm2_g": 1.0 + _normal(ks[3], (L, D_MODEL), 0.02),
        "w_ada": _normal(ks[4], (L, D_MODEL, 6 * D_MODEL), 0.5 * D_MODEL ** -0.5),
        "b_ada": _normal(ks[5], (L, 6 * D_MODEL), 0.02),
        "w_in": _normal(ks[6], (L, D_MODEL, IN_WIDTH), D_MODEL ** -0.5),
        "q_norm_g": 1.0 + _normal(ks[7], (L, HEAD_DIM), 0.02),
        "k_norm_g": 1.0 + _normal(ks[8], (L, HEAD_DIM), 0.02),
        "rel_bias": _normal(ks[9], (REL_BUCKETS, N_ATTN_HEADS), 0.5),
        "conv_w": _normal(ks[10], (L, CONV_KERNEL, CONV_WIDTH), CONV_KERNEL ** -0.5),
        "conv_b": _normal(ks[11], (L, CONV_WIDTH), 0.02),
        "conv_norm_g": 1.0 + _normal(ks[12], (L, CONV_WIDTH), 0.02),
        "w_attn_out": _normal(ks[13], (L, ATTN_OUT_WIDTH, D_MODEL), ATTN_OUT_WIDTH ** -0.5),
        "w_conv_out": _normal(ks[14], (L, CONV_WIDTH, D_MODEL), CONV_WIDTH ** -0.5),
        "w_o": _normal(ks[15], (L, D_MODEL, D_MODEL), D_MODEL ** -0.5),
        "w_router": _normal(ks[16], (L, D_MODEL, N_EXPERTS), D_MODEL ** -0.5),
        "b_router": _normal(ks[17], (L, N_EXPERTS), 0.01),
        "w_up": _normal(ks[18], (L, N_EXPERTS, D_MODEL, 2 * EXPERT_FF), D_MODEL ** -0.5),
        "b_up": _normal(ks[19], (L, N_EXPERTS, 2 * EXPERT_FF), 0.01),
        "w_down": _normal(ks[20], (L, N_EXPERTS, EXPERT_FF, D_MODEL), EXPERT_FF ** -0.5),
        "b_down": _normal(ks[21], (L, N_EXPERTS, D_MODEL), 0.01),
    }


def reference(x, c, norm1_g, norm2_g, w_ada, b_ada, w_in, q_norm_g, k_norm_g, rel_bias,
              conv_w, conv_b, conv_norm_g, w_attn_out, w_conv_out, w_o,
              w_router, b_router, w_up, b_up, w_down, b_down):
    c_act = jax.nn.silu(c)
    for layer in range(DEPTH):
        mod = c_act @ w_ada[layer] + b_ada[layer]
        shift1, scale1, gate1, shift2, scale2, gate2 = jnp.split(mod[:, None, :], 6, axis=-1)

        h = rms_norm(x, norm1_g[layer]) * (1 + scale1) + shift1
        y = mixer_sublayer(h, w_in[layer], q_norm_g[layer], k_norm_g[layer], rel_bias,
                           conv_w[layer], conv_b[layer], conv_norm_g[layer],
                           w_attn_out[layer], w_conv_out[layer], w_o[layer])
        x = x + gate1 * y

        h = rms_norm(x, norm2_g[layer]) * (1 + scale2) + shift2
        y = moe_ffn(h, layer, w_router[layer], b_router[layer], w_up, b_up, w_down, b_down)
        x = x + gate2 * y
    return x
```

```python
import functools
import math

import numpy as np
import jax
import jax.numpy as jnp
from jax import lax
from jax.experimental import pallas as pl
from jax.experimental.pallas import tpu as pltpu

D_MODEL = 4096
SEQ = 8192
HEAD_DIM = 128
ATTN_GROUPS = ((128, 1), (512, 4), (2048, 16))
N_GROUPS = len(ATTN_GROUPS)
HEADS_PER_GROUP = 8
GROUP_WIDTH = HEADS_PER_GROUP * HEAD_DIM
ATTN_WIDTH = N_GROUPS * GROUP_WIDTH
CONV_WIDTH = D_MODEL // 2
CONV_KERNEL = 31
CONV_OFF = 3 * ATTN_WIDTH
GATE_OFF = CONV_OFF + 2 * CONV_WIDTH
IN_WIDTH = GATE_OFF + 2 * D_MODEL
REL_BUCKETS = 32
REL_MAX_DISTANCE = 1024
N_EXPERTS = 32
TOP_K = 4
EXPERT_FF = 1536
SWIGLU_ALPHA = 1.702
SWIGLU_LIMIT = 7.0
NORM_EPS = 1e-6
NEG_INF = -1e30
N_SIDE = 64

V7X_VMEM_LIMIT = 56 * 1024 * 1024

SLOT_BLOCK = 256
BLOCKS_PER_TILE = 4
N_ASSIGN = SEQ * TOP_K
TILE_ROWS = SLOT_BLOCK * BLOCKS_PER_TILE
N_SLOT_BLOCKS = (N_ASSIGN + N_EXPERTS * (SLOT_BLOCK - 1)) // SLOT_BLOCK + 1
MAX_TILES = (N_SLOT_BLOCKS + N_EXPERTS * (BLOCKS_PER_TILE - 1)) // BLOCKS_PER_TILE
HALF = D_MODEL // 2

F32 = jnp.float32
BF16 = jnp.bfloat16
U32 = jnp.uint32


def _params(sem, vmem=V7X_VMEM_LIMIT):
    return pltpu.CompilerParams(dimension_semantics=sem, vmem_limit_bytes=vmem)


def _pack_halves(v):
    n = v.shape[-1] // 2
    lo = pltpu.bitcast(v[:, :n].astype(BF16).astype(F32), U32) >> 16
    hi = pltpu.bitcast(v[:, n:].astype(BF16).astype(F32), U32) & jnp.uint32(0xFFFF0000)
    return lo | hi


def _unpack_halves(p):
    lo = pltpu.bitcast(p << 16, F32)
    hi = pltpu.bitcast(p & jnp.uint32(0xFFFF0000), F32)
    return lo, hi


def _ada_kernel(c_ref, w_ref, b_ref, o_ref, cs_ref):
    @pl.when(pl.program_id(0) == 0)
    def _():
        c = c_ref[...]
        cs_ref[...] = c * jax.nn.sigmoid(c)

    tn = o_ref.shape[-1]
    rows_per_step = 64

    def body(r, acc):
        rows = pl.ds(pl.multiple_of(r * rows_per_step, rows_per_step), rows_per_step)
        cs = cs_ref[rows, :]
        prod = w_ref[rows, :] * jnp.concatenate([cs] * (tn // 128), axis=1)
        return acc + prod.reshape(rows_per_step // 8, 8, tn).sum(axis=0)

    acc = lax.fori_loop(0, D_MODEL // rows_per_step, body, jnp.zeros((8, tn), F32))
    o_ref[...] = jnp.sum(acc, axis=0, keepdims=True) + b_ref[...]


def _ada(c, w_ada, b_ada):
    tn = 512
    n = w_ada.shape[-1]
    c_lanes = jnp.broadcast_to(c.reshape(D_MODEL, 1), (D_MODEL, 128))
    return pl.pallas_call(
        _ada_kernel,
        grid=(n // tn,),
        in_specs=[pl.BlockSpec((D_MODEL, 128), lambda j: (0, 0)),
                  pl.BlockSpec((D_MODEL, tn), lambda j: (0, j)),
                  pl.BlockSpec((1, tn), lambda j: (0, j))],
        out_specs=pl.BlockSpec((1, tn), lambda j: (0, j)),
        out_shape=jax.ShapeDtypeStruct((1, n), F32),
        scratch_shapes=[pltpu.VMEM((D_MODEL, 128), F32)],
        compiler_params=_params(("arbitrary",)),
        name="ada_mod",
    )(c_lanes, w_ada, b_ada.reshape(1, n))


def _modulated_norm(x, g, scale, shift):
    ms = jnp.mean(x * x, axis=-1, keepdims=True)
    y = x * lax.rsqrt(ms + NORM_EPS) * g
    return y * (1.0 + scale) + shift


def _norm1_kernel(x_ref, g_ref, sc_ref, sh_ref, o_ref):
    o_ref[...] = _modulated_norm(x_ref[...], g_ref[...], sc_ref[...], sh_ref[...]).astype(o_ref.dtype)


def _norm1(x, g, mod):
    tm = 256
    vec = lambda k: pl.BlockSpec((1, D_MODEL), lambda i, k=k: (0, k))
    return pl.pallas_call(
        _norm1_kernel,
        grid=(SEQ // tm,),
        in_specs=[pl.BlockSpec((tm, D_MODEL), lambda i: (i, 0)), vec(0), vec(1), vec(0)],
        out_specs=pl.BlockSpec((tm, D_MODEL), lambda i: (i, 0)),
        out_shape=jax.ShapeDtypeStruct((SEQ, D_MODEL), BF16),
        compiler_params=_params(("arbitrary",)),
        name="norm1",
    )(x, g.reshape(1, D_MODEL), mod, mod)


def _head_norm(acc, gain, post_scale):
    outs = []
    for h in range(acc.shape[-1] // HEAD_DIM):
        blk = acc[:, h * HEAD_DIM:(h + 1) * HEAD_DIM]
        ms = jnp.mean(blk * blk, axis=-1, keepdims=True)
        outs.append(blk * lax.rsqrt(ms + NORM_EPS) * (gain * post_scale))
    return jnp.concatenate(outs, axis=1)


def _qkv_kernel(h_ref, w_ref, gq_ref, gk_ref, o_ref, wb_ref, acc_ref, *, dil):
    which = pl.program_id(0)
    tm = h_ref.shape[0]

    @pl.when(pl.program_id(1) == 0)
    def _():
        wb_ref[...] = w_ref[...].astype(BF16)

    acc = jnp.dot(h_ref[...], wb_ref[...], preferred_element_type=F32)

    def write(vals):
        if dil == 1:
            o_ref[0] = vals.astype(BF16)
        else:
            for h in range(HEADS_PER_GROUP):
                sl = slice(h * HEAD_DIM, (h + 1) * HEAD_DIM)
                acc_ref[h] = vals[:, sl]
                for r in range(dil):
                    o_ref[r, :, sl] = acc_ref[h, pl.ds(r, tm // dil, stride=dil), :].astype(BF16)

    @pl.when(which == 0)
    def _():
        write(_head_norm(acc, gq_ref[...], HEAD_DIM ** -0.5))

    @pl.when(which == 1)
    def _():
        write(_head_norm(acc, gk_ref[...], 1.0))

    @pl.when(which == 2)
    def _():
        write(acc)


def _qkv_proj(h, w_in, gq, gk, group, dil):
    tm = 512
    seg = SEQ // dil
    return pl.pallas_call(
        functools.partial(_qkv_kernel, dil=dil),
        grid=(3, SEQ // tm),
        in_specs=[pl.BlockSpec((tm, D_MODEL), lambda w, i: (i, 0)),
                  pl.BlockSpec((D_MODEL, GROUP_WIDTH), lambda w, i: (0, w * N_GROUPS + group),
                               pipeline_mode=pl.Buffered(1)),
                  pl.BlockSpec((1, HEAD_DIM), lambda w, i: (0, 0)),
                  pl.BlockSpec((1, HEAD_DIM), lambda w, i: (0, 0))],
        out_specs=pl.BlockSpec((None, dil, tm // dil, GROUP_WIDTH), lambda w, i: (w, 0, i, 0)),
        out_shape=jax.ShapeDtypeStruct((3, dil, seg, GROUP_WIDTH), BF16),
        scratch_shapes=[pltpu.VMEM((D_MODEL, GROUP_WIDTH), BF16), pltpu.VMEM((HEADS_PER_GROUP, tm, HEAD_DIM), F32)],
        compiler_params=_params(("arbitrary", "arbitrary")),
        name=f"qkv_proj_g{group}",
    )(h, w_in, gq.reshape(1, HEAD_DIM), gk.reshape(1, HEAD_DIM))


def _glu_kernel(h_ref, wa_ref, wg_ref, o_ref, wab_ref, wgb_ref):
    @pl.when(pl.program_id(1) == 0)
    def _():
        wab_ref[...] = wa_ref[...].astype(BF16)
        wgb_ref[...] = wg_ref[...].astype(BF16)

    h = h_ref[...]
    a = jnp.dot(h, wab_ref[...], preferred_element_type=F32)
    g = jnp.dot(h, wgb_ref[...], preferred_element_type=F32)
    o_ref[...] = (a * jax.nn.sigmoid(g)).astype(o_ref.dtype)


def _conv_glu_proj(h, w_in):
    tm, tn = 512, 512
    a0 = CONV_OFF // tn
    g0 = (CONV_OFF + CONV_WIDTH) // tn
    return pl.pallas_call(
        _glu_kernel,
        grid=(CONV_WIDTH // tn, SEQ // tm),
        in_specs=[pl.BlockSpec((tm, D_MODEL), lambda j, i: (i, 0)),
                  pl.BlockSpec((D_MODEL, tn), lambda j, i: (0, a0 + j)),
                  pl.BlockSpec((D_MODEL, tn), lambda j, i: (0, g0 + j))],
        out_specs=pl.BlockSpec((tm, tn), lambda j, i: (i, j)),
        out_shape=jax.ShapeDtypeStruct((SEQ, CONV_WIDTH), BF16),
        scratch_shapes=[pltpu.VMEM((D_MODEL, tn), BF16), pltpu.VMEM((D_MODEL, tn), BF16)],
        compiler_params=_params(("arbitrary", "arbitrary")),
        name="conv_glu_proj",
    )(h, w_in, w_in)


def _gate_kernel(h_ref, w_ref, o_ref, wb_ref):
    @pl.when(pl.program_id(1) == 0)
    def _():
        wb_ref[...] = w_ref[...].astype(BF16)

    acc = jnp.dot(h_ref[...], wb_ref[...], preferred_element_type=F32)
    o_ref[...] = jax.nn.sigmoid(acc).astype(o_ref.dtype)


def _gate_proj(h, w_in):
    tm, tn = 512, 512
    c0 = GATE_OFF // tn
    return pl.pallas_call(
        _gate_kernel,
        grid=(2 * D_MODEL // tn, SEQ // tm),
        in_specs=[pl.BlockSpec((tm, D_MODEL), lambda j, i: (i, 0)),
                  pl.BlockSpec((D_MODEL, tn), lambda j, i: (0, c0 + j))],
        out_specs=pl.BlockSpec((tm, tn), lambda j, i: (i, j)),
        out_shape=jax.ShapeDtypeStruct((SEQ, 2 * D_MODEL), BF16),
        scratch_shapes=[pltpu.VMEM((D_MODEL, tn), BF16)],
        compiler_params=_params(("arbitrary", "arbitrary")),
        name="gate_proj",
    )(h, w_in)


ATTN_TQ = 128
ATTN_TK = ATTN_TQ + 2 * N_SIDE
LSE_LANES = 128


def _bucket_tile(dil):
    a = np.arange(ATTN_TQ)[:, None]
    j = np.arange(ATTN_TK)[None, :]
    steps = j - N_SIDE - a
    rel = steps * dil
    nb = REL_BUCKETS // 2
    max_exact = nb // 2
    n = np.abs(rel)
    side = np.where(rel > 0, nb, 0)
    nf = np.maximum(n, 1).astype(np.float32)
    large = max_exact + (np.log(nf / np.float32(max_exact)) / np.float32(math.log(REL_MAX_DISTANCE / max_exact))
                         * np.float32(nb - max_exact)).astype(np.int32)
    large = np.minimum(large, nb - 1)
    bucket = side + np.where(n < max_exact, n, large)
    return np.where(np.abs(steps) <= N_SIDE, bucket, -1).astype(np.int32)


def _bias_kernel(tbl_ref, idx_ref, o_ref, *, group):
    h = pl.program_id(0)
    idx = idx_ref[...]
    acc = jnp.full(idx.shape, NEG_INF, F32)
    for b in range(REL_BUCKETS):
        acc = jnp.where(idx == b, tbl_ref[b, group * HEADS_PER_GROUP + h], acc)
    o_ref[...] = acc


def _bias_tiles(rel_bias, group, dil):
    return pl.pallas_call(
        functools.partial(_bias_kernel, group=group),
        grid=(HEADS_PER_GROUP,),
        in_specs=[pl.BlockSpec(memory_space=pltpu.SMEM),
                  pl.BlockSpec((ATTN_TQ, ATTN_TK), lambda h: (0, 0))],
        out_specs=pl.BlockSpec((None, ATTN_TQ, ATTN_TK), lambda h: (h, 0, 0)),
        out_shape=jax.ShapeDtypeStruct((HEADS_PER_GROUP, ATTN_TQ, ATTN_TK), F32),
        compiler_params=_params(("arbitrary",)),
        name=f"attn_bias_g{group}",
    )(rel_bias, jnp.asarray(_bucket_tile(dil)))


def _attn_kernel(q_ref, kp_ref, kc_ref, kn_ref, vp_ref, vc_ref, vn_ref, bias_ref, o_ref, lse_ref, *, seg):
    i = pl.program_id(1)
    q = q_ref[...]
    k = jnp.concatenate([kp_ref[...], kc_ref[...], kn_ref[...]], axis=0)
    v = jnp.concatenate([vp_ref[...], vc_ref[...], vn_ref[...]], axis=0)
    kpos = i * ATTN_TQ - N_SIDE + lax.broadcasted_iota(jnp.int32, (1, ATTN_TK), 1)
    valid = (kpos >= 0) & (kpos < seg)
    lane = lax.broadcasted_iota(jnp.int32, (ATTN_TQ, LSE_LANES), 1)
    lse_all = jnp.zeros((ATTN_TQ, LSE_LANES), F32)
    for h in range(HEADS_PER_GROUP):
        sl = slice(h * HEAD_DIM, (h + 1) * HEAD_DIM)
        s = lax.dot_general(q[:, sl], k[:, sl], (((1,), (1,)), ((), ())), preferred_element_type=F32)
        s = jnp.where(valid, s + bias_ref[h], NEG_INF)
        m = jnp.max(s, axis=-1, keepdims=True)
        p = jnp.exp(s - m)
        l = jnp.sum(p, axis=-1, keepdims=True)
        o = jnp.dot(p.astype(BF16), v[:, sl], preferred_element_type=F32)
        o_ref[:, sl] = (o / l).astype(o_ref.dtype)
        lse_all = jnp.where(lane == h, m + jnp.log(l), lse_all)
    lse_ref[...] = lse_all


def _band_attention(qkv, bias, dil):
    seg = SEQ // dil
    sub = ATTN_TQ // N_SIDE
    last = seg // N_SIDE - 1

    def cur(which):
        return pl.BlockSpec((None, None, ATTN_TQ, GROUP_WIDTH), lambda r, i, w=which: (w, r, i, 0))

    def prev(which):
        return pl.BlockSpec((None, None, N_SIDE, GROUP_WIDTH),
                            lambda r, i, w=which: (w, r, jnp.maximum(i * sub - 1, 0), 0))

    def nxt(which):
        return pl.BlockSpec((None, None, N_SIDE, GROUP_WIDTH),
                            lambda r, i, w=which: (w, r, jnp.minimum((i + 1) * sub, last), 0))

    return pl.pallas_call(
        functools.partial(_attn_kernel, seg=seg),
        grid=(dil, seg // ATTN_TQ),
        in_specs=[cur(0), prev(1), cur(1), nxt(1), prev(2), cur(2), nxt(2),
                  pl.BlockSpec((HEADS_PER_GROUP, ATTN_TQ, ATTN_TK), lambda r, i: (0, 0, 0))],
        out_specs=[pl.BlockSpec((None, ATTN_TQ, GROUP_WIDTH), lambda r, i: (r, i, 0)),
                   pl.BlockSpec((None, ATTN_TQ, LSE_LANES), lambda r, i: (r, i, 0))],
        out_shape=[jax.ShapeDtypeStruct((dil, seg, GROUP_WIDTH), BF16),
                   jax.ShapeDtypeStruct((dil, seg, LSE_LANES), F32)],
        compiler_params=_params(("arbitrary", "arbitrary")),
        name=f"band_attn_d{dil}",
    )(qkv, qkv, qkv, qkv, qkv, qkv, qkv, bias)


def _merge_kernel(o0_ref, o1_ref, o2_ref, l0_ref, l1_ref, l2_ref, out_ref, buf_ref, lbuf_ref):
    tm = out_ref.shape[0]
    for gi, (o_ref, l_ref) in enumerate(((o1_ref, l1_ref), (o2_ref, l2_ref))):
        dil = o_ref.shape[0]
        for r in range(dil):
            rows = pl.ds(r, tm // dil, stride=dil)
            lbuf_ref[gi, rows, :] = l_ref[r]
            for h in range(HEADS_PER_GROUP):
                buf_ref[gi * HEADS_PER_GROUP + h, rows, :] = (
                    o_ref[r, :, h * HEAD_DIM:(h + 1) * HEAD_DIM].astype(F32))
    lse0, lse1, lse2 = l0_ref[0], lbuf_ref[0], lbuf_ref[1]
    top = jnp.maximum(jnp.maximum(lse0, lse1), lse2)
    w0, w1, w2 = jnp.exp(lse0 - top), jnp.exp(lse1 - top), jnp.exp(lse2 - top)
    den = w0 + w1 + w2
    for h in range(HEADS_PER_GROUP):
        sl = slice(h * HEAD_DIM, (h + 1) * HEAD_DIM)
        col = slice(h, h + 1)
        num = (w0[:, col] * o0_ref[0, :, sl].astype(F32) + w1[:, col] * buf_ref[h]
               + w2[:, col] * buf_ref[HEADS_PER_GROUP + h])
        out_ref[:, sl] = (num / den[:, col]).astype(out_ref.dtype)


def _merge_groups(outs, lses):
    tm = 256
    d1, d2 = ATTN_GROUPS[1][1], ATTN_GROUPS[2][1]

    def spec(dil, width):
        return pl.BlockSpec((dil, tm // dil, width), lambda i: (0, i, 0))

    return pl.pallas_call(
        _merge_kernel,
        grid=(SEQ // tm,),
        in_specs=[spec(1, GROUP_WIDTH), spec(d1, GROUP_WIDTH), spec(d2, GROUP_WIDTH),
                  spec(1, LSE_LANES), spec(d1, LSE_LANES), spec(d2, LSE_LANES)],
        out_specs=pl.BlockSpec((tm, GROUP_WIDTH), lambda i: (i, 0)),
        out_shape=jax.ShapeDtypeStruct((SEQ, GROUP_WIDTH), BF16),
        scratch_shapes=[pltpu.VMEM((2 * HEADS_PER_GROUP, tm, HEAD_DIM), F32), pltpu.VMEM((2, tm, LSE_LANES), F32)],
        compiler_params=_params(("arbitrary",)),
        name="attn_merge",
    )(*outs, *lses)


CONV_HALO = 16


def _conv_kernel(up_ref, uc_ref, un_ref, w_ref, b_ref, g_ref, o_ref, buf_ref, y_ref):
    i = pl.program_id(0)
    tm = uc_ref.shape[0]
    first = i == 0
    last = i == pl.num_programs(0) - 1
    buf_ref[0:CONV_HALO, :] = jnp.where(first, 0.0, up_ref[...].astype(F32))
    buf_ref[CONV_HALO:CONV_HALO + tm, :] = uc_ref[...].astype(F32)
    buf_ref[CONV_HALO + tm:, :] = jnp.where(last, 0.0, un_ref[...].astype(F32))
    pad = CONV_KERNEL // 2
    lanes = 256
    for c0 in range(0, CONV_WIDTH, lanes):
        cs = slice(c0, c0 + lanes)
        part = jnp.broadcast_to(b_ref[:, cs], (tm, lanes))
        for t in range(CONV_KERNEL):
            part = part + w_ref[t:t + 1, cs] * buf_ref[pl.ds(CONV_HALO - pad + t, tm), cs]
        y_ref[:, cs] = part
    acc = y_ref[...]
    ms = jnp.mean(acc * acc, axis=-1, keepdims=True)
    y = acc * lax.rsqrt(ms + NORM_EPS) * g_ref[...]
    o_ref[...] = (y * jax.nn.sigmoid(y)).astype(o_ref.dtype)


def _conv_module(u, conv_w, conv_b, conv_norm_g):
    tm = 128
    sub = tm // CONV_HALO
    last = SEQ // CONV_HALO - 1
    return pl.pallas_call(
        _conv_kernel,
        grid=(SEQ // tm,),
        in_specs=[pl.BlockSpec((CONV_HALO, CONV_WIDTH), lambda i: (jnp.maximum(i * sub - 1, 0), 0)),
                  pl.BlockSpec((tm, CONV_WIDTH), lambda i: (i, 0)),
                  pl.BlockSpec((CONV_HALO, CONV_WIDTH), lambda i: (jnp.minimum((i + 1) * sub, last), 0)),
                  pl.BlockSpec((CONV_KERNEL, CONV_WIDTH), lambda i: (0, 0)),
                  pl.BlockSpec((1, CONV_WIDTH), lambda i: (0, 0)),
                  pl.BlockSpec((1, CONV_WIDTH), lambda i: (0, 0))],
        out_specs=pl.BlockSpec((tm, CONV_WIDTH), lambda i: (i, 0)),
        out_shape=jax.ShapeDtypeStruct((SEQ, CONV_WIDTH), BF16),
        scratch_shapes=[pltpu.VMEM((tm + 2 * CONV_HALO, CONV_WIDTH), F32), pltpu.VMEM((tm, CONV_WIDTH), F32)],
        compiler_params=_params(("arbitrary",)),
        name="conv_module",
    )(u, u, u, conv_w, conv_b.reshape(1, CONV_WIDTH), conv_norm_g.reshape(1, CONV_WIDTH))


def _branch_kernel(a_ref, u_ref, wa_ref, wc_ref, ga_ref, gc_ref, o_ref, wab_ref, wcb_ref):
    @pl.when(pl.program_id(1) == 0)
    def _():
        wab_ref[...] = wa_ref[...].astype(BF16)
        wcb_ref[...] = wc_ref[...].astype(BF16)

    attn = jnp.dot(a_ref[...], wab_ref[...], preferred_element_type=F32)
    conv = jnp.dot(u_ref[...], wcb_ref[...], preferred_element_type=F32)
    o_ref[...] = (ga_ref[...].astype(F32) * attn + gc_ref[...].astype(F32) * conv).astype(o_ref.dtype)


def _branch_merge(attn, conv, w_attn_out, w_conv_out, gates):
    tm, tn = 512, 512
    nj = D_MODEL // tn
    return pl.pallas_call(
        _branch_kernel,
        grid=(nj, SEQ // tm),
        in_specs=[pl.BlockSpec((tm, GROUP_WIDTH), lambda j, i: (i, 0)),
                  pl.BlockSpec((tm, CONV_WIDTH), lambda j, i: (i, 0)),
                  pl.BlockSpec((GROUP_WIDTH, tn), lambda j, i: (0, j)),
                  pl.BlockSpec((CONV_WIDTH, tn), lambda j, i: (0, j)),
                  pl.BlockSpec((tm, tn), lambda j, i: (i, j)),
                  pl.BlockSpec((tm, tn), lambda j, i: (i, nj + j))],
        out_specs=pl.BlockSpec((tm, tn), lambda j, i: (i, j)),
        out_shape=jax.ShapeDtypeStruct((SEQ, D_MODEL), BF16),
        scratch_shapes=[pltpu.VMEM((GROUP_WIDTH, tn), BF16), pltpu.VMEM((CONV_WIDTH, tn), BF16)],
        compiler_params=_params(("arbitrary", "arbitrary")),
        name="branch_merge",
    )(attn, conv, w_attn_out, w_conv_out, gates, gates)


def _wo_kernel(m_ref, w_ref, x_ref, g_ref, o_ref, wb_ref):
    @pl.when(pl.program_id(1) == 0)
    def _():
        wb_ref[...] = w_ref[...].astype(BF16)

    y = jnp.dot(m_ref[...], wb_ref[...], preferred_element_type=F32)
    o_ref[...] = x_ref[...] + g_ref[...] * y


def _wo_residual(merged, w_o, x, mod):
    tm, tn = 512, 512
    nj = D_MODEL // tn
    return pl.pallas_call(
        _wo_kernel,
        grid=(nj, SEQ // tm),
        in_specs=[pl.BlockSpec((tm, D_MODEL), lambda j, i: (i, 0)),
                  pl.BlockSpec((D_MODEL, tn), lambda j, i: (0, j)),
                  pl.BlockSpec((tm, tn), lambda j, i: (i, j)),
                  pl.BlockSpec((1, tn), lambda j, i: (0, 2 * nj + j))],
        out_specs=pl.BlockSpec((tm, tn), lambda j, i: (i, j)),
        out_shape=jax.ShapeDtypeStruct((SEQ, D_MODEL), F32),
        scratch_shapes=[pltpu.VMEM((D_MODEL, tn), BF16)],
        compiler_params=_params(("arbitrary", "arbitrary")),
        name="wo_residual",
    )(merged, w_o, x, mod)


def _router_kernel(x_ref, g_ref, sc_ref, sh_ref, wr_ref, br_ref, hp_ref, idx_ref, wt_ref):
    tm = x_ref.shape[0]
    h = _modulated_norm(x_ref[...], g_ref[...], sc_ref[...], sh_ref[...])
    hp_ref[...] = _pack_halves(h)

    h_hi = h.astype(BF16)
    h_lo = (h - h_hi.astype(F32)).astype(BF16)
    w = wr_ref[...]
    w_hi = w.astype(BF16)
    w_lo = (w - w_hi.astype(F32)).astype(BF16)
    logits = (jnp.dot(h_hi, w_hi, preferred_element_type=F32) + jnp.dot(h_lo, w_hi, preferred_element_type=F32)
              + jnp.dot(h_hi, w_lo, preferred_element_type=F32) + br_ref[...])

    lane = lax.broadcasted_iota(jnp.int32, (tm, N_EXPERTS), 1)
    work = logits
    vals, idxs = [], []
    for _ in range(TOP_K):
        m = jnp.max(work, axis=-1, keepdims=True)
        idx = jnp.min(jnp.where(work == m, lane, N_EXPERTS), axis=-1, keepdims=True)
        vals.append(m)
        idxs.append(idx)
        work = jnp.where(lane == idx, -jnp.inf, work)
    exps = [jnp.exp(v - vals[0]) for v in vals]
    den = exps[0] + exps[1] + exps[2] + exps[3]
    col = lax.broadcasted_iota(jnp.int32, (tm, TOP_K), 1)
    idx_out = jnp.zeros((tm, TOP_K), jnp.int32)
    wt_out = jnp.zeros((tm, TOP_K), F32)
    for k in range(TOP_K):
        idx_out = jnp.where(col == k, idxs[k], idx_out)
        wt_out = jnp.where(col == k, exps[k] / den, wt_out)
    idx_ref[...] = idx_out
    wt_ref[...] = wt_out


def _norm2_router(x1, g, mod, w_router, b_router):
    tm = 256
    vec = lambda k: pl.BlockSpec((1, D_MODEL), lambda i, k=k: (0, k))
    return pl.pallas_call(
        _router_kernel,
        grid=(SEQ // tm,),
        in_specs=[pl.BlockSpec((tm, D_MODEL), lambda i: (i, 0)), vec(0), vec(4), vec(3),
                  pl.BlockSpec((D_MODEL, N_EXPERTS), lambda i: (0, 0)),
                  pl.BlockSpec((1, N_EXPERTS), lambda i: (0, 0))],
        out_specs=[pl.BlockSpec((tm, HALF), lambda i: (i, 0)),
                   pl.BlockSpec((tm, TOP_K), lambda i: (i, 0)),
                   pl.BlockSpec((tm, TOP_K), lambda i: (i, 0))],
        out_shape=[jax.ShapeDtypeStruct((SEQ, HALF), U32),
                   jax.ShapeDtypeStruct((SEQ, TOP_K), jnp.int32),
                   jax.ShapeDtypeStruct((SEQ, TOP_K), F32)],
        compiler_params=_params(("arbitrary",)),
        name="norm2_router",
    )(x1, g.reshape(1, D_MODEL), mod, mod, w_router, b_router.reshape(1, N_EXPERTS))


def _routing_tables(top_idx):
    i32 = jnp.int32
    e_flat = top_idx.reshape(-1)
    onehot = (e_flat[:, None] == jnp.arange(N_EXPERTS, dtype=i32)[None, :]).astype(i32)
    csum = jnp.cumsum(onehot, axis=0)
    rank = jnp.sum(onehot * csum, axis=1) - 1
    counts = csum[-1]
    nblk = (counts + SLOT_BLOCK - 1) // SLOT_BLOCK
    blk_end = jnp.cumsum(nblk)
    blk_start = blk_end - nblk
    ntile = (nblk + BLOCKS_PER_TILE - 1) // BLOCKS_PER_TILE
    tile_end = jnp.cumsum(ntile)
    tile_start = tile_end - ntile
    n_tiles = tile_end[-1]
    n_blocks = blk_end[-1]

    dest = (jnp.sum(onehot * (tile_start * TILE_ROWS)[None, :], axis=1) + rank).astype(i32)
    packed = jnp.sum(onehot * (blk_start * SLOT_BLOCK)[None, :], axis=1) + rank
    tok = jnp.arange(N_ASSIGN, dtype=i32) // TOP_K
    slot_tok = jnp.zeros((N_SLOT_BLOCKS * SLOT_BLOCK,), i32).at[packed].set(tok)

    s_raw = jnp.arange(MAX_TILES, dtype=i32)
    s = jnp.minimum(s_raw, n_tiles - 1)
    tile_e = jnp.minimum(jnp.searchsorted(tile_end, s, side='right'), N_EXPERTS - 1).astype(i32)
    local = s - tile_start[tile_e]
    tile_nb = jnp.clip(nblk[tile_e] - BLOCKS_PER_TILE * local, 0, BLOCKS_PER_TILE)
    tile_nb = jnp.where(s_raw < n_tiles, tile_nb, 0).astype(i32)
    tile_b0 = (blk_start[tile_e] + BLOCKS_PER_TILE * local).astype(i32)
    tile_src = s.astype(i32)
    return dest, slot_tok, n_blocks.reshape(1).astype(i32), tile_e, tile_b0, tile_src, tile_nb


def _gather_kernel(nvalid_ref, tok_ref, h_hbm, o_ref, sem):
    def row_copy(r, tok):
        return pltpu.make_async_copy(h_hbm.at[pl.ds(tok, 1), :], o_ref.at[pl.ds(r, 1), :], sem)

    def issue(r, carry):
        row_copy(r, tok_ref[0, 0, r]).start()
        return carry

    def drain(r, carry):
        row_copy(r, 0).wait()
        return carry

    used = pl.program_id(0) < nvalid_ref[0]

    @pl.when(used)
    def _():
        lax.fori_loop(0, SLOT_BLOCK, issue, 0, unroll=8)
        lax.fori_loop(0, SLOT_BLOCK, drain, 0, unroll=8)

    @pl.when(jnp.logical_not(used))
    def _():
        o_ref[...] = jnp.zeros_like(o_ref)


def _gather_tokens(hp, slot_tok, n_blocks):
    grid_spec = pltpu.PrefetchScalarGridSpec(
        num_scalar_prefetch=1,
        grid=(N_SLOT_BLOCKS,),
        in_specs=[pl.BlockSpec((1, 1, SLOT_BLOCK), lambda b, nv: (b, 0, 0), memory_space=pltpu.SMEM),
                  pl.BlockSpec(memory_space=pl.ANY)],
        out_specs=pl.BlockSpec((SLOT_BLOCK, HALF), lambda b, nv: (b, 0)),
        scratch_shapes=[pltpu.SemaphoreType.DMA(())],
    )
    return pl.pallas_call(
        _gather_kernel,
        grid_spec=grid_spec,
        out_shape=jax.ShapeDtypeStruct((N_SLOT_BLOCKS * SLOT_BLOCK, HALF), U32),
        compiler_params=_params(("arbitrary",)),
        name="moe_gather",
    )(n_blocks, slot_tok.reshape(N_SLOT_BLOCKS, 1, SLOT_BLOCK), hp)


UP_TF = 256


def _deinterleave_matrix():
    r = lax.broadcasted_iota(jnp.int32, (256, 256), 0)
    c = lax.broadcasted_iota(jnp.int32, (256, 256), 1)
    src = jnp.where(c < 128, 2 * c, 2 * (c - 128) + 1)
    return jnp.where(r == src, 1.0, 0.0).astype(BF16)


def _up_kernel(e_ref, b0_ref, nb_ref, *refs):
    x_refs = refs[:BLOCKS_PER_TILE]
    w_ref, b_ref, o_ref, wb_ref, xb_ref = refs[BLOCKS_PER_TILE:]
    nb = nb_ref[pl.program_id(0)]

    @pl.when(pl.program_id(1) == 0)
    def _():
        for k in range(BLOCKS_PER_TILE):
            lo, hi = _unpack_halves(x_refs[k][...])
            xb_ref[k, :, :HALF] = lo.astype(BF16)
            xb_ref[k, :, HALF:] = hi.astype(BF16)

    wb_ref[...] = w_ref[...].astype(BF16)
    perm = _deinterleave_matrix()
    for k in range(BLOCKS_PER_TILE):
        rows = slice(k * SLOT_BLOCK, (k + 1) * SLOT_BLOCK)

        @pl.when(k < nb)
        def _(k=k, rows=rows):
            gu = jnp.dot(xb_ref[k], wb_ref[...], preferred_element_type=F32) + b_ref[...]
            for c in range(2 * UP_TF // 256):
                gl = jnp.dot(gu[:, c * 256:(c + 1) * 256].astype(BF16), perm, preferred_element_type=F32)
                glu = jnp.minimum(gl[:, :128], SWIGLU_LIMIT)
                lin = jnp.clip(gl[:, 128:], -SWIGLU_LIMIT, SWIGLU_LIMIT)
                act = glu * jax.nn.sigmoid(SWIGLU_ALPHA * glu) * (lin + 1.0)
                o_ref[rows, c * 128:(c + 1) * 128] = act.astype(BF16)

        @pl.when(k >= nb)
        def _(rows=rows):
            o_ref[rows, :] = jnp.zeros((SLOT_BLOCK, UP_TF), BF16)


def _expert_up(xs, w_up, b_up, tile_e, tile_b0, tile_nb):
    nj = EXPERT_FF // UP_TF

    def x_spec(k):
        return pl.BlockSpec(
            (SLOT_BLOCK, HALF),
            lambda s, j, e, b0, nb, k=k: (b0[s] + jnp.where(k < nb[s], k, 0), 0))

    grid_spec = pltpu.PrefetchScalarGridSpec(
        num_scalar_prefetch=3,
        grid=(MAX_TILES, nj),
        in_specs=[x_spec(k) for k in range(BLOCKS_PER_TILE)] + [
            pl.BlockSpec((None, D_MODEL, 2 * UP_TF), lambda s, j, e, src, nb: (e[s], 0, j)),
            pl.BlockSpec((None, 1, 2 * UP_TF), lambda s, j, e, src, nb: (e[s], 0, j))],
        out_specs=pl.BlockSpec((TILE_ROWS, UP_TF), lambda s, j, e, src, nb: (s, j)),
        scratch_shapes=[pltpu.VMEM((D_MODEL, 2 * UP_TF), BF16),
                        pltpu.VMEM((BLOCKS_PER_TILE, SLOT_BLOCK, D_MODEL), BF16)],
    )
    return pl.pallas_call(
        _up_kernel,
        grid_spec=grid_spec,
        out_shape=jax.ShapeDtypeStruct((MAX_TILES * TILE_ROWS, EXPERT_FF), BF16),
        compiler_params=_params(("arbitrary", "arbitrary")),
        name="expert_up",
    )(tile_e, tile_b0, tile_nb, *([xs] * BLOCKS_PER_TILE), w_up,
      b_up.reshape(N_EXPERTS, 1, 2 * EXPERT_FF))


DOWN_TN = 512


def _down_kernel(e_ref, src_ref, nb_ref, a_ref, wl_ref, wh_ref, bl_ref, bh_ref, o_ref, wlb_ref, whb_ref):
    nb = nb_ref[pl.program_id(0)]
    wlb_ref[...] = wl_ref[...].astype(BF16)
    whb_ref[...] = wh_ref[...].astype(BF16)
    for k in range(BLOCKS_PER_TILE):
        rows = slice(k * SLOT_BLOCK, (k + 1) * SLOT_BLOCK)

        @pl.when(k < nb)
        def _(rows=rows):
            a = a_ref[rows, :]
            y_lo = jnp.dot(a, wlb_ref[...], preferred_element_type=F32) + bl_ref[...]
            y_hi = jnp.dot(a, whb_ref[...], preferred_element_type=F32) + bh_ref[...]
            o_ref[rows, :] = _pack_halves(jnp.concatenate([y_lo, y_hi], axis=1))

        @pl.when(k >= nb)
        def _(rows=rows):
            o_ref[rows, :] = jnp.zeros((SLOT_BLOCK, DOWN_TN), U32)


def _expert_down(act, w_down, b_down, tile_e, tile_src, tile_nb):
    nj = HALF // DOWN_TN
    grid_spec = pltpu.PrefetchScalarGridSpec(
        num_scalar_prefetch=3,
        grid=(MAX_TILES, nj),
        in_specs=[pl.BlockSpec((TILE_ROWS, EXPERT_FF), lambda s, j, e, src, nb: (src[s], 0)),
                  pl.BlockSpec((None, EXPERT_FF, DOWN_TN), lambda s, j, e, src, nb: (e[s], 0, j)),
                  pl.BlockSpec((None, EXPERT_FF, DOWN_TN), lambda s, j, e, src, nb: (e[s], 0, nj + j)),
                  pl.BlockSpec((None, 1, DOWN_TN), lambda s, j, e, src, nb: (e[s], 0, j)),
                  pl.BlockSpec((None, 1, DOWN_TN), lambda s, j, e, src, nb: (e[s], 0, nj + j))],
        out_specs=pl.BlockSpec((TILE_ROWS, DOWN_TN), lambda s, j, e, src, nb: (s, j)),
        scratch_shapes=[pltpu.VMEM((EXPERT_FF, DOWN_TN), BF16), pltpu.VMEM((EXPERT_FF, DOWN_TN), BF16)],
    )
    b3 = b_down.reshape(N_EXPERTS, 1, D_MODEL)
    return pl.pallas_call(
        _down_kernel,
        grid_spec=grid_spec,
        out_shape=jax.ShapeDtypeStruct((MAX_TILES * TILE_ROWS, HALF), U32),
        compiler_params=_params(("arbitrary", "arbitrary")),
        name="expert_down",
    )(tile_e, tile_src, tile_nb, act, w_down, w_down, b3, b3)


COMBINE_TM = 64


def _combine_kernel(dest_ref, y_hbm, x_ref, wt_ref, g_ref, o_ref, buf_ref, sem):
    def row_copy(t, k, slot):
        return pltpu.make_async_copy(y_hbm.at[pl.ds(slot, 1), :], buf_ref.at[k, pl.ds(t, 1), :], sem)

    def issue(t, carry):
        for k in range(TOP_K):
            row_copy(t, k, dest_ref[0, 0, t * TOP_K + k]).start()
        return carry

    def drain(t, carry):
        for k in range(TOP_K):
            row_copy(t, k, 0).wait()
        return carry

    lax.fori_loop(0, COMBINE_TM, issue, 0, unroll=4)
    lax.fori_loop(0, COMBINE_TM, drain, 0, unroll=4)

    acc_lo = jnp.zeros((COMBINE_TM, HALF), F32)
    acc_hi = jnp.zeros((COMBINE_TM, HALF), F32)
    for k in range(TOP_K):
        lo, hi = _unpack_halves(buf_ref[k])
        w = wt_ref[:, k:k + 1]
        acc_lo = acc_lo + w * lo
        acc_hi = acc_hi + w * hi
    o_ref[:, :HALF] = x_ref[:, :HALF] + g_ref[:, :HALF] * acc_lo
    o_ref[:, HALF:] = x_ref[:, HALF:] + g_ref[:, HALF:] * acc_hi


def _combine(y_slots, dest, x1, top_w, mod):
    nt = SEQ // COMBINE_TM
    return pl.pallas_call(
        _combine_kernel,
        grid=(nt,),
        in_specs=[pl.BlockSpec((1, 1, COMBINE_TM * TOP_K), lambda i: (i, 0, 0), memory_space=pltpu.SMEM),
                  pl.BlockSpec(memory_space=pl.ANY),
                  pl.BlockSpec((COMBINE_TM, D_MODEL), lambda i: (i, 0)),
                  pl.BlockSpec((COMBINE_TM, TOP_K), lambda i: (i, 0)),
                  pl.BlockSpec((1, D_MODEL), lambda i: (0, 5))],
        out_specs=pl.BlockSpec((COMBINE_TM, D_MODEL), lambda i: (i, 0)),
        out_shape=jax.ShapeDtypeStruct((SEQ, D_MODEL), F32),
        scratch_shapes=[pltpu.VMEM((TOP_K, COMBINE_TM, HALF), U32), pltpu.SemaphoreType.DMA(())],
        compiler_params=_params(("arbitrary",)),
        name="moe_combine",
    )(dest.reshape(nt, 1, COMBINE_TM * TOP_K), y_slots, x1, top_w, mod)


def _layer(x, c, norm1_g, norm2_g, w_ada, b_ada, w_in, q_norm_g, k_norm_g, rel_bias, conv_w, conv_b,
           conv_norm_g, w_attn_out, w_conv_out, w_o, w_router, b_router, w_up, b_up, w_down, b_down):
    mod = _ada(c, w_ada, b_ada)
    h = _norm1(x, norm1_g, mod)

    outs, lses = [], []
    for group, (_, dil) in enumerate(ATTN_GROUPS):
        qkv = _qkv_proj(h, w_in, q_norm_g, k_norm_g, group, dil)
        o, lse = _band_attention(qkv, _bias_tiles(rel_bias, group, dil), dil)
        outs.append(o)
        lses.append(lse)
    attn = _merge_groups(outs, lses)

    u = _conv_module(_conv_glu_proj(h, w_in), conv_w, conv_b, conv_norm_g)
    gates = _gate_proj(h, w_in)
    merged = _branch_merge(attn, u, w_attn_out, w_conv_out, gates)
    x1 = _wo_residual(merged, w_o, x, mod)

    hp, top_idx, top_w = _norm2_router(x1, norm2_g, mod, w_router, b_router)
    dest, slot_tok, n_blocks, tile_e, tile_b0, tile_src, tile_nb = _routing_tables(top_idx)
    xs = _gather_tokens(hp, slot_tok, n_blocks)
    act = _expert_up(xs, w_up, b_up, tile_e, tile_b0, tile_nb)
    y_slots = _expert_down(act, w_down, b_down, tile_e, tile_src, tile_nb)
    return _combine(y_slots, dest, x1, top_w, mod)


def kernel(x, c, norm1_g, norm2_g, w_ada, b_ada, w_in, q_norm_g, k_norm_g, rel_bias, conv_w, conv_b,
           conv_norm_g, w_attn_out, w_conv_out, w_o, w_router, b_router, w_up, b_up, w_down, b_down):
    batch = x.shape[0]
    xs = x.reshape(batch * SEQ, D_MODEL)
    for layer in range(w_in.shape[0]):
        xs = _layer(xs, c, norm1_g[layer], norm2_g[layer], w_ada[layer], b_ada[layer], w_in[layer],
                    q_norm_g[layer], k_norm_g[layer], rel_bias, conv_w[layer], conv_b[layer],
                    conv_norm_g[layer], w_attn_out[layer], w_conv_out[layer], w_o[layer],
                    w_router[layer], b_router[layer], w_up[layer], b_up[layer], w_down[layer], b_down[layer])
    return xs.reshape(x.shape)
```

```python
import functools
import math

import numpy as np
import jax
import jax.numpy as jnp
from jax import lax
from jax.experimental import pallas as pl
from jax.experimental.pallas import tpu as pltpu

D_MODEL = 4096
SEQ = 8192
HEAD_DIM = 128
ATTN_GROUPS = ((128, 1), (512, 4), (2048, 16))
N_GROUPS = len(ATTN_GROUPS)
HEADS_PER_GROUP = 8
GROUP_WIDTH = HEADS_PER_GROUP * HEAD_DIM
ATTN_WIDTH = N_GROUPS * GROUP_WIDTH
CONV_WIDTH = D_MODEL // 2
CONV_KERNEL = 31
CONV_OFF = 3 * ATTN_WIDTH
GATE_OFF = CONV_OFF + 2 * CONV_WIDTH
IN_WIDTH = GATE_OFF + 2 * D_MODEL
REL_BUCKETS = 32
REL_MAX_DISTANCE = 1024
N_EXPERTS = 32
TOP_K = 4
EXPERT_FF = 1536
SWIGLU_ALPHA = 1.702
SWIGLU_LIMIT = 7.0
NORM_EPS = 1e-6
NEG_INF = -1e30
N_SIDE = 64

V7X_VMEM_LIMIT = 56 * 1024 * 1024

SLOT_BLOCK = 256
BLOCKS_PER_TILE = 5
N_ASSIGN = SEQ * TOP_K
TILE_ROWS = SLOT_BLOCK * BLOCKS_PER_TILE
N_SLOT_BLOCKS = (N_ASSIGN + N_EXPERTS * (SLOT_BLOCK - 1)) // SLOT_BLOCK + 1
MAX_TILES = (N_SLOT_BLOCKS + N_EXPERTS * (BLOCKS_PER_TILE - 1)) // BLOCKS_PER_TILE
HALF = D_MODEL // 2

F32 = jnp.float32
BF16 = jnp.bfloat16
U32 = jnp.uint32


def _params(sem, vmem=V7X_VMEM_LIMIT):
    return pltpu.CompilerParams(dimension_semantics=sem, vmem_limit_bytes=vmem)


def _pack_halves(v):
    n = v.shape[-1] // 2
    lo = pltpu.bitcast(v[:, :n].astype(BF16).astype(F32), U32) >> 16
    hi = pltpu.bitcast(v[:, n:].astype(BF16).astype(F32), U32) & jnp.uint32(0xFFFF0000)
    return lo | hi


def _unpack_halves(p):
    lo = pltpu.bitcast(p << 16, F32)
    hi = pltpu.bitcast(p & jnp.uint32(0xFFFF0000), F32)
    return lo, hi


TOKEN_TILE = 8


def _store_token_tiles(ref, row0, rows, packed):
    for half in range(2):
        for c in range(TOKEN_TILE):
            col = (half * TOKEN_TILE + c) * 128
            ref[half, pl.ds(row0 * TOKEN_TILE + c, rows, stride=TOKEN_TILE), :] = packed[:, col:col + 128]


def _load_token_tile_column(ref, lead, row0, rows, half, c):
    return ref[(*lead, half, pl.ds(row0 * TOKEN_TILE + c, rows, stride=TOKEN_TILE), slice(None))]


def _ada_kernel(c_ref, w_ref, b_ref, o_ref, cs_ref):
    @pl.when(pl.program_id(0) == 0)
    def _():
        c = c_ref[...]
        cs_ref[...] = c * jax.nn.sigmoid(c)

    tn = o_ref.shape[-1]
    rows_per_step = 64

    def body(r, acc):
        rows = pl.ds(pl.multiple_of(r * rows_per_step, rows_per_step), rows_per_step)
        cs = cs_ref[rows, :]
        prod = w_ref[rows, :] * jnp.concatenate([cs] * (tn // 128), axis=1)
        return acc + prod.reshape(rows_per_step // 8, 8, tn).sum(axis=0)

    acc = lax.fori_loop(0, D_MODEL // rows_per_step, body, jnp.zeros((8, tn), F32))
    o_ref[...] = jnp.sum(acc, axis=0, keepdims=True) + b_ref[...]


def _ada(c, w_ada, b_ada):
    tn = 512
    n = w_ada.shape[-1]
    c_lanes = jnp.broadcast_to(c.reshape(D_MODEL, 1), (D_MODEL, 128))
    return pl.pallas_call(
        _ada_kernel,
        grid=(n // tn,),
        in_specs=[pl.BlockSpec((D_MODEL, 128), lambda j: (0, 0)),
                  pl.BlockSpec((D_MODEL, tn), lambda j: (0, j)),
                  pl.BlockSpec((1, tn), lambda j: (0, j))],
        out_specs=pl.BlockSpec((1, tn), lambda j: (0, j)),
        out_shape=jax.ShapeDtypeStruct((1, n), F32),
        scratch_shapes=[pltpu.VMEM((D_MODEL, 128), F32)],
        compiler_params=_params(("arbitrary",)),
        name="ada_mod",
    )(c_lanes, w_ada, b_ada.reshape(1, n))


def _modulated_norm(x, g, scale, shift):
    ms = jnp.mean(x * x, axis=-1, keepdims=True)
    y = x * lax.rsqrt(ms + NORM_EPS) * g
    return y * (1.0 + scale) + shift


def _norm1_kernel(x_ref, g_ref, sc_ref, sh_ref, o_ref):
    o_ref[...] = _modulated_norm(x_ref[...], g_ref[...], sc_ref[...], sh_ref[...]).astype(o_ref.dtype)


def _norm1(x, g, mod):
    tm = 256
    vec = lambda k: pl.BlockSpec((1, D_MODEL), lambda i, k=k: (0, k))
    return pl.pallas_call(
        _norm1_kernel,
        grid=(SEQ // tm,),
        in_specs=[pl.BlockSpec((tm, D_MODEL), lambda i: (i, 0)), vec(0), vec(1), vec(0)],
        out_specs=pl.BlockSpec((tm, D_MODEL), lambda i: (i, 0)),
        out_shape=jax.ShapeDtypeStruct((SEQ, D_MODEL), BF16),
        compiler_params=_params(("arbitrary",)),
        name="norm1",
    )(x, g.reshape(1, D_MODEL), mod, mod)


def _head_norm(acc, gain, post_scale):
    outs = []
    for h in range(acc.shape[-1] // HEAD_DIM):
        blk = acc[:, h * HEAD_DIM:(h + 1) * HEAD_DIM]
        ms = jnp.mean(blk * blk, axis=-1, keepdims=True)
        outs.append(blk * lax.rsqrt(ms + NORM_EPS) * (gain * post_scale))
    return jnp.concatenate(outs, axis=1)


def _qkv_kernel(h_ref, w_ref, gq_ref, gk_ref, o_ref, wb_ref, acc_ref, *, dil):
    which = pl.program_id(0)
    tm = h_ref.shape[0]

    @pl.when(pl.program_id(1) == 0)
    def _():
        wb_ref[...] = w_ref[...].astype(BF16)

    acc = jnp.dot(h_ref[...], wb_ref[...], preferred_element_type=F32)

    def write(vals):
        if dil == 1:
            o_ref[0] = vals.astype(BF16)
        else:
            for h in range(HEADS_PER_GROUP):
                sl = slice(h * HEAD_DIM, (h + 1) * HEAD_DIM)
                acc_ref[h] = vals[:, sl]
                for r in range(dil):
                    o_ref[r, :, sl] = acc_ref[h, pl.ds(r, tm // dil, stride=dil), :].astype(BF16)

    @pl.when(which == 0)
    def _():
        write(_head_norm(acc, gq_ref[...], HEAD_DIM ** -0.5))

    @pl.when(which == 1)
    def _():
        write(_head_norm(acc, gk_ref[...], 1.0))

    @pl.when(which == 2)
    def _():
        write(acc)


def _qkv_proj(h, w_in, gq, gk, group, dil):
    tm = 512
    seg = SEQ // dil
    return pl.pallas_call(
        functools.partial(_qkv_kernel, dil=dil),
        grid=(3, SEQ // tm),
        in_specs=[pl.BlockSpec((tm, D_MODEL), lambda w, i: (i, 0)),
                  pl.BlockSpec((D_MODEL, GROUP_WIDTH), lambda w, i: (0, w * N_GROUPS + group),
                               pipeline_mode=pl.Buffered(1)),
                  pl.BlockSpec((1, HEAD_DIM), lambda w, i: (0, 0)),
                  pl.BlockSpec((1, HEAD_DIM), lambda w, i: (0, 0))],
        out_specs=pl.BlockSpec((None, dil, tm // dil, GROUP_WIDTH), lambda w, i: (w, 0, i, 0)),
        out_shape=jax.ShapeDtypeStruct((3, dil, seg, GROUP_WIDTH), BF16),
        scratch_shapes=[pltpu.VMEM((D_MODEL, GROUP_WIDTH), BF16), pltpu.VMEM((HEADS_PER_GROUP, tm, HEAD_DIM), F32)],
        compiler_params=_params(("arbitrary", "arbitrary")),
        name=f"qkv_proj_g{group}",
    )(h, w_in, gq.reshape(1, HEAD_DIM), gk.reshape(1, HEAD_DIM))


def _glu_kernel(h_ref, wa_ref, wg_ref, o_ref, wab_ref, wgb_ref):
    @pl.when(pl.program_id(1) == 0)
    def _():
        wab_ref[...] = wa_ref[...].astype(BF16)
        wgb_ref[...] = wg_ref[...].astype(BF16)

    h = h_ref[...]
    a = jnp.dot(h, wab_ref[...], preferred_element_type=F32)
    g = jnp.dot(h, wgb_ref[...], preferred_element_type=F32)
    o_ref[...] = (a * jax.nn.sigmoid(g)).astype(o_ref.dtype)


def _conv_glu_proj(h, w_in):
    tm, tn = 512, 512
    a0 = CONV_OFF // tn
    g0 = (CONV_OFF + CONV_WIDTH) // tn
    return pl.pallas_call(
        _glu_kernel,
        grid=(CONV_WIDTH // tn, SEQ // tm),
        in_specs=[pl.BlockSpec((tm, D_MODEL), lambda j, i: (i, 0)),
                  pl.BlockSpec((D_MODEL, tn), lambda j, i: (0, a0 + j)),
                  pl.BlockSpec((D_MODEL, tn), lambda j, i: (0, g0 + j))],
        out_specs=pl.BlockSpec((tm, tn), lambda j, i: (i, j)),
        out_shape=jax.ShapeDtypeStruct((SEQ, CONV_WIDTH), BF16),
        scratch_shapes=[pltpu.VMEM((D_MODEL, tn), BF16), pltpu.VMEM((D_MODEL, tn), BF16)],
        compiler_params=_params(("arbitrary", "arbitrary")),
        name="conv_glu_proj",
    )(h, w_in, w_in)


def _gate_kernel(h_ref, w_ref, o_ref, wb_ref):
    @pl.when(pl.program_id(1) == 0)
    def _():
        wb_ref[...] = w_ref[...].astype(BF16)

    acc = jnp.dot(h_ref[...], wb_ref[...], preferred_element_type=F32)
    o_ref[...] = jax.nn.sigmoid(acc).astype(o_ref.dtype)


def _gate_proj(h, w_in):
    tm, tn = 1024, 512
    c0 = GATE_OFF // tn
    return pl.pallas_call(
        _gate_kernel,
        grid=(2 * D_MODEL // tn, SEQ // tm),
        in_specs=[pl.BlockSpec((tm, D_MODEL), lambda j, i: (i, 0)),
                  pl.BlockSpec((D_MODEL, tn), lambda j, i: (0, c0 + j))],
        out_specs=pl.BlockSpec((tm, tn), lambda j, i: (i, j)),
        out_shape=jax.ShapeDtypeStruct((SEQ, 2 * D_MODEL), BF16),
        scratch_shapes=[pltpu.VMEM((D_MODEL, tn), BF16)],
        compiler_params=_params(("arbitrary", "arbitrary")),
        name="gate_proj",
    )(h, w_in)


ATTN_TQ = 128
ATTN_TK = ATTN_TQ + 2 * N_SIDE
LSE_LANES = 128


def _bucket_tile(dil):
    a = np.arange(ATTN_TQ)[:, None]
    j = np.arange(ATTN_TK)[None, :]
    steps = j - N_SIDE - a
    rel = steps * dil
    nb = REL_BUCKETS // 2
    max_exact = nb // 2
    n = np.abs(rel)
    side = np.where(rel > 0, nb, 0)
    nf = np.maximum(n, 1).astype(np.float32)
    large = max_exact + (np.log(nf / np.float32(max_exact)) / np.float32(math.log(REL_MAX_DISTANCE / max_exact))
                         * np.float32(nb - max_exact)).astype(np.int32)
    large = np.minimum(large, nb - 1)
    bucket = side + np.where(n < max_exact, n, large)
    return np.where(np.abs(steps) <= N_SIDE, bucket, -1).astype(np.int32)


def _bias_kernel(tbl_ref, idx_ref, o_ref, *, group):
    h = pl.program_id(0)
    idx = idx_ref[...]
    acc = jnp.full(idx.shape, NEG_INF, F32)
    for b in range(REL_BUCKETS):
        acc = jnp.where(idx == b, tbl_ref[b, group * HEADS_PER_GROUP + h], acc)
    o_ref[...] = acc


def _bias_tiles(rel_bias, group, dil):
    return pl.pallas_call(
        functools.partial(_bias_kernel, group=group),
        grid=(HEADS_PER_GROUP,),
        in_specs=[pl.BlockSpec(memory_space=pltpu.SMEM),
                  pl.BlockSpec((ATTN_TQ, ATTN_TK), lambda h: (0, 0))],
        out_specs=pl.BlockSpec((None, ATTN_TQ, ATTN_TK), lambda h: (h, 0, 0)),
        out_shape=jax.ShapeDtypeStruct((HEADS_PER_GROUP, ATTN_TQ, ATTN_TK), F32),
        compiler_params=_params(("arbitrary",)),
        name=f"attn_bias_g{group}",
    )(rel_bias, jnp.asarray(_bucket_tile(dil)))


def _attn_kernel(q_ref, kp_ref, kc_ref, kn_ref, vp_ref, vc_ref, vn_ref, bias_ref, o_ref, lse_ref, *, seg):
    i = pl.program_id(1)
    q = q_ref[...]
    k = jnp.concatenate([kp_ref[...], kc_ref[...], kn_ref[...]], axis=0)
    v = jnp.concatenate([vp_ref[...], vc_ref[...], vn_ref[...]], axis=0)
    kpos = i * ATTN_TQ - N_SIDE + lax.broadcasted_iota(jnp.int32, (1, ATTN_TK), 1)
    valid = (kpos >= 0) & (kpos < seg)
    lane = lax.broadcasted_iota(jnp.int32, (ATTN_TQ, LSE_LANES), 1)
    lse_all = jnp.zeros((ATTN_TQ, LSE_LANES), F32)
    for h in range(HEADS_PER_GROUP):
        sl = slice(h * HEAD_DIM, (h + 1) * HEAD_DIM)
        s = lax.dot_general(q[:, sl], k[:, sl], (((1,), (1,)), ((), ())), preferred_element_type=F32)
        s = jnp.where(valid, s + bias_ref[h], NEG_INF)
        m = jnp.max(s, axis=-1, keepdims=True)
        p = jnp.exp(s - m)
        l = jnp.sum(p, axis=-1, keepdims=True)
        o = jnp.dot(p.astype(BF16), v[:, sl], preferred_element_type=F32)
        o_ref[:, sl] = (o / l).astype(o_ref.dtype)
        lse_all = jnp.where(lane == h, m + jnp.log(l), lse_all)
    lse_ref[...] = lse_all


def _band_attention(qkv, bias, dil):
    seg = SEQ // dil
    sub = ATTN_TQ // N_SIDE
    last = seg // N_SIDE - 1

    def cur(which):
        return pl.BlockSpec((None, None, ATTN_TQ, GROUP_WIDTH), lambda r, i, w=which: (w, r, i, 0))

    def prev(which):
        return pl.BlockSpec((None, None, N_SIDE, GROUP_WIDTH),
                            lambda r, i, w=which: (w, r, jnp.maximum(i * sub - 1, 0), 0))

    def nxt(which):
        return pl.BlockSpec((None, None, N_SIDE, GROUP_WIDTH),
                            lambda r, i, w=which: (w, r, jnp.minimum((i + 1) * sub, last), 0))

    return pl.pallas_call(
        functools.partial(_attn_kernel, seg=seg),
        grid=(dil, seg // ATTN_TQ),
        in_specs=[cur(0), prev(1), cur(1), nxt(1), prev(2), cur(2), nxt(2),
                  pl.BlockSpec((HEADS_PER_GROUP, ATTN_TQ, ATTN_TK), lambda r, i: (0, 0, 0))],
        out_specs=[pl.BlockSpec((None, ATTN_TQ, GROUP_WIDTH), lambda r, i: (r, i, 0)),
                   pl.BlockSpec((None, ATTN_TQ, LSE_LANES), lambda r, i: (r, i, 0))],
        out_shape=[jax.ShapeDtypeStruct((dil, seg, GROUP_WIDTH), BF16),
                   jax.ShapeDtypeStruct((dil, seg, LSE_LANES), F32)],
        compiler_params=_params(("arbitrary", "arbitrary")),
        name=f"band_attn_d{dil}",
    )(qkv, qkv, qkv, qkv, qkv, qkv, qkv, bias)


def _merge_kernel(o0_ref, o1_ref, o2_ref, l0_ref, l1_ref, l2_ref, out_ref, buf_ref, lbuf_ref):
    tm = out_ref.shape[0]
    for gi, (o_ref, l_ref) in enumerate(((o1_ref, l1_ref), (o2_ref, l2_ref))):
        dil = o_ref.shape[0]
        for r in range(dil):
            rows = pl.ds(r, tm // dil, stride=dil)
            lbuf_ref[gi, rows, :] = l_ref[r]
            for h in range(HEADS_PER_GROUP):
                buf_ref[gi * HEADS_PER_GROUP + h, rows, :] = (
                    o_ref[r, :, h * HEAD_DIM:(h + 1) * HEAD_DIM].astype(F32))
    lse0, lse1, lse2 = l0_ref[0], lbuf_ref[0], lbuf_ref[1]
    top = jnp.maximum(jnp.maximum(lse0, lse1), lse2)
    w0, w1, w2 = jnp.exp(lse0 - top), jnp.exp(lse1 - top), jnp.exp(lse2 - top)
    den = w0 + w1 + w2
    for h in range(HEADS_PER_GROUP):
        sl = slice(h * HEAD_DIM, (h + 1) * HEAD_DIM)
        col = slice(h, h + 1)
        num = (w0[:, col] * o0_ref[0, :, sl].astype(F32) + w1[:, col] * buf_ref[h]
               + w2[:, col] * buf_ref[HEADS_PER_GROUP + h])
        out_ref[:, sl] = (num / den[:, col]).astype(out_ref.dtype)


def _merge_groups(outs, lses):
    tm = 256
    d1, d2 = ATTN_GROUPS[1][1], ATTN_GROUPS[2][1]

    def spec(dil, width):
        return pl.BlockSpec((dil, tm // dil, width), lambda i: (0, i, 0))

    return pl.pallas_call(
        _merge_kernel,
        grid=(SEQ // tm,),
        in_specs=[spec(1, GROUP_WIDTH), spec(d1, GROUP_WIDTH), spec(d2, GROUP_WIDTH),
                  spec(1, LSE_LANES), spec(d1, LSE_LANES), spec(d2, LSE_LANES)],
        out_specs=pl.BlockSpec((tm, GROUP_WIDTH), lambda i: (i, 0)),
        out_shape=jax.ShapeDtypeStruct((SEQ, GROUP_WIDTH), BF16),
        scratch_shapes=[pltpu.VMEM((2 * HEADS_PER_GROUP, tm, HEAD_DIM), F32), pltpu.VMEM((2, tm, LSE_LANES), F32)],
        compiler_params=_params(("arbitrary",)),
        name="attn_merge",
    )(*outs, *lses)


CONV_HALO = 16


def _conv_kernel(up_ref, uc_ref, un_ref, w_ref, b_ref, g_ref, o_ref, buf_ref, y_ref):
    i = pl.program_id(0)
    tm = uc_ref.shape[0]
    first = i == 0
    last = i == pl.num_programs(0) - 1
    buf_ref[0:CONV_HALO, :] = jnp.where(first, 0.0, up_ref[...].astype(F32))
    buf_ref[CONV_HALO:CONV_HALO + tm, :] = uc_ref[...].astype(F32)
    buf_ref[CONV_HALO + tm:, :] = jnp.where(last, 0.0, un_ref[...].astype(F32))
    pad = CONV_KERNEL // 2
    lanes = 256
    for c0 in range(0, CONV_WIDTH, lanes):
        cs = slice(c0, c0 + lanes)
        part = jnp.broadcast_to(b_ref[:, cs], (tm, lanes))
        first = CONV_HALO - pad
        for s in range(8):
            group = None
            for tap in range(CONV_KERNEL):
                off = first + tap
                if off % 8 != s:
                    continue
                term = w_ref[tap:tap + 1, cs] * buf_ref[pl.ds(off - s, tm + 8), cs]
                group = term if group is None else group + term
            part = part + group[s:s + tm, :]
        y_ref[:, cs] = part
    acc = y_ref[...]
    ms = jnp.mean(acc * acc, axis=-1, keepdims=True)
    y = acc * lax.rsqrt(ms + NORM_EPS) * g_ref[...]
    o_ref[...] = (y * jax.nn.sigmoid(y)).astype(o_ref.dtype)


def _conv_module(u, conv_w, conv_b, conv_norm_g):
    tm = 128
    sub = tm // CONV_HALO
    last = SEQ // CONV_HALO - 1
    return pl.pallas_call(
        _conv_kernel,
        grid=(SEQ // tm,),
        in_specs=[pl.BlockSpec((CONV_HALO, CONV_WIDTH), lambda i: (jnp.maximum(i * sub - 1, 0), 0)),
                  pl.BlockSpec((tm, CONV_WIDTH), lambda i: (i, 0)),
                  pl.BlockSpec((CONV_HALO, CONV_WIDTH), lambda i: (jnp.minimum((i + 1) * sub, last), 0)),
                  pl.BlockSpec((CONV_KERNEL, CONV_WIDTH), lambda i: (0, 0)),
                  pl.BlockSpec((1, CONV_WIDTH), lambda i: (0, 0)),
                  pl.BlockSpec((1, CONV_WIDTH), lambda i: (0, 0))],
        out_specs=pl.BlockSpec((tm, CONV_WIDTH), lambda i: (i, 0)),
        out_shape=jax.ShapeDtypeStruct((SEQ, CONV_WIDTH), BF16),
        scratch_shapes=[pltpu.VMEM((tm + 2 * CONV_HALO, CONV_WIDTH), F32), pltpu.VMEM((tm, CONV_WIDTH), F32)],
        compiler_params=_params(("arbitrary",)),
        name="conv_module",
    )(u, u, u, conv_w, conv_b.reshape(1, CONV_WIDTH), conv_norm_g.reshape(1, CONV_WIDTH))


def _branch_kernel(a_ref, u_ref, wa_ref, wc_ref, ga_ref, gc_ref, o_ref, wab_ref, wcb_ref):
    @pl.when(pl.program_id(1) == 0)
    def _():
        wab_ref[...] = wa_ref[...].astype(BF16)
        wcb_ref[...] = wc_ref[...].astype(BF16)

    attn = jnp.dot(a_ref[...], wab_ref[...], preferred_element_type=F32)
    conv = jnp.dot(u_ref[...], wcb_ref[...], preferred_element_type=F32)
    o_ref[...] = (ga_ref[...].astype(F32) * attn + gc_ref[...].astype(F32) * conv).astype(o_ref.dtype)


def _branch_merge(attn, conv, w_attn_out, w_conv_out, gates):
    tm, tn = 512, 512
    nj = D_MODEL // tn
    return pl.pallas_call(
        _branch_kernel,
        grid=(nj, SEQ // tm),
        in_specs=[pl.BlockSpec((tm, GROUP_WIDTH), lambda j, i: (i, 0)),
                  pl.BlockSpec((tm, CONV_WIDTH), lambda j, i: (i, 0)),
                  pl.BlockSpec((GROUP_WIDTH, tn), lambda j, i: (0, j)),
                  pl.BlockSpec((CONV_WIDTH, tn), lambda j, i: (0, j)),
                  pl.BlockSpec((tm, tn), lambda j, i: (i, j)),
                  pl.BlockSpec((tm, tn), lambda j, i: (i, nj + j))],
        out_specs=pl.BlockSpec((tm, tn), lambda j, i: (i, j)),
        out_shape=jax.ShapeDtypeStruct((SEQ, D_MODEL), BF16),
        scratch_shapes=[pltpu.VMEM((GROUP_WIDTH, tn), BF16), pltpu.VMEM((CONV_WIDTH, tn), BF16)],
        compiler_params=_params(("arbitrary", "arbitrary")),
        name="branch_merge",
    )(attn, conv, w_attn_out, w_conv_out, gates, gates)


def _wo_kernel(m_ref, w_ref, x_ref, g_ref, o_ref, wb_ref):
    @pl.when(pl.program_id(1) == 0)
    def _():
        wb_ref[...] = w_ref[...].astype(BF16)

    y = jnp.dot(m_ref[...], wb_ref[...], preferred_element_type=F32)
    o_ref[...] = x_ref[...] + g_ref[...] * y


def _wo_residual(merged, w_o, x, mod):
    tm, tn = 1024, 512
    nj = D_MODEL // tn
    return pl.pallas_call(
        _wo_kernel,
        grid=(nj, SEQ // tm),
        in_specs=[pl.BlockSpec((tm, D_MODEL), lambda j, i: (i, 0)),
                  pl.BlockSpec((D_MODEL, tn), lambda j, i: (0, j)),
                  pl.BlockSpec((tm, tn), lambda j, i: (i, j)),
                  pl.BlockSpec((1, tn), lambda j, i: (0, 2 * nj + j))],
        out_specs=pl.BlockSpec((tm, tn), lambda j, i: (i, j)),
        out_shape=jax.ShapeDtypeStruct((SEQ, D_MODEL), F32),
        scratch_shapes=[pltpu.VMEM((D_MODEL, tn), BF16)],
        compiler_params=_params(("arbitrary", "arbitrary")),
        name="wo_residual",
    )(merged, w_o, x, mod)


def _router_kernel(x_ref, g_ref, sc_ref, sh_ref, wr_ref, br_ref, hp_ref, idx_ref, wt_ref):
    tm = x_ref.shape[0]
    h = _modulated_norm(x_ref[...], g_ref[...], sc_ref[...], sh_ref[...])
    _store_token_tiles(hp_ref, 0, tm, _pack_halves(h))

    h_hi = h.astype(BF16)
    h_lo = (h - h_hi.astype(F32)).astype(BF16)
    w = wr_ref[...]
    w_hi = w.astype(BF16)
    w_lo = (w - w_hi.astype(F32)).astype(BF16)
    logits = (jnp.dot(h_hi, w_hi, preferred_element_type=F32) + jnp.dot(h_lo, w_hi, preferred_element_type=F32)
              + jnp.dot(h_hi, w_lo, preferred_element_type=F32) + br_ref[...])

    lane = lax.broadcasted_iota(jnp.int32, (tm, N_EXPERTS), 1)
    work = logits
    vals, idxs = [], []
    for _ in range(TOP_K):
        m = jnp.max(work, axis=-1, keepdims=True)
        idx = jnp.min(jnp.where(work == m, lane, N_EXPERTS), axis=-1, keepdims=True)
        vals.append(m)
        idxs.append(idx)
        work = jnp.where(lane == idx, -jnp.inf, work)
    exps = [jnp.exp(v - vals[0]) for v in vals]
    den = exps[0] + exps[1] + exps[2] + exps[3]
    col = lax.broadcasted_iota(jnp.int32, (tm, TOP_K), 1)
    idx_out = jnp.zeros((tm, TOP_K), jnp.int32)
    wt_out = jnp.zeros((tm, TOP_K), F32)
    for k in range(TOP_K):
        idx_out = jnp.where(col == k, idxs[k], idx_out)
        wt_out = jnp.where(col == k, exps[k] / den, wt_out)
    idx_ref[...] = idx_out
    wt_ref[...] = wt_out


def _norm2_router(x1, g, mod, w_router, b_router):
    tm = 256
    vec = lambda k: pl.BlockSpec((1, D_MODEL), lambda i, k=k: (0, k))
    return pl.pallas_call(
        _router_kernel,
        grid=(SEQ // tm,),
        in_specs=[pl.BlockSpec((tm, D_MODEL), lambda i: (i, 0)), vec(0), vec(4), vec(3),
                  pl.BlockSpec((D_MODEL, N_EXPERTS), lambda i: (0, 0)),
                  pl.BlockSpec((1, N_EXPERTS), lambda i: (0, 0))],
        out_specs=[pl.BlockSpec((2, tm * TOKEN_TILE, 128), lambda i: (0, i, 0)),
                   pl.BlockSpec((tm, TOP_K), lambda i: (i, 0)),
                   pl.BlockSpec((tm, TOP_K), lambda i: (i, 0))],
        out_shape=[jax.ShapeDtypeStruct((2, SEQ * TOKEN_TILE, 128), U32),
                   jax.ShapeDtypeStruct((SEQ, TOP_K), jnp.int32),
                   jax.ShapeDtypeStruct((SEQ, TOP_K), F32)],
        compiler_params=_params(("arbitrary",)),
        name="norm2_router",
    )(x1, g.reshape(1, D_MODEL), mod, mod, w_router, b_router.reshape(1, N_EXPERTS))


def _routing_tables(top_idx):
    i32 = jnp.int32
    e_flat = top_idx.reshape(-1)
    onehot = (e_flat[:, None] == jnp.arange(N_EXPERTS, dtype=i32)[None, :]).astype(i32)
    csum = jnp.cumsum(onehot, axis=0)
    rank = jnp.sum(onehot * csum, axis=1) - 1
    counts = csum[-1]
    nblk = (counts + SLOT_BLOCK - 1) // SLOT_BLOCK
    blk_end = jnp.cumsum(nblk)
    blk_start = blk_end - nblk
    ntile = (nblk + BLOCKS_PER_TILE - 1) // BLOCKS_PER_TILE
    tile_end = jnp.cumsum(ntile)
    tile_start = tile_end - ntile
    n_tiles = tile_end[-1]
    n_blocks = blk_end[-1]

    dest = (jnp.sum(onehot * (tile_start * TILE_ROWS)[None, :], axis=1) + rank).astype(i32)
    packed = jnp.sum(onehot * (blk_start * SLOT_BLOCK)[None, :], axis=1) + rank
    tok = jnp.arange(N_ASSIGN, dtype=i32) // TOP_K
    slot_tok = jnp.zeros((N_SLOT_BLOCKS * SLOT_BLOCK,), i32).at[packed].set(tok, unique_indices=True)

    s_raw = jnp.arange(MAX_TILES, dtype=i32)
    s = jnp.minimum(s_raw, n_tiles - 1)
    tile_e = jnp.minimum(jnp.searchsorted(tile_end, s, side='right'), N_EXPERTS - 1).astype(i32)
    local = s - tile_start[tile_e]
    tile_nb = jnp.clip(nblk[tile_e] - BLOCKS_PER_TILE * local, 0, BLOCKS_PER_TILE)
    tile_nb = jnp.where(s_raw < n_tiles, tile_nb, 0).astype(i32)
    tile_b0 = (blk_start[tile_e] + BLOCKS_PER_TILE * local).astype(i32)
    tile_src = s.astype(i32)
    return dest, slot_tok, n_blocks.reshape(1).astype(i32), tile_e, tile_b0, tile_src, tile_nb


def _gather_kernel(nvalid_ref, tok_ref, h_hbm, o_ref, sem):
    def tile_copy(r, tok, half):
        src = h_hbm.at[half, pl.ds(pl.multiple_of(tok * TOKEN_TILE, TOKEN_TILE), TOKEN_TILE), :]
        dst = o_ref.at[half, pl.ds(pl.multiple_of(r * TOKEN_TILE, TOKEN_TILE), TOKEN_TILE), :]
        return pltpu.make_async_copy(src, dst, sem)

    def issue(r, carry):
        tok = tok_ref[0, 0, r]
        tile_copy(r, tok, 0).start()
        tile_copy(r, tok, 1).start()
        return carry

    def drain(r, carry):
        tile_copy(r, 0, 0).wait()
        tile_copy(r, 0, 1).wait()
        return carry

    used = pl.program_id(0) < nvalid_ref[0]

    @pl.when(used)
    def _():
        lax.fori_loop(0, SLOT_BLOCK, issue, 0, unroll=8)
        lax.fori_loop(0, SLOT_BLOCK, drain, 0, unroll=8)

    @pl.when(jnp.logical_not(used))
    def _():
        o_ref[...] = jnp.zeros_like(o_ref)


def _gather_tokens(hp, slot_tok, n_blocks):
    grid_spec = pltpu.PrefetchScalarGridSpec(
        num_scalar_prefetch=1,
        grid=(N_SLOT_BLOCKS,),
        in_specs=[pl.BlockSpec((1, 1, SLOT_BLOCK), lambda b, nv: (b, 0, 0), memory_space=pltpu.SMEM),
                  pl.BlockSpec(memory_space=pl.ANY)],
        out_specs=pl.BlockSpec((2, SLOT_BLOCK * TOKEN_TILE, 128), lambda b, nv: (0, b, 0)),
        scratch_shapes=[pltpu.SemaphoreType.DMA(())],
    )
    return pl.pallas_call(
        _gather_kernel,
        grid_spec=grid_spec,
        out_shape=jax.ShapeDtypeStruct((2, N_SLOT_BLOCKS * SLOT_BLOCK * TOKEN_TILE, 128), U32),
        compiler_params=_params(("arbitrary",)),
        name="moe_gather",
    )(n_blocks, slot_tok.reshape(N_SLOT_BLOCKS, 1, SLOT_BLOCK), hp)


UP_TF = 256


def _deinterleave_matrix():
    r = lax.broadcasted_iota(jnp.int32, (256, 256), 0)
    c = lax.broadcasted_iota(jnp.int32, (256, 256), 1)
    src = jnp.where(c < 128, 2 * c, 2 * (c - 128) + 1)
    return jnp.where(r == src, 1.0, 0.0).astype(BF16)


def _up_kernel(e_ref, b0_ref, nb_ref, *refs):
    x_refs = refs[:BLOCKS_PER_TILE]
    w_ref, b_ref, o_ref, wb_ref, xb_ref = refs[BLOCKS_PER_TILE:]
    nb = nb_ref[pl.program_id(0)]

    @pl.when(pl.program_id(1) == 0)
    def _():
        for k in range(BLOCKS_PER_TILE):
            for half in range(2):
                for c in range(TOKEN_TILE):
                    col = (half * TOKEN_TILE + c) * 128
                    lo, hi = _unpack_halves(_load_token_tile_column(x_refs[k], (), 0, SLOT_BLOCK, half, c))
                    xb_ref[k, :, col:col + 128] = lo.astype(BF16)
                    xb_ref[k, :, HALF + col:HALF + col + 128] = hi.astype(BF16)

    wb_ref[...] = w_ref[...].astype(BF16)
    perm = _deinterleave_matrix()
    for k in range(BLOCKS_PER_TILE):
        rows = slice(k * SLOT_BLOCK, (k + 1) * SLOT_BLOCK)

        @pl.when(k < nb)
        def _(k=k, rows=rows):
            gu = jnp.dot(xb_ref[k], wb_ref[...], preferred_element_type=F32) + b_ref[...]
            for c in range(2 * UP_TF // 256):
                gl = jnp.dot(gu[:, c * 256:(c + 1) * 256].astype(BF16), perm, preferred_element_type=F32)
                glu = jnp.minimum(gl[:, :128], SWIGLU_LIMIT)
                lin = jnp.clip(gl[:, 128:], -SWIGLU_LIMIT, SWIGLU_LIMIT)
                act = glu * jax.nn.sigmoid(SWIGLU_ALPHA * glu) * (lin + 1.0)
                o_ref[rows, c * 128:(c + 1) * 128] = act.astype(BF16)

        @pl.when(k >= nb)
        def _(rows=rows):
            o_ref[rows, :] = jnp.zeros((SLOT_BLOCK, UP_TF), BF16)


def _expert_up(xs, w_up, b_up, tile_e, tile_b0, tile_nb):
    nj = EXPERT_FF // UP_TF

    def x_spec(k):
        return pl.BlockSpec(
            (2, SLOT_BLOCK * TOKEN_TILE, 128),
            lambda s, j, e, b0, nb, k=k: (0, b0[s] + jnp.where(k < nb[s], k, 0), 0))

    grid_spec = pltpu.PrefetchScalarGridSpec(
        num_scalar_prefetch=3,
        grid=(MAX_TILES, nj),
        in_specs=[x_spec(k) for k in range(BLOCKS_PER_TILE)] + [
            pl.BlockSpec((None, D_MODEL, 2 * UP_TF), lambda s, j, e, src, nb: (e[s], 0, j)),
            pl.BlockSpec((None, 1, 2 * UP_TF), lambda s, j, e, src, nb: (e[s], 0, j))],
        out_specs=pl.BlockSpec((TILE_ROWS, UP_TF), lambda s, j, e, src, nb: (s, j)),
        scratch_shapes=[pltpu.VMEM((D_MODEL, 2 * UP_TF), BF16),
                        pltpu.VMEM((BLOCKS_PER_TILE, SLOT_BLOCK, D_MODEL), BF16)],
    )
    return pl.pallas_call(
        _up_kernel,
        grid_spec=grid_spec,
        out_shape=jax.ShapeDtypeStruct((MAX_TILES * TILE_ROWS, EXPERT_FF), BF16),
        compiler_params=_params(("arbitrary", "arbitrary")),
        name="expert_up",
    )(tile_e, tile_b0, tile_nb, *([xs] * BLOCKS_PER_TILE), w_up,
      b_up.reshape(N_EXPERTS, 1, 2 * EXPERT_FF))


DOWN_TN = TOKEN_TILE * 128


def _down_kernel(e_ref, src_ref, nb_ref, a_ref, wl_ref, wh_ref, bl_ref, bh_ref, o_ref, wlb_ref, whb_ref):
    nb = nb_ref[pl.program_id(0)]
    wlb_ref[...] = wl_ref[...].astype(BF16)
    whb_ref[...] = wh_ref[...].astype(BF16)
    for k in range(BLOCKS_PER_TILE):
        @pl.when(k < nb)
        def _(k=k):
            a = a_ref[k * SLOT_BLOCK:(k + 1) * SLOT_BLOCK, :]
            y_lo = jnp.dot(a, wlb_ref[...], preferred_element_type=F32) + bl_ref[...]
            y_hi = jnp.dot(a, whb_ref[...], preferred_element_type=F32) + bh_ref[...]
            packed = _pack_halves(jnp.concatenate([y_lo, y_hi], axis=1))
            for c in range(TOKEN_TILE):
                o_ref[pl.ds(k * SLOT_BLOCK * TOKEN_TILE + c, SLOT_BLOCK, stride=TOKEN_TILE), :] = (
                    packed[:, c * 128:(c + 1) * 128])

        @pl.when(k >= nb)
        def _(k=k):
            o_ref[k * SLOT_BLOCK * TOKEN_TILE:(k + 1) * SLOT_BLOCK * TOKEN_TILE, :] = (
                jnp.zeros((SLOT_BLOCK * TOKEN_TILE, 128), U32))


def _expert_down(act, w_down, b_down, tile_e, tile_src, tile_nb):
    nj = HALF // DOWN_TN
    grid_spec = pltpu.PrefetchScalarGridSpec(
        num_scalar_prefetch=3,
        grid=(MAX_TILES, nj),
        in_specs=[pl.BlockSpec((TILE_ROWS, EXPERT_FF), lambda s, j, e, src, nb: (src[s], 0)),
                  pl.BlockSpec((None, EXPERT_FF, DOWN_TN), lambda s, j, e, src, nb: (e[s], 0, j)),
                  pl.BlockSpec((None, EXPERT_FF, DOWN_TN), lambda s, j, e, src, nb: (e[s], 0, nj + j)),
                  pl.BlockSpec((None, 1, DOWN_TN), lambda s, j, e, src, nb: (e[s], 0, j)),
                  pl.BlockSpec((None, 1, DOWN_TN), lambda s, j, e, src, nb: (e[s], 0, nj + j))],
        out_specs=pl.BlockSpec((None, TILE_ROWS * TOKEN_TILE, 128), lambda s, j, e, src, nb: (j, s, 0)),
        scratch_shapes=[pltpu.VMEM((EXPERT_FF, DOWN_TN), BF16), pltpu.VMEM((EXPERT_FF, DOWN_TN), BF16)],
    )
    b3 = b_down.reshape(N_EXPERTS, 1, D_MODEL)
    return pl.pallas_call(
        _down_kernel,
        grid_spec=grid_spec,
        out_shape=jax.ShapeDtypeStruct((2, MAX_TILES * TILE_ROWS * TOKEN_TILE, 128), U32),
        compiler_params=_params(("arbitrary", "arbitrary")),
        name="expert_down",
    )(tile_e, tile_src, tile_nb, act, w_down, w_down, b3, b3)


COMBINE_TM = 64


def _combine_kernel(dest_ref, y_hbm, x_ref, wt_ref, g_ref, o_ref, buf_ref, sem):
    def tile_copy(t, k, half, slot):
        src = y_hbm.at[half, pl.ds(pl.multiple_of(slot * TOKEN_TILE, TOKEN_TILE), TOKEN_TILE), :]
        dst = buf_ref.at[k, half, pl.ds(pl.multiple_of(t * TOKEN_TILE, TOKEN_TILE), TOKEN_TILE), :]
        return pltpu.make_async_copy(src, dst, sem)

    def issue(t, carry):
        for k in range(TOP_K):
            slot = dest_ref[0, 0, t * TOP_K + k]
            tile_copy(t, k, 0, slot).start()
            tile_copy(t, k, 1, slot).start()
        return carry

    def drain(t, carry):
        for k in range(TOP_K):
            tile_copy(t, k, 0, 0).wait()
            tile_copy(t, k, 1, 0).wait()
        return carry

    lax.fori_loop(0, COMBINE_TM, issue, 0, unroll=4)
    lax.fori_loop(0, COMBINE_TM, drain, 0, unroll=4)

    weights = [wt_ref[:, k:k + 1] for k in range(TOP_K)]
    for half in range(2):
        for c in range(TOKEN_TILE):
            col = (half * TOKEN_TILE + c) * 128
            acc_lo = jnp.zeros((COMBINE_TM, 128), F32)
            acc_hi = jnp.zeros((COMBINE_TM, 128), F32)
            for k in range(TOP_K):
                lo, hi = _unpack_halves(_load_token_tile_column(buf_ref, (k,), 0, COMBINE_TM, half, c))
                acc_lo = acc_lo + weights[k] * lo
                acc_hi = acc_hi + weights[k] * hi
            o_ref[:, col:col + 128] = x_ref[:, col:col + 128] + g_ref[:, col:col + 128] * acc_lo
            hcol = HALF + col
            o_ref[:, hcol:hcol + 128] = x_ref[:, hcol:hcol + 128] + g_ref[:, hcol:hcol + 128] * acc_hi


def _combine(y_slots, dest, x1, top_w, mod):
    nt = SEQ // COMBINE_TM
    return pl.pallas_call(
        _combine_kernel,
        grid=(nt,),
        in_specs=[pl.BlockSpec((1, 1, COMBINE_TM * TOP_K), lambda i: (i, 0, 0), memory_space=pltpu.SMEM),
                  pl.BlockSpec(memory_space=pl.ANY),
                  pl.BlockSpec((COMBINE_TM, D_MODEL), lambda i: (i, 0)),
                  pl.BlockSpec((COMBINE_TM, TOP_K), lambda i: (i, 0)),
                  pl.BlockSpec((1, D_MODEL), lambda i: (0, 5))],
        out_specs=pl.BlockSpec((COMBINE_TM, D_MODEL), lambda i: (i, 0)),
        out_shape=jax.ShapeDtypeStruct((SEQ, D_MODEL), F32),
        scratch_shapes=[pltpu.VMEM((TOP_K, 2, COMBINE_TM * TOKEN_TILE, 128), U32), pltpu.SemaphoreType.DMA(())],
        compiler_params=_params(("arbitrary",)),
        name="moe_combine",
    )(dest.reshape(nt, 1, COMBINE_TM * TOP_K), y_slots, x1, top_w, mod)


def _layer(x, c, norm1_g, norm2_g, w_ada, b_ada, w_in, q_norm_g, k_norm_g, rel_bias, conv_w, conv_b,
           conv_norm_g, w_attn_out, w_conv_out, w_o, w_router, b_router, w_up, b_up, w_down, b_down):
    mod = _ada(c, w_ada, b_ada)
    h = _norm1(x, norm1_g, mod)

    outs, lses = [], []
    for group, (_, dil) in enumerate(ATTN_GROUPS):
        qkv = _qkv_proj(h, w_in, q_norm_g, k_norm_g, group, dil)
        o, lse = _band_attention(qkv, _bias_tiles(rel_bias, group, dil), dil)
        outs.append(o)
        lses.append(lse)
    attn = _merge_groups(outs, lses)

    u = _conv_module(_conv_glu_proj(h, w_in), conv_w, conv_b, conv_norm_g)
    gates = _gate_proj(h, w_in)
    merged = _branch_merge(attn, u, w_attn_out, w_conv_out, gates)
    x1 = _wo_residual(merged, w_o, x, mod)

    hp, top_idx, top_w = _norm2_router(x1, norm2_g, mod, w_router, b_router)
    dest, slot_tok, n_blocks, tile_e, tile_b0, tile_src, tile_nb = _routing_tables(top_idx)
    xs = _gather_tokens(hp, slot_tok, n_blocks)
    act = _expert_up(xs, w_up, b_up, tile_e, tile_b0, tile_nb)
    y_slots = _expert_down(act, w_down, b_down, tile_e, tile_src, tile_nb)
    return _combine(y_slots, dest, x1, top_w, mod)


def kernel(x, c, norm1_g, norm2_g, w_ada, b_ada, w_in, q_norm_g, k_norm_g, rel_bias, conv_w, conv_b,
           conv_norm_g, w_attn_out, w_conv_out, w_o, w_router, b_router, w_up, b_up, w_down, b_down):
    batch = x.shape[0]
    xs = x.reshape(batch * SEQ, D_MODEL)
    for layer in range(w_in.shape[0]):
        xs = _layer(xs, c, norm1_g[layer], norm2_g[layer], w_ada[layer], b_ada[layer], w_in[layer],
                    q_norm_g[layer], k_norm_g[layer], rel_bias, conv_w[layer], conv_b[layer],
                    conv_norm_g[layer], w_attn_out[layer], w_conv_out[layer], w_o[layer],
                    w_router[layer], b_router[layer], w_up[layer], b_up[layer], w_down[layer], b_down[layer])
    return xs.reshape(x.shape)
```

```python
import functools
import math

import numpy as np
import jax
import jax.numpy as jnp
from jax import lax
from jax.experimental import pallas as pl
from jax.experimental.pallas import tpu as pltpu

D_MODEL = 4096
SEQ = 8192
HEAD_DIM = 128
ATTN_GROUPS = ((128, 1), (512, 4), (2048, 16))
N_GROUPS = len(ATTN_GROUPS)
HEADS_PER_GROUP = 8
GROUP_WIDTH = HEADS_PER_GROUP * HEAD_DIM
ATTN_WIDTH = N_GROUPS * GROUP_WIDTH
CONV_WIDTH = D_MODEL // 2
CONV_KERNEL = 31
CONV_OFF = 3 * ATTN_WIDTH
GATE_OFF = CONV_OFF + 2 * CONV_WIDTH
IN_WIDTH = GATE_OFF + 2 * D_MODEL
REL_BUCKETS = 32
REL_MAX_DISTANCE = 1024
N_EXPERTS = 32
TOP_K = 4
EXPERT_FF = 1536
SWIGLU_ALPHA = 1.702
SWIGLU_LIMIT = 7.0
NORM_EPS = 1e-6
NEG_INF = -1e30
N_SIDE = 64

V7X_VMEM_LIMIT = 56 * 1024 * 1024

SLOT_BLOCK = 256
BLOCKS_PER_TILE = 5
N_ASSIGN = SEQ * TOP_K
TILE_ROWS = SLOT_BLOCK * BLOCKS_PER_TILE
TILE_ROW_VARIANTS = (4, 5)
N_SLOT_BLOCKS = (N_ASSIGN + N_EXPERTS * (SLOT_BLOCK - 1)) // SLOT_BLOCK + 1
MAX_TILES = (N_SLOT_BLOCKS + N_EXPERTS * (BLOCKS_PER_TILE - 1)) // BLOCKS_PER_TILE
HALF = D_MODEL // 2

F32 = jnp.float32
BF16 = jnp.bfloat16
U32 = jnp.uint32


def _params(sem, vmem=V7X_VMEM_LIMIT):
    return pltpu.CompilerParams(dimension_semantics=sem, vmem_limit_bytes=vmem)


def _pack_halves(v):
    n = v.shape[-1] // 2
    lo = pltpu.bitcast(v[:, :n].astype(BF16).astype(F32), U32) >> 16
    hi = pltpu.bitcast(v[:, n:].astype(BF16).astype(F32), U32) & jnp.uint32(0xFFFF0000)
    return lo | hi


def _unpack_halves(p):
    lo = pltpu.bitcast(p << 16, F32)
    hi = pltpu.bitcast(p & jnp.uint32(0xFFFF0000), F32)
    return lo, hi


TOKEN_TILE = 8


def _store_token_tiles(ref, row0, rows, packed):
    for half in range(2):
        for c in range(TOKEN_TILE):
            col = (half * TOKEN_TILE + c) * 128
            ref[half, pl.ds(row0 * TOKEN_TILE + c, rows, stride=TOKEN_TILE), :] = packed[:, col:col + 128]


ROW_TILE = HALF // 128


def _load_token_tile_column(ref, lead, row0, rows, half, c):
    return ref[(*lead, half, pl.ds(row0 * TOKEN_TILE + c, rows, stride=TOKEN_TILE), slice(None))]


def _ada_kernel(c_ref, w_ref, b_ref, o_ref, cs_ref):
    @pl.when(pl.program_id(0) == 0)
    def _():
        c = c_ref[...]
        cs_ref[...] = c * jax.nn.sigmoid(c)

    tn = o_ref.shape[-1]
    rows_per_step = 64

    def body(r, acc):
        rows = pl.ds(pl.multiple_of(r * rows_per_step, rows_per_step), rows_per_step)
        cs = cs_ref[rows, :]
        prod = w_ref[rows, :] * jnp.concatenate([cs] * (tn // 128), axis=1)
        return acc + prod.reshape(rows_per_step // 8, 8, tn).sum(axis=0)

    acc = lax.fori_loop(0, D_MODEL // rows_per_step, body, jnp.zeros((8, tn), F32))
    o_ref[...] = jnp.sum(acc, axis=0, keepdims=True) + b_ref[...]


def _ada(c, w_ada, b_ada):
    tn = 512
    n = w_ada.shape[-1]
    c_lanes = jnp.broadcast_to(c.reshape(D_MODEL, 1), (D_MODEL, 128))
    return pl.pallas_call(
        _ada_kernel,
        grid=(n // tn,),
        in_specs=[pl.BlockSpec((D_MODEL, 128), lambda j: (0, 0)),
                  pl.BlockSpec((D_MODEL, tn), lambda j: (0, j)),
                  pl.BlockSpec((1, tn), lambda j: (0, j))],
        out_specs=pl.BlockSpec((1, tn), lambda j: (0, j)),
        out_shape=jax.ShapeDtypeStruct((1, n), F32),
        scratch_shapes=[pltpu.VMEM((D_MODEL, 128), F32)],
        compiler_params=_params(("arbitrary",)),
        name="ada_mod",
    )(c_lanes, w_ada, b_ada.reshape(1, n))


def _modulated_norm(x, g, scale, shift):
    ms = jnp.mean(x * x, axis=-1, keepdims=True)
    y = x * lax.rsqrt(ms + NORM_EPS) * g
    return y * (1.0 + scale) + shift


def _norm1_kernel(x_ref, g_ref, sc_ref, sh_ref, o_ref):
    o_ref[...] = _modulated_norm(x_ref[...], g_ref[...], sc_ref[...], sh_ref[...]).astype(o_ref.dtype)


def _norm1(x, g, mod):
    tm = 256
    vec = lambda k: pl.BlockSpec((1, D_MODEL), lambda i, k=k: (0, k))
    return pl.pallas_call(
        _norm1_kernel,
        grid=(SEQ // tm,),
        in_specs=[pl.BlockSpec((tm, D_MODEL), lambda i: (i, 0)), vec(0), vec(1), vec(0)],
        out_specs=pl.BlockSpec((tm, D_MODEL), lambda i: (i, 0)),
        out_shape=jax.ShapeDtypeStruct((SEQ, D_MODEL), BF16),
        compiler_params=_params(("arbitrary",)),
        name="norm1",
    )(x, g.reshape(1, D_MODEL), mod, mod)


def _head_norm(acc, gain, post_scale):
    outs = []
    for h in range(acc.shape[-1] // HEAD_DIM):
        blk = acc[:, h * HEAD_DIM:(h + 1) * HEAD_DIM]
        ms = jnp.mean(blk * blk, axis=-1, keepdims=True)
        outs.append(blk * lax.rsqrt(ms + NORM_EPS) * (gain * post_scale))
    return jnp.concatenate(outs, axis=1)


def _qkv_kernel(h_ref, w_ref, gq_ref, gk_ref, o_ref, wb_ref, acc_ref, *, dil):
    which = pl.program_id(0)
    tm = h_ref.shape[0]

    @pl.when(pl.program_id(1) == 0)
    def _():
        wb_ref[...] = w_ref[...].astype(BF16)

    acc = jnp.dot(h_ref[...], wb_ref[...], preferred_element_type=F32)

    def write(vals):
        if dil == 1:
            o_ref[0] = vals.astype(BF16)
        else:
            for h in range(HEADS_PER_GROUP):
                sl = slice(h * HEAD_DIM, (h + 1) * HEAD_DIM)
                acc_ref[h] = vals[:, sl]
                for r in range(dil):
                    o_ref[r, :, sl] = acc_ref[h, pl.ds(r, tm // dil, stride=dil), :].astype(BF16)

    @pl.when(which == 0)
    def _():
        write(_head_norm(acc, gq_ref[...], HEAD_DIM ** -0.5))

    @pl.when(which == 1)
    def _():
        write(_head_norm(acc, gk_ref[...], 1.0))

    @pl.when(which == 2)
    def _():
        write(acc)


def _qkv_proj(h, w_in, gq, gk, group, dil):
    tm = 512
    seg = SEQ // dil
    return pl.pallas_call(
        functools.partial(_qkv_kernel, dil=dil),
        grid=(3, SEQ // tm),
        in_specs=[pl.BlockSpec((tm, D_MODEL), lambda w, i: (i, 0)),
                  pl.BlockSpec((D_MODEL, GROUP_WIDTH), lambda w, i: (0, w * N_GROUPS + group),
                               pipeline_mode=pl.Buffered(1)),
                  pl.BlockSpec((1, HEAD_DIM), lambda w, i: (0, 0)),
                  pl.BlockSpec((1, HEAD_DIM), lambda w, i: (0, 0))],
        out_specs=pl.BlockSpec((None, dil, tm // dil, GROUP_WIDTH), lambda w, i: (w, 0, i, 0)),
        out_shape=jax.ShapeDtypeStruct((3, dil, seg, GROUP_WIDTH), BF16),
        scratch_shapes=[pltpu.VMEM((D_MODEL, GROUP_WIDTH), BF16), pltpu.VMEM((HEADS_PER_GROUP, tm, HEAD_DIM), F32)],
        compiler_params=_params(("arbitrary", "arbitrary")),
        name=f"qkv_proj_g{group}",
    )(h, w_in, gq.reshape(1, HEAD_DIM), gk.reshape(1, HEAD_DIM))


def _glu_kernel(h_ref, wa_ref, wg_ref, o_ref, wab_ref, wgb_ref):
    @pl.when(pl.program_id(1) == 0)
    def _():
        wab_ref[...] = wa_ref[...].astype(BF16)
        wgb_ref[...] = wg_ref[...].astype(BF16)

    h = h_ref[...]
    a = jnp.dot(h, wab_ref[...], preferred_element_type=F32)
    g = jnp.dot(h, wgb_ref[...], preferred_element_type=F32)
    o_ref[...] = (a * jax.nn.sigmoid(g)).astype(o_ref.dtype)


def _conv_glu_proj(h, w_in):
    tm, tn = 512, 512
    a0 = CONV_OFF // tn
    g0 = (CONV_OFF + CONV_WIDTH) // tn
    return pl.pallas_call(
        _glu_kernel,
        grid=(CONV_WIDTH // tn, SEQ // tm),
        in_specs=[pl.BlockSpec((tm, D_MODEL), lambda j, i: (i, 0)),
                  pl.BlockSpec((D_MODEL, tn), lambda j, i: (0, a0 + j)),
                  pl.BlockSpec((D_MODEL, tn), lambda j, i: (0, g0 + j))],
        out_specs=pl.BlockSpec((tm, tn), lambda j, i: (i, j)),
        out_shape=jax.ShapeDtypeStruct((SEQ, CONV_WIDTH), BF16),
        scratch_shapes=[pltpu.VMEM((D_MODEL, tn), BF16), pltpu.VMEM((D_MODEL, tn), BF16)],
        compiler_params=_params(("arbitrary", "arbitrary")),
        name="conv_glu_proj",
    )(h, w_in, w_in)


def _gate_kernel(h_ref, w_ref, o_ref, wb_ref):
    @pl.when(pl.program_id(1) == 0)
    def _():
        wb_ref[...] = w_ref[...].astype(BF16)

    acc = jnp.dot(h_ref[...], wb_ref[...], preferred_element_type=F32)
    o_ref[...] = jax.nn.sigmoid(acc).astype(o_ref.dtype)


def _gate_proj(h, w_in):
    tm, tn = 1024, 512
    c0 = GATE_OFF // tn
    return pl.pallas_call(
        _gate_kernel,
        grid=(2 * D_MODEL // tn, SEQ // tm),
        in_specs=[pl.BlockSpec((tm, D_MODEL), lambda j, i: (i, 0)),
                  pl.BlockSpec((D_MODEL, tn), lambda j, i: (0, c0 + j))],
        out_specs=pl.BlockSpec((tm, tn), lambda j, i: (i, j)),
        out_shape=jax.ShapeDtypeStruct((SEQ, 2 * D_MODEL), BF16),
        scratch_shapes=[pltpu.VMEM((D_MODEL, tn), BF16)],
        compiler_params=_params(("arbitrary", "arbitrary")),
        name="gate_proj",
    )(h, w_in)


ATTN_TQ = 128
ATTN_TK = ATTN_TQ + 2 * N_SIDE
LSE_LANES = 128


def _bucket_tile(dil):
    a = np.arange(ATTN_TQ)[:, None]
    j = np.arange(ATTN_TK)[None, :]
    steps = j - N_SIDE - a
    rel = steps * dil
    nb = REL_BUCKETS // 2
    max_exact = nb // 2
    n = np.abs(rel)
    side = np.where(rel > 0, nb, 0)
    nf = np.maximum(n, 1).astype(np.float32)
    large = max_exact + (np.log(nf / np.float32(max_exact)) / np.float32(math.log(REL_MAX_DISTANCE / max_exact))
                         * np.float32(nb - max_exact)).astype(np.int32)
    large = np.minimum(large, nb - 1)
    bucket = side + np.where(n < max_exact, n, large)
    return np.where(np.abs(steps) <= N_SIDE, bucket, -1).astype(np.int32)


def _bias_kernel(tbl_ref, idx_ref, o_ref, *, group):
    h = pl.program_id(0)
    idx = idx_ref[...]
    acc = jnp.full(idx.shape, NEG_INF, F32)
    for b in range(REL_BUCKETS):
        acc = jnp.where(idx == b, tbl_ref[b, group * HEADS_PER_GROUP + h], acc)
    o_ref[...] = acc


def _bias_tiles(rel_bias, group, dil):
    return pl.pallas_call(
        functools.partial(_bias_kernel, group=group),
        grid=(HEADS_PER_GROUP,),
        in_specs=[pl.BlockSpec(memory_space=pltpu.SMEM),
                  pl.BlockSpec((ATTN_TQ, ATTN_TK), lambda h: (0, 0))],
        out_specs=pl.BlockSpec((None, ATTN_TQ, ATTN_TK), lambda h: (h, 0, 0)),
        out_shape=jax.ShapeDtypeStruct((HEADS_PER_GROUP, ATTN_TQ, ATTN_TK), F32),
        compiler_params=_params(("arbitrary",)),
        name=f"attn_bias_g{group}",
    )(rel_bias, jnp.asarray(_bucket_tile(dil)))


def _attn_kernel(q_ref, kp_ref, kc_ref, kn_ref, vp_ref, vc_ref, vn_ref, bias_ref, o_ref, lse_ref, *, seg):
    i = pl.program_id(1)
    q = q_ref[...]
    k = jnp.concatenate([kp_ref[...], kc_ref[...], kn_ref[...]], axis=0)
    v = jnp.concatenate([vp_ref[...], vc_ref[...], vn_ref[...]], axis=0)
    kpos = i * ATTN_TQ - N_SIDE + lax.broadcasted_iota(jnp.int32, (1, ATTN_TK), 1)
    valid = (kpos >= 0) & (kpos < seg)
    lane = lax.broadcasted_iota(jnp.int32, (ATTN_TQ, LSE_LANES), 1)
    lse_all = jnp.zeros((ATTN_TQ, LSE_LANES), F32)
    for h in range(HEADS_PER_GROUP):
        sl = slice(h * HEAD_DIM, (h + 1) * HEAD_DIM)
        s = lax.dot_general(q[:, sl], k[:, sl], (((1,), (1,)), ((), ())), preferred_element_type=F32)
        s = jnp.where(valid, s + bias_ref[h], NEG_INF)
        m = jnp.max(s, axis=-1, keepdims=True)
        p = jnp.exp(s - m)
        l = jnp.sum(p, axis=-1, keepdims=True)
        o = jnp.dot(p.astype(BF16), v[:, sl], preferred_element_type=F32)
        o_ref[:, sl] = (o / l).astype(o_ref.dtype)
        lse_all = jnp.where(lane == h, m + jnp.log(l), lse_all)
    lse_ref[...] = lse_all


def _band_attention(qkv, bias, dil):
    seg = SEQ // dil
    sub = ATTN_TQ // N_SIDE
    last = seg // N_SIDE - 1

    def cur(which):
        return pl.BlockSpec((None, None, ATTN_TQ, GROUP_WIDTH), lambda r, i, w=which: (w, r, i, 0))

    def prev(which):
        return pl.BlockSpec((None, None, N_SIDE, GROUP_WIDTH),
                            lambda r, i, w=which: (w, r, jnp.maximum(i * sub - 1, 0), 0))

    def nxt(which):
        return pl.BlockSpec((None, None, N_SIDE, GROUP_WIDTH),
                            lambda r, i, w=which: (w, r, jnp.minimum((i + 1) * sub, last), 0))

    return pl.pallas_call(
        functools.partial(_attn_kernel, seg=seg),
        grid=(dil, seg // ATTN_TQ),
        in_specs=[cur(0), prev(1), cur(1), nxt(1), prev(2), cur(2), nxt(2),
                  pl.BlockSpec((HEADS_PER_GROUP, ATTN_TQ, ATTN_TK), lambda r, i: (0, 0, 0))],
        out_specs=[pl.BlockSpec((None, ATTN_TQ, GROUP_WIDTH), lambda r, i: (r, i, 0)),
                   pl.BlockSpec((None, ATTN_TQ, LSE_LANES), lambda r, i: (r, i, 0))],
        out_shape=[jax.ShapeDtypeStruct((dil, seg, GROUP_WIDTH), BF16),
                   jax.ShapeDtypeStruct((dil, seg, LSE_LANES), F32)],
        compiler_params=_params(("arbitrary", "arbitrary")),
        name=f"band_attn_d{dil}",
    )(qkv, qkv, qkv, qkv, qkv, qkv, qkv, bias)


def _merge_kernel(o0_ref, o1_ref, o2_ref, l0_ref, l1_ref, l2_ref, out_ref, buf_ref, lbuf_ref):
    tm = out_ref.shape[0]
    for gi, (o_ref, l_ref) in enumerate(((o1_ref, l1_ref), (o2_ref, l2_ref))):
        dil = o_ref.shape[0]
        for r in range(dil):
            rows = pl.ds(r, tm // dil, stride=dil)
            lbuf_ref[gi, rows, :] = l_ref[r]
            for h in range(HEADS_PER_GROUP):
                buf_ref[gi * HEADS_PER_GROUP + h, rows, :] = (
                    o_ref[r, :, h * HEAD_DIM:(h + 1) * HEAD_DIM].astype(F32))
    lse0, lse1, lse2 = l0_ref[0], lbuf_ref[0], lbuf_ref[1]
    top = jnp.maximum(jnp.maximum(lse0, lse1), lse2)
    w0, w1, w2 = jnp.exp(lse0 - top), jnp.exp(lse1 - top), jnp.exp(lse2 - top)
    den = w0 + w1 + w2
    for h in range(HEADS_PER_GROUP):
        sl = slice(h * HEAD_DIM, (h + 1) * HEAD_DIM)
        col = slice(h, h + 1)
        num = (w0[:, col] * o0_ref[0, :, sl].astype(F32) + w1[:, col] * buf_ref[h]
               + w2[:, col] * buf_ref[HEADS_PER_GROUP + h])
        out_ref[:, sl] = (num / den[:, col]).astype(out_ref.dtype)


def _merge_groups(outs, lses):
    tm = 256
    d1, d2 = ATTN_GROUPS[1][1], ATTN_GROUPS[2][1]

    def spec(dil, width):
        return pl.BlockSpec((dil, tm // dil, width), lambda i: (0, i, 0))

    return pl.pallas_call(
        _merge_kernel,
        grid=(SEQ // tm,),
        in_specs=[spec(1, GROUP_WIDTH), spec(d1, GROUP_WIDTH), spec(d2, GROUP_WIDTH),
                  spec(1, LSE_LANES), spec(d1, LSE_LANES), spec(d2, LSE_LANES)],
        out_specs=pl.BlockSpec((tm, GROUP_WIDTH), lambda i: (i, 0)),
        out_shape=jax.ShapeDtypeStruct((SEQ, GROUP_WIDTH), BF16),
        scratch_shapes=[pltpu.VMEM((2 * HEADS_PER_GROUP, tm, HEAD_DIM), F32), pltpu.VMEM((2, tm, LSE_LANES), F32)],
        compiler_params=_params(("arbitrary",)),
        name="attn_merge",
    )(*outs, *lses)


CONV_HALO = 16


def _conv_kernel(up_ref, uc_ref, un_ref, w_ref, b_ref, g_ref, o_ref, buf_ref, y_ref):
    i = pl.program_id(0)
    tm = uc_ref.shape[0]
    first = i == 0
    last = i == pl.num_programs(0) - 1
    buf_ref[0:CONV_HALO, :] = jnp.where(first, 0.0, up_ref[...].astype(F32))
    buf_ref[CONV_HALO:CONV_HALO + tm, :] = uc_ref[...].astype(F32)
    buf_ref[CONV_HALO + tm:, :] = jnp.where(last, 0.0, un_ref[...].astype(F32))
    pad = CONV_KERNEL // 2
    lanes = 256
    for c0 in range(0, CONV_WIDTH, lanes):
        cs = slice(c0, c0 + lanes)
        part = jnp.broadcast_to(b_ref[:, cs], (tm, lanes))
        first = CONV_HALO - pad
        for s in range(8):
            group = None
            for tap in range(CONV_KERNEL):
                off = first + tap
                if off % 8 != s:
                    continue
                term = w_ref[tap:tap + 1, cs] * buf_ref[pl.ds(off - s, tm + 8), cs]
                group = term if group is None else group + term
            part = part + group[s:s + tm, :]
        y_ref[:, cs] = part
    acc = y_ref[...]
    ms = jnp.mean(acc * acc, axis=-1, keepdims=True)
    y = acc * lax.rsqrt(ms + NORM_EPS) * g_ref[...]
    o_ref[...] = (y * jax.nn.sigmoid(y)).astype(o_ref.dtype)


def _conv_module(u, conv_w, conv_b, conv_norm_g):
    tm = 128
    sub = tm // CONV_HALO
    last = SEQ // CONV_HALO - 1
    return pl.pallas_call(
        _conv_kernel,
        grid=(SEQ // tm,),
        in_specs=[pl.BlockSpec((CONV_HALO, CONV_WIDTH), lambda i: (jnp.maximum(i * sub - 1, 0), 0)),
                  pl.BlockSpec((tm, CONV_WIDTH), lambda i: (i, 0)),
                  pl.BlockSpec((CONV_HALO, CONV_WIDTH), lambda i: (jnp.minimum((i + 1) * sub, last), 0)),
                  pl.BlockSpec((CONV_KERNEL, CONV_WIDTH), lambda i: (0, 0)),
                  pl.BlockSpec((1, CONV_WIDTH), lambda i: (0, 0)),
                  pl.BlockSpec((1, CONV_WIDTH), lambda i: (0, 0))],
        out_specs=pl.BlockSpec((tm, CONV_WIDTH), lambda i: (i, 0)),
        out_shape=jax.ShapeDtypeStruct((SEQ, CONV_WIDTH), BF16),
        scratch_shapes=[pltpu.VMEM((tm + 2 * CONV_HALO, CONV_WIDTH), F32), pltpu.VMEM((tm, CONV_WIDTH), F32)],
        compiler_params=_params(("arbitrary",)),
        name="conv_module",
    )(u, u, u, conv_w, conv_b.reshape(1, CONV_WIDTH), conv_norm_g.reshape(1, CONV_WIDTH))


def _branch_kernel(a_ref, u_ref, wa_ref, wc_ref, ga_ref, gc_ref, o_ref, wab_ref, wcb_ref):
    @pl.when(pl.program_id(1) == 0)
    def _():
        wab_ref[...] = wa_ref[...].astype(BF16)
        wcb_ref[...] = wc_ref[...].astype(BF16)

    attn = jnp.dot(a_ref[...], wab_ref[...], preferred_element_type=F32)
    conv = jnp.dot(u_ref[...], wcb_ref[...], preferred_element_type=F32)
    o_ref[...] = (ga_ref[...].astype(F32) * attn + gc_ref[...].astype(F32) * conv).astype(o_ref.dtype)


def _branch_merge(attn, conv, w_attn_out, w_conv_out, gates):
    tm, tn = 512, 512
    nj = D_MODEL // tn
    return pl.pallas_call(
        _branch_kernel,
        grid=(nj, SEQ // tm),
        in_specs=[pl.BlockSpec((tm, GROUP_WIDTH), lambda j, i: (i, 0)),
                  pl.BlockSpec((tm, CONV_WIDTH), lambda j, i: (i, 0)),
                  pl.BlockSpec((GROUP_WIDTH, tn), lambda j, i: (0, j)),
                  pl.BlockSpec((CONV_WIDTH, tn), lambda j, i: (0, j)),
                  pl.BlockSpec((tm, tn), lambda j, i: (i, j)),
                  pl.BlockSpec((tm, tn), lambda j, i: (i, nj + j))],
        out_specs=pl.BlockSpec((tm, tn), lambda j, i: (i, j)),
        out_shape=jax.ShapeDtypeStruct((SEQ, D_MODEL), BF16),
        scratch_shapes=[pltpu.VMEM((GROUP_WIDTH, tn), BF16), pltpu.VMEM((CONV_WIDTH, tn), BF16)],
        compiler_params=_params(("arbitrary", "arbitrary")),
        name="branch_merge",
    )(attn, conv, w_attn_out, w_conv_out, gates, gates)


def _wo_kernel(m_ref, w_ref, x_ref, g_ref, o_ref, wb_ref):
    @pl.when(pl.program_id(1) == 0)
    def _():
        wb_ref[...] = w_ref[...].astype(BF16)

    y = jnp.dot(m_ref[...], wb_ref[...], preferred_element_type=F32)
    o_ref[...] = x_ref[...] + g_ref[...] * y


def _wo_residual(merged, w_o, x, mod):
    tm, tn = 1024, 512
    nj = D_MODEL // tn
    return pl.pallas_call(
        _wo_kernel,
        grid=(nj, SEQ // tm),
        in_specs=[pl.BlockSpec((tm, D_MODEL), lambda j, i: (i, 0)),
                  pl.BlockSpec((D_MODEL, tn), lambda j, i: (0, j)),
                  pl.BlockSpec((tm, tn), lambda j, i: (i, j)),
                  pl.BlockSpec((1, tn), lambda j, i: (0, 2 * nj + j))],
        out_specs=pl.BlockSpec((tm, tn), lambda j, i: (i, j)),
        out_shape=jax.ShapeDtypeStruct((SEQ, D_MODEL), F32),
        scratch_shapes=[pltpu.VMEM((D_MODEL, tn), BF16)],
        compiler_params=_params(("arbitrary", "arbitrary")),
        name="wo_residual",
    )(merged, w_o, x, mod)


def _router_kernel(x_ref, g_ref, sc_ref, sh_ref, wr_ref, br_ref, hp_ref, idx_ref, wt_ref):
    tm = x_ref.shape[0]
    h = _modulated_norm(x_ref[...], g_ref[...], sc_ref[...], sh_ref[...])
    packed = _pack_halves(h)
    for q in range(ROW_TILE):
        hp_ref[pl.ds(q, tm, stride=ROW_TILE), :] = packed[:, q * 128:(q + 1) * 128]

    h_hi = h.astype(BF16)
    h_lo = (h - h_hi.astype(F32)).astype(BF16)
    w = wr_ref[...]
    w_hi = w.astype(BF16)
    w_lo = (w - w_hi.astype(F32)).astype(BF16)
    logits = (jnp.dot(h_hi, w_hi, preferred_element_type=F32) + jnp.dot(h_lo, w_hi, preferred_element_type=F32)
              + jnp.dot(h_hi, w_lo, preferred_element_type=F32) + br_ref[...])

    lane = lax.broadcasted_iota(jnp.int32, (tm, N_EXPERTS), 1)
    work = logits
    vals, idxs = [], []
    for _ in range(TOP_K):
        m = jnp.max(work, axis=-1, keepdims=True)
        idx = jnp.min(jnp.where(work == m, lane, N_EXPERTS), axis=-1, keepdims=True)
        vals.append(m)
        idxs.append(idx)
        work = jnp.where(lane == idx, -jnp.inf, work)
    exps = [jnp.exp(v - vals[0]) for v in vals]
    den = exps[0] + exps[1] + exps[2] + exps[3]
    col = lax.broadcasted_iota(jnp.int32, (tm, TOP_K), 1)
    idx_out = jnp.zeros((tm, TOP_K), jnp.int32)
    wt_out = jnp.zeros((tm, TOP_K), F32)
    for k in range(TOP_K):
        idx_out = jnp.where(col == k, idxs[k], idx_out)
        wt_out = jnp.where(col == k, exps[k] / den, wt_out)
    idx_ref[...] = idx_out
    wt_ref[...] = wt_out


def _norm2_router(x1, g, mod, w_router, b_router):
    tm = 256
    vec = lambda k: pl.BlockSpec((1, D_MODEL), lambda i, k=k: (0, k))
    return pl.pallas_call(
        _router_kernel,
        grid=(SEQ // tm,),
        in_specs=[pl.BlockSpec((tm, D_MODEL), lambda i: (i, 0)), vec(0), vec(4), vec(3),
                  pl.BlockSpec((D_MODEL, N_EXPERTS), lambda i: (0, 0)),
                  pl.BlockSpec((1, N_EXPERTS), lambda i: (0, 0))],
        out_specs=[pl.BlockSpec((tm * ROW_TILE, 128), lambda i: (i, 0)),
                   pl.BlockSpec((tm, TOP_K), lambda i: (i, 0)),
                   pl.BlockSpec((tm, TOP_K), lambda i: (i, 0))],
        out_shape=[jax.ShapeDtypeStruct((SEQ * ROW_TILE, 128), U32),
                   jax.ShapeDtypeStruct((SEQ, TOP_K), jnp.int32),
                   jax.ShapeDtypeStruct((SEQ, TOP_K), F32)],
        compiler_params=_params(("arbitrary",)),
        name="norm2_router",
    )(x1, g.reshape(1, D_MODEL), mod, mod, w_router, b_router.reshape(1, N_EXPERTS))


def _routing_tables(top_idx):
    i32 = jnp.int32
    e_flat = top_idx.reshape(-1)
    onehot = (e_flat[:, None] == jnp.arange(N_EXPERTS, dtype=i32)[None, :]).astype(i32)
    csum = jnp.cumsum(onehot, axis=0)
    rank = jnp.sum(onehot * csum, axis=1) - 1
    counts = csum[-1]
    nblk = (counts + SLOT_BLOCK - 1) // SLOT_BLOCK
    blk_end = jnp.cumsum(nblk)
    blk_start = blk_end - nblk
    ntile = (nblk + BLOCKS_PER_TILE - 1) // BLOCKS_PER_TILE
    tile_end = jnp.cumsum(ntile)
    tile_start = tile_end - ntile
    n_tiles = tile_end[-1]
    n_blocks = blk_end[-1]

    dest = (jnp.sum(onehot * (tile_start * TILE_ROWS)[None, :], axis=1) + rank).astype(i32)
    packed = jnp.sum(onehot * (blk_start * SLOT_BLOCK)[None, :], axis=1) + rank
    tok = jnp.arange(N_ASSIGN, dtype=i32) // TOP_K
    slot_tok = jnp.zeros((N_SLOT_BLOCKS * SLOT_BLOCK,), i32).at[packed].set(tok, unique_indices=True)

    s_raw = jnp.arange(MAX_TILES, dtype=i32)
    s = jnp.minimum(s_raw, n_tiles - 1)
    tile_e = jnp.minimum(jnp.searchsorted(tile_end, s, side='right'), N_EXPERTS - 1).astype(i32)
    local = s - tile_start[tile_e]
    tile_nb = jnp.clip(nblk[tile_e] - BLOCKS_PER_TILE * local, 0, BLOCKS_PER_TILE)
    tile_nb = jnp.where(s_raw < n_tiles, tile_nb, 0).astype(i32)
    tile_b0 = (blk_start[tile_e] + BLOCKS_PER_TILE * local).astype(i32)
    tile_src = s.astype(i32)
    return dest, slot_tok, tile_e, tile_b0, tile_src, tile_nb


def _gather_kernel(tok_ref, next_ref, h_hbm, o_ref, buf_ref, sems):
    b = pl.program_id(0)
    cur = b % 2

    def row_copy(buf, r, tok):
        src = h_hbm.at[pl.ds(pl.multiple_of(tok * ROW_TILE, ROW_TILE), ROW_TILE), :]
        dst = buf_ref.at[buf, pl.ds(pl.multiple_of(r * ROW_TILE, ROW_TILE), ROW_TILE), :]
        return pltpu.make_async_copy(src, dst, sems.at[buf])

    def issue_all(table_ref, buf):
        def issue(r, carry):
            row_copy(buf, r, table_ref[0, 0, r]).start()
            return carry
        lax.fori_loop(0, SLOT_BLOCK, issue, 0, unroll=8)

    def drain(r, carry):
        row_copy(cur, r, 0).wait()
        return carry

    @pl.when(b == 0)
    def _():
        issue_all(tok_ref, 0)

    @pl.when(b + 1 < pl.num_programs(0))
    def _():
        issue_all(next_ref, 1 - cur)

    lax.fori_loop(0, SLOT_BLOCK, drain, 0, unroll=8)
    o_ref[...] = buf_ref[cur]


def _gather_tokens(hp, slot_tok):
    table = slot_tok.reshape(N_SLOT_BLOCKS, 1, SLOT_BLOCK)
    return pl.pallas_call(
        _gather_kernel,
        grid=(N_SLOT_BLOCKS,),
        in_specs=[pl.BlockSpec((1, 1, SLOT_BLOCK), lambda b: (b, 0, 0), memory_space=pltpu.SMEM),
                  pl.BlockSpec((1, 1, SLOT_BLOCK), lambda b: (jnp.minimum(b + 1, N_SLOT_BLOCKS - 1), 0, 0),
                               memory_space=pltpu.SMEM),
                  pl.BlockSpec(memory_space=pl.ANY)],
        out_specs=pl.BlockSpec((SLOT_BLOCK * ROW_TILE, 128), lambda b: (b, 0)),
        out_shape=jax.ShapeDtypeStruct((N_SLOT_BLOCKS * SLOT_BLOCK * ROW_TILE, 128), U32),
        scratch_shapes=[pltpu.VMEM((2, SLOT_BLOCK * ROW_TILE, 128), U32), pltpu.SemaphoreType.DMA((2,))],
        compiler_params=_params(("arbitrary",)),
        name="moe_gather",
    )(table, table, hp)


UP_TF = 256


def _deinterleave_matrix():
    r = lax.broadcasted_iota(jnp.int32, (256, 256), 0)
    c = lax.broadcasted_iota(jnp.int32, (256, 256), 1)
    src = jnp.where(c < 128, 2 * c, 2 * (c - 128) + 1)
    return jnp.where(r == src, 1.0, 0.0).astype(BF16)


def _up_kernel(e_ref, b0_ref, nb_ref, *refs):
    x_refs = refs[:BLOCKS_PER_TILE]
    w_ref, b_ref, o_ref, wb_ref, xb_ref = refs[BLOCKS_PER_TILE:]
    nb = nb_ref[pl.program_id(0)]

    @pl.when(pl.program_id(1) == 0)
    def _():
        for k in range(BLOCKS_PER_TILE):
            rows = slice(k * SLOT_BLOCK, (k + 1) * SLOT_BLOCK)

            @pl.when(k < nb)
            def _(k=k, rows=rows):
                for q in range(ROW_TILE):
                    lo, hi = _unpack_halves(x_refs[k][pl.ds(q, SLOT_BLOCK, stride=ROW_TILE), :])
                    xb_ref[rows, q * 128:(q + 1) * 128] = lo.astype(BF16)
                    xb_ref[rows, HALF + q * 128:HALF + (q + 1) * 128] = hi.astype(BF16)

            @pl.when(k >= nb)
            def _(rows=rows):
                xb_ref[rows, :] = jnp.zeros((SLOT_BLOCK, D_MODEL), BF16)

    def run(n_rows):
        if n_rows == 0:
            o_ref[...] = jnp.zeros((TILE_ROWS, UP_TF), BF16)
            return
        wb_ref[...] = w_ref[...].astype(BF16)
        perm = _deinterleave_matrix()
        gu = jnp.dot(xb_ref[0:n_rows, :], wb_ref[...], preferred_element_type=F32) + b_ref[...]
        for c in range(2 * UP_TF // 256):
            gl = jnp.dot(gu[:, c * 256:(c + 1) * 256].astype(BF16), perm, preferred_element_type=F32)
            glu = jnp.minimum(gl[:, :128], SWIGLU_LIMIT)
            lin = jnp.clip(gl[:, 128:], -SWIGLU_LIMIT, SWIGLU_LIMIT)
            act = glu * jax.nn.sigmoid(SWIGLU_ALPHA * glu) * (lin + 1.0)
            o_ref[0:n_rows, c * 128:(c + 1) * 128] = act.astype(BF16)
        if n_rows < TILE_ROWS:
            o_ref[n_rows:, :] = jnp.zeros((TILE_ROWS - n_rows, UP_TF), BF16)

    _run_row_variant(nb, run)


def _run_row_variant(nb, run):
    lo = 0
    for blocks in TILE_ROW_VARIANTS:
        @pl.when((nb > lo) & (nb <= blocks))
        def _(blocks=blocks):
            run(blocks * SLOT_BLOCK)
        lo = blocks

    @pl.when(nb == 0)
    def _():
        run(0)


def _expert_up(xs, w_up, b_up, tile_e, tile_b0, tile_nb):
    nj = EXPERT_FF // UP_TF

    def x_spec(k):
        return pl.BlockSpec(
            (SLOT_BLOCK * ROW_TILE, 128),
            lambda s, j, e, b0, nb, k=k: (b0[s] + jnp.where(k < nb[s], k, 0), 0))

    def w_index(s, j, e, b0, nb):
        return (e[s], 0, jnp.where(nb[s] > 0, j, nj - 1))

    grid_spec = pltpu.PrefetchScalarGridSpec(
        num_scalar_prefetch=3,
        grid=(MAX_TILES, nj),
        in_specs=[x_spec(k) for k in range(BLOCKS_PER_TILE)] + [
            pl.BlockSpec((None, D_MODEL, 2 * UP_TF), w_index),
            pl.BlockSpec((None, 1, 2 * UP_TF), w_index)],
        out_specs=pl.BlockSpec((TILE_ROWS, UP_TF), lambda s, j, e, src, nb: (s, j)),
        scratch_shapes=[pltpu.VMEM((D_MODEL, 2 * UP_TF), BF16),
                        pltpu.VMEM((TILE_ROWS, D_MODEL), BF16)],
    )
    return pl.pallas_call(
        _up_kernel,
        grid_spec=grid_spec,
        out_shape=jax.ShapeDtypeStruct((MAX_TILES * TILE_ROWS, EXPERT_FF), BF16),
        compiler_params=_params(("arbitrary", "arbitrary")),
        name="expert_up",
    )(tile_e, tile_b0, tile_nb, *([xs] * BLOCKS_PER_TILE), w_up,
      b_up.reshape(N_EXPERTS, 1, 2 * EXPERT_FF))


DOWN_TN = TOKEN_TILE * 128


def _down_kernel(e_ref, src_ref, nb_ref, a_ref, wl_ref, wh_ref, bl_ref, bh_ref, o_ref, wlb_ref, whb_ref):
    nb = nb_ref[pl.program_id(0)]

    def run(n_rows):
        if n_rows == 0:
            o_ref[...] = jnp.zeros((TILE_ROWS * TOKEN_TILE, 128), U32)
            return
        wlb_ref[...] = wl_ref[...].astype(BF16)
        whb_ref[...] = wh_ref[...].astype(BF16)
        a = a_ref[0:n_rows, :]
        y_lo = jnp.dot(a, wlb_ref[...], preferred_element_type=F32) + bl_ref[...]
        y_hi = jnp.dot(a, whb_ref[...], preferred_element_type=F32) + bh_ref[...]
        packed = _pack_halves(jnp.concatenate([y_lo, y_hi], axis=1))
        for c in range(TOKEN_TILE):
            o_ref[pl.ds(c, n_rows, stride=TOKEN_TILE), :] = packed[:, c * 128:(c + 1) * 128]
        if n_rows < TILE_ROWS:
            o_ref[n_rows * TOKEN_TILE:, :] = jnp.zeros(((TILE_ROWS - n_rows) * TOKEN_TILE, 128), U32)

    _run_row_variant(nb, run)


def _expert_down(act, w_down, b_down, tile_e, tile_src, tile_nb):
    nj = HALF // DOWN_TN

    def w_index(hi):
        return lambda s, j, e, src, nb: (e[s], 0, hi * nj + jnp.where(nb[s] > 0, j, nj - 1))

    grid_spec = pltpu.PrefetchScalarGridSpec(
        num_scalar_prefetch=3,
        grid=(MAX_TILES, nj),
        in_specs=[pl.BlockSpec((TILE_ROWS, EXPERT_FF), lambda s, j, e, src, nb: (src[s], 0)),
                  pl.BlockSpec((None, EXPERT_FF, DOWN_TN), w_index(0)),
                  pl.BlockSpec((None, EXPERT_FF, DOWN_TN), w_index(1)),
                  pl.BlockSpec((None, 1, DOWN_TN), w_index(0)),
                  pl.BlockSpec((None, 1, DOWN_TN), w_index(1))],
        out_specs=pl.BlockSpec((None, TILE_ROWS * TOKEN_TILE, 128), lambda s, j, e, src, nb: (j, s, 0)),
        scratch_shapes=[pltpu.VMEM((EXPERT_FF, DOWN_TN), BF16), pltpu.VMEM((EXPERT_FF, DOWN_TN), BF16)],
    )
    b3 = b_down.reshape(N_EXPERTS, 1, D_MODEL)
    return pl.pallas_call(
        _down_kernel,
        grid_spec=grid_spec,
        out_shape=jax.ShapeDtypeStruct((2, MAX_TILES * TILE_ROWS * TOKEN_TILE, 128), U32),
        compiler_params=_params(("arbitrary", "arbitrary")),
        name="expert_down",
    )(tile_e, tile_src, tile_nb, act, w_down, w_down, b3, b3)


COMBINE_TM = 64


def _combine_kernel(dest_ref, next_ref, y_hbm, x_ref, wt_ref, g_ref, o_ref, buf_ref, sems):
    i = pl.program_id(0)
    cur = i % 2

    def tile_copy(buf, t, k, half, slot):
        src = y_hbm.at[half, pl.ds(pl.multiple_of(slot * TOKEN_TILE, TOKEN_TILE), TOKEN_TILE), :]
        dst = buf_ref.at[buf, k, half, pl.ds(pl.multiple_of(t * TOKEN_TILE, TOKEN_TILE), TOKEN_TILE), :]
        return pltpu.make_async_copy(src, dst, sems.at[buf])

    def issue_all(table_ref, buf):
        def issue(t, carry):
            for k in range(TOP_K):
                slot = table_ref[0, 0, t * TOP_K + k]
                tile_copy(buf, t, k, 0, slot).start()
                tile_copy(buf, t, k, 1, slot).start()
            return carry
        lax.fori_loop(0, COMBINE_TM, issue, 0, unroll=4)

    def drain(t, carry):
        for k in range(TOP_K):
            tile_copy(cur, t, k, 0, 0).wait()
            tile_copy(cur, t, k, 1, 0).wait()
        return carry

    @pl.when(i == 0)
    def _():
        issue_all(dest_ref, 0)

    @pl.when(i + 1 < pl.num_programs(0))
    def _():
        issue_all(next_ref, 1 - cur)

    lax.fori_loop(0, COMBINE_TM, drain, 0, unroll=4)

    weights = [wt_ref[:, k:k + 1] for k in range(TOP_K)]
    for half in range(2):
        for c in range(TOKEN_TILE):
            col = (half * TOKEN_TILE + c) * 128
            acc_lo = jnp.zeros((COMBINE_TM, 128), F32)
            acc_hi = jnp.zeros((COMBINE_TM, 128), F32)
            for k in range(TOP_K):
                lo, hi = _unpack_halves(_load_token_tile_column(buf_ref, (cur, k), 0, COMBINE_TM, half, c))
                acc_lo = acc_lo + weights[k] * lo
                acc_hi = acc_hi + weights[k] * hi
            o_ref[:, col:col + 128] = x_ref[:, col:col + 128] + g_ref[:, col:col + 128] * acc_lo
            hcol = HALF + col
            o_ref[:, hcol:hcol + 128] = x_ref[:, hcol:hcol + 128] + g_ref[:, hcol:hcol + 128] * acc_hi


def _combine(y_slots, dest, x1, top_w, mod):
    nt = SEQ // COMBINE_TM
    return pl.pallas_call(
        _combine_kernel,
        grid=(nt,),
        in_specs=[pl.BlockSpec((1, 1, COMBINE_TM * TOP_K), lambda i: (i, 0, 0), memory_space=pltpu.SMEM),
                  pl.BlockSpec((1, 1, COMBINE_TM * TOP_K), lambda i: (jnp.minimum(i + 1, nt - 1), 0, 0),
                               memory_space=pltpu.SMEM),
                  pl.BlockSpec(memory_space=pl.ANY),
                  pl.BlockSpec((COMBINE_TM, D_MODEL), lambda i: (i, 0)),
                  pl.BlockSpec((COMBINE_TM, TOP_K), lambda i: (i, 0)),
                  pl.BlockSpec((1, D_MODEL), lambda i: (0, 5))],
        out_specs=pl.BlockSpec((COMBINE_TM, D_MODEL), lambda i: (i, 0)),
        out_shape=jax.ShapeDtypeStruct((SEQ, D_MODEL), F32),
        scratch_shapes=[pltpu.VMEM((2, TOP_K, 2, COMBINE_TM * TOKEN_TILE, 128), U32),
                        pltpu.SemaphoreType.DMA((2,))],
        compiler_params=_params(("arbitrary",)),
        name="moe_combine",
    )(dest.reshape(nt, 1, COMBINE_TM * TOP_K), dest.reshape(nt, 1, COMBINE_TM * TOP_K), y_slots, x1, top_w, mod)


def _layer(x, c, norm1_g, norm2_g, w_ada, b_ada, w_in, q_norm_g, k_norm_g, rel_bias, conv_w, conv_b,
           conv_norm_g, w_attn_out, w_conv_out, w_o, w_router, b_router, w_up, b_up, w_down, b_down):
    mod = _ada(c, w_ada, b_ada)
    h = _norm1(x, norm1_g, mod)

    outs, lses = [], []
    for group, (_, dil) in enumerate(ATTN_GROUPS):
        qkv = _qkv_proj(h, w_in, q_norm_g, k_norm_g, group, dil)
        o, lse = _band_attention(qkv, _bias_tiles(rel_bias, group, dil), dil)
        outs.append(o)
        lses.append(lse)
    attn = _merge_groups(outs, lses)

    u = _conv_module(_conv_glu_proj(h, w_in), conv_w, conv_b, conv_norm_g)
    gates = _gate_proj(h, w_in)
    merged = _branch_merge(attn, u, w_attn_out, w_conv_out, gates)
    x1 = _wo_residual(merged, w_o, x, mod)

    hp, top_idx, top_w = _norm2_router(x1, norm2_g, mod, w_router, b_router)
    dest, slot_tok, tile_e, tile_b0, tile_src, tile_nb = _routing_tables(top_idx)
    xs = _gather_tokens(hp, slot_tok)
    act = _expert_up(xs, w_up, b_up, tile_e, tile_b0, tile_nb)
    y_slots = _expert_down(act, w_down, b_down, tile_e, tile_src, tile_nb)
    return _combine(y_slots, dest, x1, top_w, mod)


def kernel(x, c, norm1_g, norm2_g, w_ada, b_ada, w_in, q_norm_g, k_norm_g, rel_bias, conv_w, conv_b,
           conv_norm_g, w_attn_out, w_conv_out, w_o, w_router, b_router, w_up, b_up, w_down, b_down):
    batch = x.shape[0]
    xs = x.reshape(batch * SEQ, D_MODEL)
    for layer in range(w_in.shape[0]):
        xs = _layer(xs, c, norm1_g[layer], norm2_g[layer], w_ada[layer], b_ada[layer], w_in[layer],
                    q_norm_g[layer], k_norm_g[layer], rel_bias, conv_w[layer], conv_b[layer],
                    conv_norm_g[layer], w_attn_out[layer], w_conv_out[layer], w_o[layer],
                    w_router[layer], b_router[layer], w_up[layer], b_up[layer], w_down[layer], b_down[layer])
    return xs.reshape(x.shape)
```

```python
import functools
import math

import numpy as np
import jax
import jax.numpy as jnp
from jax import lax
from jax.experimental import pallas as pl
from jax.experimental.pallas import tpu as pltpu

D_MODEL = 4096
SEQ = 8192
HEAD_DIM = 128
ATTN_GROUPS = ((128, 1), (512, 4), (2048, 16))
N_GROUPS = len(ATTN_GROUPS)
HEADS_PER_GROUP = 8
GROUP_WIDTH = HEADS_PER_GROUP * HEAD_DIM
ATTN_WIDTH = N_GROUPS * GROUP_WIDTH
CONV_WIDTH = D_MODEL // 2
CONV_KERNEL = 31
CONV_OFF = 3 * ATTN_WIDTH
GATE_OFF = CONV_OFF + 2 * CONV_WIDTH
IN_WIDTH = GATE_OFF + 2 * D_MODEL
REL_BUCKETS = 32
REL_MAX_DISTANCE = 1024
N_EXPERTS = 32
TOP_K = 4
EXPERT_FF = 1536
SWIGLU_ALPHA = 1.702
SWIGLU_LIMIT = 7.0
NORM_EPS = 1e-6
NEG_INF = -1e30
N_SIDE = 64

V7X_VMEM_LIMIT = 56 * 1024 * 1024

SLOT_BLOCK = 256
BLOCKS_PER_TILE = 5
N_ASSIGN = SEQ * TOP_K
TILE_ROWS = SLOT_BLOCK * BLOCKS_PER_TILE
TILE_ROW_VARIANTS = (4, 5)
N_SLOT_BLOCKS = (N_ASSIGN + N_EXPERTS * (SLOT_BLOCK - 1)) // SLOT_BLOCK + 1
MAX_TILES = (N_SLOT_BLOCKS + N_EXPERTS * (BLOCKS_PER_TILE - 1)) // BLOCKS_PER_TILE
HALF = D_MODEL // 2

F32 = jnp.float32
BF16 = jnp.bfloat16
U32 = jnp.uint32


def _params(sem, vmem=V7X_VMEM_LIMIT):
    return pltpu.CompilerParams(dimension_semantics=sem, vmem_limit_bytes=vmem)


def _pack_halves(v):
    n = v.shape[-1] // 2
    lo = pltpu.bitcast(v[:, :n].astype(BF16).astype(F32), U32) >> 16
    hi = pltpu.bitcast(v[:, n:].astype(BF16).astype(F32), U32) & jnp.uint32(0xFFFF0000)
    return lo | hi


def _unpack_halves(p):
    lo = pltpu.bitcast(p << 16, F32)
    hi = pltpu.bitcast(p & jnp.uint32(0xFFFF0000), F32)
    return lo, hi


TOKEN_TILE = 8


def _store_token_tiles(ref, row0, rows, packed):
    for half in range(2):
        for c in range(TOKEN_TILE):
            col = (half * TOKEN_TILE + c) * 128
            ref[half, pl.ds(row0 * TOKEN_TILE + c, rows, stride=TOKEN_TILE), :] = packed[:, col:col + 128]


ROW_TILE = HALF // 128


def _load_token_tile_column(ref, lead, row0, rows, half, c):
    return ref[(*lead, half, pl.ds(row0 * TOKEN_TILE + c, rows, stride=TOKEN_TILE), slice(None))]


def _ada_kernel(c_ref, w_ref, b_ref, o_ref, cs_ref):
    @pl.when(pl.program_id(0) == 0)
    def _():
        c = c_ref[...]
        cs_ref[...] = c * jax.nn.sigmoid(c)

    tn = o_ref.shape[-1]
    rows_per_step = 64

    def body(r, acc):
        rows = pl.ds(pl.multiple_of(r * rows_per_step, rows_per_step), rows_per_step)
        cs = cs_ref[rows, :]
        prod = w_ref[rows, :] * jnp.concatenate([cs] * (tn // 128), axis=1)
        return acc + prod.reshape(rows_per_step // 8, 8, tn).sum(axis=0)

    acc = lax.fori_loop(0, D_MODEL // rows_per_step, body, jnp.zeros((8, tn), F32))
    o_ref[...] = jnp.sum(acc, axis=0, keepdims=True) + b_ref[...]


def _ada(c, w_ada, b_ada):
    tn = 512
    n = w_ada.shape[-1]
    c_lanes = jnp.broadcast_to(c.reshape(D_MODEL, 1), (D_MODEL, 128))
    return pl.pallas_call(
        _ada_kernel,
        grid=(n // tn,),
        in_specs=[pl.BlockSpec((D_MODEL, 128), lambda j: (0, 0)),
                  pl.BlockSpec((D_MODEL, tn), lambda j: (0, j)),
                  pl.BlockSpec((1, tn), lambda j: (0, j))],
        out_specs=pl.BlockSpec((1, tn), lambda j: (0, j)),
        out_shape=jax.ShapeDtypeStruct((1, n), F32),
        scratch_shapes=[pltpu.VMEM((D_MODEL, 128), F32)],
        compiler_params=_params(("arbitrary",)),
        name="ada_mod",
    )(c_lanes, w_ada, b_ada.reshape(1, n))


def _modulated_norm(x, g, scale, shift):
    ms = jnp.mean(x * x, axis=-1, keepdims=True)
    y = x * lax.rsqrt(ms + NORM_EPS) * g
    return y * (1.0 + scale) + shift


def _norm1_kernel(x_ref, g_ref, sc_ref, sh_ref, o_ref):
    o_ref[...] = _modulated_norm(x_ref[...], g_ref[...], sc_ref[...], sh_ref[...]).astype(o_ref.dtype)


def _norm1(x, g, mod):
    tm = 256
    vec = lambda k: pl.BlockSpec((1, D_MODEL), lambda i, k=k: (0, k))
    return pl.pallas_call(
        _norm1_kernel,
        grid=(SEQ // tm,),
        in_specs=[pl.BlockSpec((tm, D_MODEL), lambda i: (i, 0)), vec(0), vec(1), vec(0)],
        out_specs=pl.BlockSpec((tm, D_MODEL), lambda i: (i, 0)),
        out_shape=jax.ShapeDtypeStruct((SEQ, D_MODEL), BF16),
        compiler_params=_params(("arbitrary",)),
        name="norm1",
    )(x, g.reshape(1, D_MODEL), mod, mod)


def _qkv_kernel(h_ref, w_ref, gq_ref, gk_ref, o_ref, wb_ref, acc_ref, *, dil):
    which = pl.program_id(0)
    tm = h_ref.shape[0]

    @pl.when(pl.program_id(1) == 0)
    def _():
        wb_ref[...] = w_ref[...].astype(BF16)

    acc = jnp.dot(h_ref[...], wb_ref[...], preferred_element_type=F32)

    is_v = which == 2
    gain = jnp.where(is_v, 1.0, jnp.where(which == 0, gq_ref[...] * HEAD_DIM ** -0.5, gk_ref[...]))
    for h in range(HEADS_PER_GROUP):
        sl = slice(h * HEAD_DIM, (h + 1) * HEAD_DIM)
        blk = acc[:, sl]
        ms = jnp.mean(blk * blk, axis=-1, keepdims=True)
        vals = blk * jnp.where(is_v, 1.0, lax.rsqrt(ms + NORM_EPS)) * gain
        if dil == 1:
            o_ref[0, :, sl] = vals.astype(BF16)
        else:
            acc_ref[h] = vals
            for r in range(dil):
                o_ref[r, :, sl] = acc_ref[h, pl.ds(r, tm // dil, stride=dil), :].astype(BF16)


def _qkv_proj(h, w_in, gq, gk, group, dil):
    tm = 512
    seg = SEQ // dil
    return pl.pallas_call(
        functools.partial(_qkv_kernel, dil=dil),
        grid=(3, SEQ // tm),
        in_specs=[pl.BlockSpec((tm, D_MODEL), lambda w, i: (i, 0)),
                  pl.BlockSpec((D_MODEL, GROUP_WIDTH), lambda w, i: (0, w * N_GROUPS + group),
                               pipeline_mode=pl.Buffered(1)),
                  pl.BlockSpec((1, HEAD_DIM), lambda w, i: (0, 0)),
                  pl.BlockSpec((1, HEAD_DIM), lambda w, i: (0, 0))],
        out_specs=pl.BlockSpec((None, dil, tm // dil, GROUP_WIDTH), lambda w, i: (w, 0, i, 0)),
        out_shape=jax.ShapeDtypeStruct((3, dil, seg, GROUP_WIDTH), BF16),
        scratch_shapes=[pltpu.VMEM((D_MODEL, GROUP_WIDTH), BF16), pltpu.VMEM((HEADS_PER_GROUP, tm, HEAD_DIM), F32)],
        compiler_params=_params(("arbitrary", "arbitrary")),
        name=f"qkv_proj_g{group}",
    )(h, w_in, gq.reshape(1, HEAD_DIM), gk.reshape(1, HEAD_DIM))


def _glu_kernel(h_ref, wa_ref, wg_ref, o_ref, wab_ref, wgb_ref):
    @pl.when(pl.program_id(1) == 0)
    def _():
        wab_ref[...] = wa_ref[...].astype(BF16)
        wgb_ref[...] = wg_ref[...].astype(BF16)

    h = h_ref[...]
    a = jnp.dot(h, wab_ref[...], preferred_element_type=F32)
    g = jnp.dot(h, wgb_ref[...], preferred_element_type=F32)
    o_ref[...] = (a * jax.nn.sigmoid(g)).astype(o_ref.dtype)


def _conv_glu_proj(h, w_in):
    tm, tn = 512, 512
    a0 = CONV_OFF // tn
    g0 = (CONV_OFF + CONV_WIDTH) // tn
    return pl.pallas_call(
        _glu_kernel,
        grid=(CONV_WIDTH // tn, SEQ // tm),
        in_specs=[pl.BlockSpec((tm, D_MODEL), lambda j, i: (i, 0)),
                  pl.BlockSpec((D_MODEL, tn), lambda j, i: (0, a0 + j)),
                  pl.BlockSpec((D_MODEL, tn), lambda j, i: (0, g0 + j))],
        out_specs=pl.BlockSpec((tm, tn), lambda j, i: (i, j)),
        out_shape=jax.ShapeDtypeStruct((SEQ, CONV_WIDTH), BF16),
        scratch_shapes=[pltpu.VMEM((D_MODEL, tn), BF16), pltpu.VMEM((D_MODEL, tn), BF16)],
        compiler_params=_params(("arbitrary", "arbitrary")),
        name="conv_glu_proj",
    )(h, w_in, w_in)


def _gate_kernel(h_ref, w_ref, o_ref, wb_ref):
    @pl.when(pl.program_id(1) == 0)
    def _():
        wb_ref[...] = w_ref[...].astype(BF16)

    acc = jnp.dot(h_ref[...], wb_ref[...], preferred_element_type=F32)
    o_ref[...] = jax.nn.sigmoid(acc).astype(o_ref.dtype)


def _gate_proj(h, w_in):
    tm, tn = 1024, 512
    c0 = GATE_OFF // tn
    return pl.pallas_call(
        _gate_kernel,
        grid=(2 * D_MODEL // tn, SEQ // tm),
        in_specs=[pl.BlockSpec((tm, D_MODEL), lambda j, i: (i, 0)),
                  pl.BlockSpec((D_MODEL, tn), lambda j, i: (0, c0 + j))],
        out_specs=pl.BlockSpec((tm, tn), lambda j, i: (i, j)),
        out_shape=jax.ShapeDtypeStruct((SEQ, 2 * D_MODEL), BF16),
        scratch_shapes=[pltpu.VMEM((D_MODEL, tn), BF16)],
        compiler_params=_params(("arbitrary", "arbitrary")),
        name="gate_proj",
    )(h, w_in)


ATTN_TQ = 128
ATTN_TK = ATTN_TQ + 2 * N_SIDE
LSE_LANES = 128


def _bucket_tile(dil):
    a = np.arange(ATTN_TQ)[:, None]
    j = np.arange(ATTN_TK)[None, :]
    steps = j - N_SIDE - a
    rel = steps * dil
    nb = REL_BUCKETS // 2
    max_exact = nb // 2
    n = np.abs(rel)
    side = np.where(rel > 0, nb, 0)
    nf = np.maximum(n, 1).astype(np.float32)
    large = max_exact + (np.log(nf / np.float32(max_exact)) / np.float32(math.log(REL_MAX_DISTANCE / max_exact))
                         * np.float32(nb - max_exact)).astype(np.int32)
    large = np.minimum(large, nb - 1)
    bucket = side + np.where(n < max_exact, n, large)
    return np.where(np.abs(steps) <= N_SIDE, bucket, -1).astype(np.int32)


def _bias_kernel(tbl_ref, idx_ref, o_ref, *, group):
    h = pl.program_id(0)
    idx = idx_ref[...]
    acc = jnp.full(idx.shape, NEG_INF, F32)
    for b in range(REL_BUCKETS):
        acc = jnp.where(idx == b, tbl_ref[b, group * HEADS_PER_GROUP + h], acc)
    o_ref[...] = acc


def _bias_tiles(rel_bias, group, dil):
    return pl.pallas_call(
        functools.partial(_bias_kernel, group=group),
        grid=(HEADS_PER_GROUP,),
        in_specs=[pl.BlockSpec(memory_space=pltpu.SMEM),
                  pl.BlockSpec((ATTN_TQ, ATTN_TK), lambda h: (0, 0))],
        out_specs=pl.BlockSpec((None, ATTN_TQ, ATTN_TK), lambda h: (h, 0, 0)),
        out_shape=jax.ShapeDtypeStruct((HEADS_PER_GROUP, ATTN_TQ, ATTN_TK), F32),
        compiler_params=_params(("arbitrary",)),
        name=f"attn_bias_g{group}",
    )(rel_bias, jnp.asarray(_bucket_tile(dil)))


def _attn_kernel(q_ref, kp_ref, kc_ref, kn_ref, vp_ref, vc_ref, vn_ref, bias_ref, o_ref, lse_ref, *, seg):
    i = pl.program_id(1)
    q = q_ref[...]
    k = jnp.concatenate([kp_ref[...], kc_ref[...], kn_ref[...]], axis=0)
    v = jnp.concatenate([vp_ref[...], vc_ref[...], vn_ref[...]], axis=0)
    kpos = i * ATTN_TQ - N_SIDE + lax.broadcasted_iota(jnp.int32, (1, ATTN_TK), 1)
    valid = (kpos >= 0) & (kpos < seg)
    lane = lax.broadcasted_iota(jnp.int32, (ATTN_TQ, LSE_LANES), 1)
    lse_all = jnp.zeros((ATTN_TQ, LSE_LANES), F32)
    for h in range(HEADS_PER_GROUP):
        sl = slice(h * HEAD_DIM, (h + 1) * HEAD_DIM)
        s = lax.dot_general(q[:, sl], k[:, sl], (((1,), (1,)), ((), ())), preferred_element_type=F32)
        s = jnp.where(valid, s + bias_ref[h], NEG_INF)
        m = jnp.max(s, axis=-1, keepdims=True)
        p = jnp.exp(s - m)
        l = jnp.sum(p, axis=-1, keepdims=True)
        o = jnp.dot(p.astype(BF16), v[:, sl], preferred_element_type=F32)
        o_ref[:, sl] = (o / l).astype(o_ref.dtype)
        lse_all = jnp.where(lane == h, m + jnp.log(l), lse_all)
    lse_ref[...] = lse_all


def _band_attention(qkv, bias, dil):
    seg = SEQ // dil
    sub = ATTN_TQ // N_SIDE
    last = seg // N_SIDE - 1

    def cur(which):
        return pl.BlockSpec((None, None, ATTN_TQ, GROUP_WIDTH), lambda r, i, w=which: (w, r, i, 0))

    def prev(which):
        return pl.BlockSpec((None, None, N_SIDE, GROUP_WIDTH),
                            lambda r, i, w=which: (w, r, jnp.maximum(i * sub - 1, 0), 0))

    def nxt(which):
        return pl.BlockSpec((None, None, N_SIDE, GROUP_WIDTH),
                            lambda r, i, w=which: (w, r, jnp.minimum((i + 1) * sub, last), 0))

    return pl.pallas_call(
        functools.partial(_attn_kernel, seg=seg),
        grid=(dil, seg // ATTN_TQ),
        in_specs=[cur(0), prev(1), cur(1), nxt(1), prev(2), cur(2), nxt(2),
                  pl.BlockSpec((HEADS_PER_GROUP, ATTN_TQ, ATTN_TK), lambda r, i: (0, 0, 0))],
        out_specs=[pl.BlockSpec((None, ATTN_TQ, GROUP_WIDTH), lambda r, i: (r, i, 0)),
                   pl.BlockSpec((None, ATTN_TQ, LSE_LANES), lambda r, i: (r, i, 0))],
        out_shape=[jax.ShapeDtypeStruct((dil, seg, GROUP_WIDTH), BF16),
                   jax.ShapeDtypeStruct((dil, seg, LSE_LANES), F32)],
        compiler_params=_params(("arbitrary", "arbitrary")),
        name=f"band_attn_d{dil}",
    )(qkv, qkv, qkv, qkv, qkv, qkv, qkv, bias)


def _merge_kernel(o0_ref, o1_ref, o2_ref, l0_ref, l1_ref, l2_ref, out_ref, buf_ref, lbuf_ref):
    tm = out_ref.shape[0]
    for gi, (o_ref, l_ref) in enumerate(((o1_ref, l1_ref), (o2_ref, l2_ref))):
        dil = o_ref.shape[0]
        for r in range(dil):
            rows = pl.ds(r, tm // dil, stride=dil)
            lbuf_ref[gi, rows, :] = l_ref[r]
            for h in range(HEADS_PER_GROUP):
                buf_ref[gi * HEADS_PER_GROUP + h, rows, :] = (
                    o_ref[r, :, h * HEAD_DIM:(h + 1) * HEAD_DIM].astype(F32))
    lse0, lse1, lse2 = l0_ref[0], lbuf_ref[0], lbuf_ref[1]
    top = jnp.maximum(jnp.maximum(lse0, lse1), lse2)
    w0, w1, w2 = jnp.exp(lse0 - top), jnp.exp(lse1 - top), jnp.exp(lse2 - top)
    den = w0 + w1 + w2
    for h in range(HEADS_PER_GROUP):
        sl = slice(h * HEAD_DIM, (h + 1) * HEAD_DIM)
        col = slice(h, h + 1)
        num = (w0[:, col] * o0_ref[0, :, sl].astype(F32) + w1[:, col] * buf_ref[h]
               + w2[:, col] * buf_ref[HEADS_PER_GROUP + h])
        out_ref[:, sl] = (num / den[:, col]).astype(out_ref.dtype)


def _merge_groups(outs, lses):
    tm = 256
    d1, d2 = ATTN_GROUPS[1][1], ATTN_GROUPS[2][1]

    def spec(dil, width):
        return pl.BlockSpec((dil, tm // dil, width), lambda i: (0, i, 0))

    return pl.pallas_call(
        _merge_kernel,
        grid=(SEQ // tm,),
        in_specs=[spec(1, GROUP_WIDTH), spec(d1, GROUP_WIDTH), spec(d2, GROUP_WIDTH),
                  spec(1, LSE_LANES), spec(d1, LSE_LANES), spec(d2, LSE_LANES)],
        out_specs=pl.BlockSpec((tm, GROUP_WIDTH), lambda i: (i, 0)),
        out_shape=jax.ShapeDtypeStruct((SEQ, GROUP_WIDTH), BF16),
        scratch_shapes=[pltpu.VMEM((2 * HEADS_PER_GROUP, tm, HEAD_DIM), F32), pltpu.VMEM((2, tm, LSE_LANES), F32)],
        compiler_params=_params(("arbitrary",)),
        name="attn_merge",
    )(*outs, *lses)


CONV_HALO = 16


def _conv_kernel(up_ref, uc_ref, un_ref, w_ref, b_ref, g_ref, o_ref, buf_ref, y_ref):
    i = pl.program_id(0)
    tm = uc_ref.shape[0]
    first = i == 0
    last = i == pl.num_programs(0) - 1
    buf_ref[0:CONV_HALO, :] = jnp.where(first, 0.0, up_ref[...].astype(F32))
    buf_ref[CONV_HALO:CONV_HALO + tm, :] = uc_ref[...].astype(F32)
    buf_ref[CONV_HALO + tm:, :] = jnp.where(last, 0.0, un_ref[...].astype(F32))
    pad = CONV_KERNEL // 2
    lanes = 256
    for c0 in range(0, CONV_WIDTH, lanes):
        cs = slice(c0, c0 + lanes)
        part = jnp.broadcast_to(b_ref[:, cs], (tm, lanes))
        first = CONV_HALO - pad
        for s in range(8):
            group = None
            for tap in range(CONV_KERNEL):
                off = first + tap
                if off % 8 != s:
                    continue
                term = w_ref[tap:tap + 1, cs] * buf_ref[pl.ds(off - s, tm + 8), cs]
                group = term if group is None else group + term
            part = part + group[s:s + tm, :]
        y_ref[:, cs] = part
    acc = y_ref[...]
    ms = jnp.mean(acc * acc, axis=-1, keepdims=True)
    y = acc * lax.rsqrt(ms + NORM_EPS) * g_ref[...]
    o_ref[...] = (y * jax.nn.sigmoid(y)).astype(o_ref.dtype)


def _conv_module(u, conv_w, conv_b, conv_norm_g):
    tm = 128
    sub = tm // CONV_HALO
    last = SEQ // CONV_HALO - 1
    return pl.pallas_call(
        _conv_kernel,
        grid=(SEQ // tm,),
        in_specs=[pl.BlockSpec((CONV_HALO, CONV_WIDTH), lambda i: (jnp.maximum(i * sub - 1, 0), 0)),
                  pl.BlockSpec((tm, CONV_WIDTH), lambda i: (i, 0)),
                  pl.BlockSpec((CONV_HALO, CONV_WIDTH), lambda i: (jnp.minimum((i + 1) * sub, last), 0)),
                  pl.BlockSpec((CONV_KERNEL, CONV_WIDTH), lambda i: (0, 0)),
                  pl.BlockSpec((1, CONV_WIDTH), lambda i: (0, 0)),
                  pl.BlockSpec((1, CONV_WIDTH), lambda i: (0, 0))],
        out_specs=pl.BlockSpec((tm, CONV_WIDTH), lambda i: (i, 0)),
        out_shape=jax.ShapeDtypeStruct((SEQ, CONV_WIDTH), BF16),
        scratch_shapes=[pltpu.VMEM((tm + 2 * CONV_HALO, CONV_WIDTH), F32), pltpu.VMEM((tm, CONV_WIDTH), F32)],
        compiler_params=_params(("arbitrary",)),
        name="conv_module",
    )(u, u, u, conv_w, conv_b.reshape(1, CONV_WIDTH), conv_norm_g.reshape(1, CONV_WIDTH))


def _branch_kernel(a_ref, u_ref, wa_ref, wc_ref, ga_ref, gc_ref, o_ref, wab_ref, wcb_ref):
    @pl.when(pl.program_id(1) == 0)
    def _():
        wab_ref[...] = wa_ref[...].astype(BF16)
        wcb_ref[...] = wc_ref[...].astype(BF16)

    attn = jnp.dot(a_ref[...], wab_ref[...], preferred_element_type=F32)
    conv = jnp.dot(u_ref[...], wcb_ref[...], preferred_element_type=F32)
    o_ref[...] = (ga_ref[...].astype(F32) * attn + gc_ref[...].astype(F32) * conv).astype(o_ref.dtype)


def _branch_merge(attn, conv, w_attn_out, w_conv_out, gates):
    tm, tn = 512, 512
    nj = D_MODEL // tn
    return pl.pallas_call(
        _branch_kernel,
        grid=(nj, SEQ // tm),
        in_specs=[pl.BlockSpec((tm, GROUP_WIDTH), lambda j, i: (i, 0)),
                  pl.BlockSpec((tm, CONV_WIDTH), lambda j, i: (i, 0)),
                  pl.BlockSpec((GROUP_WIDTH, tn), lambda j, i: (0, j)),
                  pl.BlockSpec((CONV_WIDTH, tn), lambda j, i: (0, j)),
                  pl.BlockSpec((tm, tn), lambda j, i: (i, j)),
                  pl.BlockSpec((tm, tn), lambda j, i: (i, nj + j))],
        out_specs=pl.BlockSpec((tm, tn), lambda j, i: (i, j)),
        out_shape=jax.ShapeDtypeStruct((SEQ, D_MODEL), BF16),
        scratch_shapes=[pltpu.VMEM((GROUP_WIDTH, tn), BF16), pltpu.VMEM((CONV_WIDTH, tn), BF16)],
        compiler_params=_params(("arbitrary", "arbitrary")),
        name="branch_merge",
    )(attn, conv, w_attn_out, w_conv_out, gates, gates)


def _wo_kernel(m_ref, w_ref, x_ref, g_ref, o_ref, wb_ref):
    @pl.when(pl.program_id(1) == 0)
    def _():
        wb_ref[...] = w_ref[...].astype(BF16)

    y = jnp.dot(m_ref[...], wb_ref[...], preferred_element_type=F32)
    o_ref[...] = x_ref[...] + g_ref[...] * y


def _wo_residual(merged, w_o, x, mod):
    tm, tn = 1024, 512
    nj = D_MODEL // tn
    return pl.pallas_call(
        _wo_kernel,
        grid=(nj, SEQ // tm),
        in_specs=[pl.BlockSpec((tm, D_MODEL), lambda j, i: (i, 0)),
                  pl.BlockSpec((D_MODEL, tn), lambda j, i: (0, j)),
                  pl.BlockSpec((tm, tn), lambda j, i: (i, j)),
                  pl.BlockSpec((1, tn), lambda j, i: (0, 2 * nj + j))],
        out_specs=pl.BlockSpec((tm, tn), lambda j, i: (i, j)),
        out_shape=jax.ShapeDtypeStruct((SEQ, D_MODEL), F32),
        scratch_shapes=[pltpu.VMEM((D_MODEL, tn), BF16)],
        compiler_params=_params(("arbitrary", "arbitrary")),
        name="wo_residual",
    )(merged, w_o, x, mod)


def _router_kernel(x_ref, g_ref, sc_ref, sh_ref, wr_ref, br_ref, hp_ref, idx_ref, wt_ref):
    tm = x_ref.shape[0]
    h = _modulated_norm(x_ref[...], g_ref[...], sc_ref[...], sh_ref[...])
    packed = _pack_halves(h)
    for q in range(ROW_TILE):
        hp_ref[pl.ds(q, tm, stride=ROW_TILE), :] = packed[:, q * 128:(q + 1) * 128]

    h_hi = h.astype(BF16)
    h_lo = (h - h_hi.astype(F32)).astype(BF16)
    w = wr_ref[...]
    w_hi = w.astype(BF16)
    w_lo = (w - w_hi.astype(F32)).astype(BF16)
    logits = (jnp.dot(h_hi, w_hi, preferred_element_type=F32) + jnp.dot(h_lo, w_hi, preferred_element_type=F32)
              + jnp.dot(h_hi, w_lo, preferred_element_type=F32) + br_ref[...])

    lane = lax.broadcasted_iota(jnp.int32, (tm, N_EXPERTS), 1)
    work = logits
    vals, idxs = [], []
    for _ in range(TOP_K):
        m = jnp.max(work, axis=-1, keepdims=True)
        idx = jnp.min(jnp.where(work == m, lane, N_EXPERTS), axis=-1, keepdims=True)
        vals.append(m)
        idxs.append(idx)
        work = jnp.where(lane == idx, -jnp.inf, work)
    exps = [jnp.exp(v - vals[0]) for v in vals]
    den = exps[0] + exps[1] + exps[2] + exps[3]
    col = lax.broadcasted_iota(jnp.int32, (tm, TOP_K), 1)
    idx_out = jnp.zeros((tm, TOP_K), jnp.int32)
    wt_out = jnp.zeros((tm, TOP_K), F32)
    for k in range(TOP_K):
        idx_out = jnp.where(col == k, idxs[k], idx_out)
        wt_out = jnp.where(col == k, exps[k] / den, wt_out)
    idx_ref[...] = idx_out
    wt_ref[...] = wt_out


def _norm2_router(x1, g, mod, w_router, b_router):
    tm = 256
    vec = lambda k: pl.BlockSpec((1, D_MODEL), lambda i, k=k: (0, k))
    return pl.pallas_call(
        _router_kernel,
        grid=(SEQ // tm,),
        in_specs=[pl.BlockSpec((tm, D_MODEL), lambda i: (i, 0)), vec(0), vec(4), vec(3),
                  pl.BlockSpec((D_MODEL, N_EXPERTS), lambda i: (0, 0)),
                  pl.BlockSpec((1, N_EXPERTS), lambda i: (0, 0))],
        out_specs=[pl.BlockSpec((tm * ROW_TILE, 128), lambda i: (i, 0)),
                   pl.BlockSpec((tm, TOP_K), lambda i: (i, 0)),
                   pl.BlockSpec((tm, TOP_K), lambda i: (i, 0))],
        out_shape=[jax.ShapeDtypeStruct((SEQ * ROW_TILE, 128), U32),
                   jax.ShapeDtypeStruct((SEQ, TOP_K), jnp.int32),
                   jax.ShapeDtypeStruct((SEQ, TOP_K), F32)],
        compiler_params=_params(("arbitrary",)),
        name="norm2_router",
    )(x1, g.reshape(1, D_MODEL), mod, mod, w_router, b_router.reshape(1, N_EXPERTS))


def _routing_tables(top_idx):
    i32 = jnp.int32
    e_flat = top_idx.reshape(-1)
    onehot = (e_flat[:, None] == jnp.arange(N_EXPERTS, dtype=i32)[None, :]).astype(i32)
    csum = jnp.cumsum(onehot, axis=0)
    rank = jnp.sum(onehot * csum, axis=1) - 1
    counts = csum[-1]
    nblk = (counts + SLOT_BLOCK - 1) // SLOT_BLOCK
    blk_end = jnp.cumsum(nblk)
    blk_start = blk_end - nblk
    ntile = (nblk + BLOCKS_PER_TILE - 1) // BLOCKS_PER_TILE
    tile_end = jnp.cumsum(ntile)
    tile_start = tile_end - ntile
    n_tiles = tile_end[-1]
    n_blocks = blk_end[-1]

    dest = (jnp.sum(onehot * (tile_start * TILE_ROWS)[None, :], axis=1) + rank).astype(i32)
    packed = jnp.sum(onehot * (blk_start * SLOT_BLOCK)[None, :], axis=1) + rank
    tok = jnp.arange(N_ASSIGN, dtype=i32) // TOP_K
    slot_tok = jnp.zeros((N_SLOT_BLOCKS * SLOT_BLOCK,), i32).at[packed].set(tok, unique_indices=True)

    s_raw = jnp.arange(MAX_TILES, dtype=i32)
    s = jnp.minimum(s_raw, n_tiles - 1)
    tile_e = jnp.minimum(jnp.searchsorted(tile_end, s, side='right'), N_EXPERTS - 1).astype(i32)
    local = s - tile_start[tile_e]
    tile_nb = jnp.clip(nblk[tile_e] - BLOCKS_PER_TILE * local, 0, BLOCKS_PER_TILE)
    tile_nb = jnp.where(s_raw < n_tiles, tile_nb, 0).astype(i32)
    tile_b0 = blk_start[tile_e] + BLOCKS_PER_TILE * local
    tile_src = s.astype(i32)
    rows = tile_b0[:, None] * SLOT_BLOCK + jnp.arange(TILE_ROWS, dtype=i32)[None, :]
    tile_tok = slot_tok[jnp.minimum(rows, N_SLOT_BLOCKS * SLOT_BLOCK - 1)].reshape(MAX_TILES, 1, TILE_ROWS)
    return dest, tile_tok, tile_e, tile_src, tile_nb


UP_TF = 256


def _deinterleave_matrix():
    r = lax.broadcasted_iota(jnp.int32, (256, 256), 0)
    c = lax.broadcasted_iota(jnp.int32, (256, 256), 1)
    src = jnp.where(c < 128, 2 * c, 2 * (c - 128) + 1)
    return jnp.where(r == src, 1.0, 0.0).astype(BF16)


def _up_kernel(e_ref, nb_ref, first_ref, next_ref, h_hbm, w_ref, b_ref, o_ref, wb_ref, xb_ref, g_ref, sem):
    s = pl.program_id(0)
    j = pl.program_id(1)
    nb = nb_ref[s]

    def row_copy(r, tok):
        src = h_hbm.at[pl.ds(pl.multiple_of(tok * ROW_TILE, ROW_TILE), ROW_TILE), :]
        dst = g_ref.at[pl.ds(pl.multiple_of(r * ROW_TILE, ROW_TILE), ROW_TILE), :]
        return pltpu.make_async_copy(src, dst, sem)

    def request_block(table_ref, k):
        def issue(i, carry):
            for u in range(2):
                r = k * SLOT_BLOCK + 2 * i + u
                row_copy(r, table_ref[0, 0, r]).start(priority=u)
            return carry
        lax.fori_loop(0, SLOT_BLOCK // 2, issue, 0, unroll=4)

    @pl.when((s == 0) & (j == 0))
    def _():
        for k in range(BLOCKS_PER_TILE):
            @pl.when(k < nb)
            def _(k=k):
                request_block(first_ref, k)

    @pl.when(j == 0)
    def _():
        def drain_row(r, carry):
            row_copy(0, 0).wait()
            return carry

        def drain_block(k, carry):
            return lax.fori_loop(0, SLOT_BLOCK, drain_row, carry, unroll=8)

        lax.fori_loop(0, nb, drain_block, 0)

        for k in range(BLOCKS_PER_TILE):
            rows = slice(k * SLOT_BLOCK, (k + 1) * SLOT_BLOCK)

            @pl.when(k < nb)
            def _(k=k, rows=rows):
                for q in range(ROW_TILE):
                    lo, hi = _unpack_halves(g_ref[pl.ds(k * SLOT_BLOCK * ROW_TILE + q, SLOT_BLOCK, stride=ROW_TILE), :])
                    xb_ref[rows, q * 128:(q + 1) * 128] = lo.astype(BF16)
                    xb_ref[rows, HALF + q * 128:HALF + (q + 1) * 128] = hi.astype(BF16)

            @pl.when(k >= nb)
            def _(rows=rows):
                xb_ref[rows, :] = jnp.zeros((SLOT_BLOCK, D_MODEL), BF16)

    nxt = jnp.minimum(s + 1, MAX_TILES - 1)

    @pl.when((j >= 1) & (s + 1 < MAX_TILES) & (j - 1 < nb_ref[nxt]))
    def _():
        request_block(next_ref, j - 1)

    def run(n_rows):
        if n_rows == 0:
            o_ref[...] = jnp.zeros((TILE_ROWS, UP_TF), BF16)
            return
        wb_ref[...] = w_ref[...].astype(BF16)
        perm = _deinterleave_matrix()
        gu = jnp.dot(xb_ref[0:n_rows, :], wb_ref[...], preferred_element_type=F32) + b_ref[...]
        for c in range(2 * UP_TF // 256):
            gl = jnp.dot(gu[:, c * 256:(c + 1) * 256].astype(BF16), perm, preferred_element_type=F32)
            glu = jnp.minimum(gl[:, :128], SWIGLU_LIMIT)
            lin = jnp.clip(gl[:, 128:], -SWIGLU_LIMIT, SWIGLU_LIMIT)
            act = glu * jax.nn.sigmoid(SWIGLU_ALPHA * glu) * (lin + 1.0)
            o_ref[0:n_rows, c * 128:(c + 1) * 128] = act.astype(BF16)
        if n_rows < TILE_ROWS:
            o_ref[n_rows:, :] = jnp.zeros((TILE_ROWS - n_rows, UP_TF), BF16)

    _run_row_variant(nb, run)


def _run_row_variant(nb, run):
    lo = 0
    for blocks in TILE_ROW_VARIANTS:
        @pl.when((nb > lo) & (nb <= blocks))
        def _(blocks=blocks):
            run(blocks * SLOT_BLOCK)
        lo = blocks

    @pl.when(nb == 0)
    def _():
        run(0)


def _expert_up(hp, tile_tok, w_up, b_up, tile_e, tile_nb):
    nj = EXPERT_FF // UP_TF
    assert BLOCKS_PER_TILE <= nj - 1

    def w_index(s, j, e, nb):
        return (e[s], 0, jnp.where(nb[s] > 0, j, nj - 1))

    grid_spec = pltpu.PrefetchScalarGridSpec(
        num_scalar_prefetch=2,
        grid=(MAX_TILES, nj),
        in_specs=[pl.BlockSpec((1, 1, TILE_ROWS), lambda s, j, e, nb: (0, 0, 0), memory_space=pltpu.SMEM),
                  pl.BlockSpec((1, 1, TILE_ROWS), lambda s, j, e, nb: (jnp.minimum(s + 1, MAX_TILES - 1), 0, 0),
                               memory_space=pltpu.SMEM),
                  pl.BlockSpec(memory_space=pl.ANY),
                  pl.BlockSpec((None, D_MODEL, 2 * UP_TF), w_index),
                  pl.BlockSpec((None, 1, 2 * UP_TF), w_index)],
        out_specs=pl.BlockSpec((TILE_ROWS, UP_TF), lambda s, j, e, nb: (s, j)),
        scratch_shapes=[pltpu.VMEM((D_MODEL, 2 * UP_TF), BF16),
                        pltpu.VMEM((TILE_ROWS, D_MODEL), BF16),
                        pltpu.VMEM((TILE_ROWS * ROW_TILE, 128), U32),
                        pltpu.SemaphoreType.DMA(())],
    )
    return pl.pallas_call(
        _up_kernel,
        grid_spec=grid_spec,
        out_shape=jax.ShapeDtypeStruct((MAX_TILES * TILE_ROWS, EXPERT_FF), BF16),
        compiler_params=_params(("arbitrary", "arbitrary")),
        name="expert_up",
    )(tile_e, tile_nb, tile_tok, tile_tok, hp, w_up, b_up.reshape(N_EXPERTS, 1, 2 * EXPERT_FF))


DOWN_TN = TOKEN_TILE * 128


def _down_kernel(e_ref, src_ref, nb_ref, a_ref, wl_ref, wh_ref, bl_ref, bh_ref, o_ref, wlb_ref, whb_ref):
    nb = nb_ref[pl.program_id(0)]

    def run(n_rows):
        if n_rows == 0:
            o_ref[...] = jnp.zeros((TILE_ROWS * TOKEN_TILE, 128), U32)
            return
        wlb_ref[...] = wl_ref[...].astype(BF16)
        whb_ref[...] = wh_ref[...].astype(BF16)
        a = a_ref[0:n_rows, :]
        y_lo = jnp.dot(a, wlb_ref[...], preferred_element_type=F32) + bl_ref[...]
        y_hi = jnp.dot(a, whb_ref[...], preferred_element_type=F32) + bh_ref[...]
        packed = _pack_halves(jnp.concatenate([y_lo, y_hi], axis=1))
        for c in range(TOKEN_TILE):
            o_ref[pl.ds(c, n_rows, stride=TOKEN_TILE), :] = packed[:, c * 128:(c + 1) * 128]
        if n_rows < TILE_ROWS:
            o_ref[n_rows * TOKEN_TILE:, :] = jnp.zeros(((TILE_ROWS - n_rows) * TOKEN_TILE, 128), U32)

    _run_row_variant(nb, run)


def _expert_down(act, w_down, b_down, tile_e, tile_src, tile_nb):
    nj = HALF // DOWN_TN

    def w_index(hi):
        return lambda s, j, e, src, nb: (e[s], 0, hi * nj + jnp.where(nb[s] > 0, j, nj - 1))

    grid_spec = pltpu.PrefetchScalarGridSpec(
        num_scalar_prefetch=3,
        grid=(MAX_TILES, nj),
        in_specs=[pl.BlockSpec((TILE_ROWS, EXPERT_FF), lambda s, j, e, src, nb: (src[s], 0)),
                  pl.BlockSpec((None, EXPERT_FF, DOWN_TN), w_index(0)),
                  pl.BlockSpec((None, EXPERT_FF, DOWN_TN), w_index(1)),
                  pl.BlockSpec((None, 1, DOWN_TN), w_index(0)),
                  pl.BlockSpec((None, 1, DOWN_TN), w_index(1))],
        out_specs=pl.BlockSpec((None, TILE_ROWS * TOKEN_TILE, 128), lambda s, j, e, src, nb: (j, s, 0)),
        scratch_shapes=[pltpu.VMEM((EXPERT_FF, DOWN_TN), BF16), pltpu.VMEM((EXPERT_FF, DOWN_TN), BF16)],
    )
    b3 = b_down.reshape(N_EXPERTS, 1, D_MODEL)
    return pl.pallas_call(
        _down_kernel,
        grid_spec=grid_spec,
        out_shape=jax.ShapeDtypeStruct((2, MAX_TILES * TILE_ROWS * TOKEN_TILE, 128), U32),
        compiler_params=_params(("arbitrary", "arbitrary")),
        name="expert_down",
    )(tile_e, tile_src, tile_nb, act, w_down, w_down, b3, b3)


COMBINE_TM = 64


def _combine_kernel(dest_ref, next_ref, y_hbm, x_ref, wt_ref, g_ref, o_ref, buf_ref, sems):
    i = pl.program_id(0)
    cur = i % 2

    def tile_copy(buf, t, k, half, slot):
        src = y_hbm.at[half, pl.ds(pl.multiple_of(slot * TOKEN_TILE, TOKEN_TILE), TOKEN_TILE), :]
        dst = buf_ref.at[buf, k, half, pl.ds(pl.multiple_of(t * TOKEN_TILE, TOKEN_TILE), TOKEN_TILE), :]
        return pltpu.make_async_copy(src, dst, sems.at[buf])

    def issue_all(table_ref, buf):
        def issue(t, carry):
            for k in range(TOP_K):
                slot = table_ref[0, 0, t * TOP_K + k]
                tile_copy(buf, t, k, 0, slot).start(priority=0)
                tile_copy(buf, t, k, 1, slot).start(priority=1)
            return carry
        lax.fori_loop(0, COMBINE_TM, issue, 0, unroll=4)

    def drain(t, carry):
        for k in range(TOP_K):
            tile_copy(cur, t, k, 0, 0).wait()
            tile_copy(cur, t, k, 1, 0).wait()
        return carry

    @pl.when(i == 0)
    def _():
        issue_all(dest_ref, 0)

    @pl.when(i + 1 < pl.num_programs(0))
    def _():
        issue_all(next_ref, 1 - cur)

    lax.fori_loop(0, COMBINE_TM, drain, 0, unroll=4)

    weights = [wt_ref[:, k:k + 1] for k in range(TOP_K)]
    for half in range(2):
        for c in range(TOKEN_TILE):
            col = (half * TOKEN_TILE + c) * 128
            acc_lo = jnp.zeros((COMBINE_TM, 128), F32)
            acc_hi = jnp.zeros((COMBINE_TM, 128), F32)
            for k in range(TOP_K):
                lo, hi = _unpack_halves(_load_token_tile_column(buf_ref, (cur, k), 0, COMBINE_TM, half, c))
                acc_lo = acc_lo + weights[k] * lo
                acc_hi = acc_hi + weights[k] * hi
            o_ref[:, col:col + 128] = x_ref[:, col:col + 128] + g_ref[:, col:col + 128] * acc_lo
            hcol = HALF + col
            o_ref[:, hcol:hcol + 128] = x_ref[:, hcol:hcol + 128] + g_ref[:, hcol:hcol + 128] * acc_hi


def _combine(y_slots, dest, x1, top_w, mod):
    nt = SEQ // COMBINE_TM
    return pl.pallas_call(
        _combine_kernel,
        grid=(nt,),
        in_specs=[pl.BlockSpec((1, 1, COMBINE_TM * TOP_K), lambda i: (i, 0, 0), memory_space=pltpu.SMEM),
                  pl.BlockSpec((1, 1, COMBINE_TM * TOP_K), lambda i: (jnp.minimum(i + 1, nt - 1), 0, 0),
                               memory_space=pltpu.SMEM),
                  pl.BlockSpec(memory_space=pl.ANY),
                  pl.BlockSpec((COMBINE_TM, D_MODEL), lambda i: (i, 0)),
                  pl.BlockSpec((COMBINE_TM, TOP_K), lambda i: (i, 0)),
                  pl.BlockSpec((1, D_MODEL), lambda i: (0, 5))],
        out_specs=pl.BlockSpec((COMBINE_TM, D_MODEL), lambda i: (i, 0)),
        out_shape=jax.ShapeDtypeStruct((SEQ, D_MODEL), F32),
        scratch_shapes=[pltpu.VMEM((2, TOP_K, 2, COMBINE_TM * TOKEN_TILE, 128), U32),
                        pltpu.SemaphoreType.DMA((2,))],
        compiler_params=_params(("arbitrary",)),
        name="moe_combine",
    )(dest.reshape(nt, 1, COMBINE_TM * TOP_K), dest.reshape(nt, 1, COMBINE_TM * TOP_K), y_slots, x1, top_w, mod)


def _layer(x, c, norm1_g, norm2_g, w_ada, b_ada, w_in, q_norm_g, k_norm_g, rel_bias, conv_w, conv_b,
           conv_norm_g, w_attn_out, w_conv_out, w_o, w_router, b_router, w_up, b_up, w_down, b_down):
    mod = _ada(c, w_ada, b_ada)
    h = _norm1(x, norm1_g, mod)

    outs, lses = [], []
    for group, (_, dil) in enumerate(ATTN_GROUPS):
        qkv = _qkv_proj(h, w_in, q_norm_g, k_norm_g, group, dil)
        o, lse = _band_attention(qkv, _bias_tiles(rel_bias, group, dil), dil)
        outs.append(o)
        lses.append(lse)
    attn = _merge_groups(outs, lses)

    u = _conv_module(_conv_glu_proj(h, w_in), conv_w, conv_b, conv_norm_g)
    gates = _gate_proj(h, w_in)
    merged = _branch_merge(attn, u, w_attn_out, w_conv_out, gates)
    x1 = _wo_residual(merged, w_o, x, mod)

    hp, top_idx, top_w = _norm2_router(x1, norm2_g, mod, w_router, b_router)
    dest, tile_tok, tile_e, tile_src, tile_nb = _routing_tables(top_idx)
    act = _expert_up(hp, tile_tok, w_up, b_up, tile_e, tile_nb)
    y_slots = _expert_down(act, w_down, b_down, tile_e, tile_src, tile_nb)
    return _combine(y_slots, dest, x1, top_w, mod)


def kernel(x, c, norm1_g, norm2_g, w_ada, b_ada, w_in, q_norm_g, k_norm_g, rel_bias, conv_w, conv_b,
           conv_norm_g, w_attn_out, w_conv_out, w_o, w_router, b_router, w_up, b_up, w_down, b_down):
    batch = x.shape[0]
    xs = x.reshape(batch * SEQ, D_MODEL)
    for layer in range(w_in.shape[0]):
        xs = _layer(xs, c, norm1_g[layer], norm2_g[layer], w_ada[layer], b_ada[layer], w_in[layer],
                    q_norm_g[layer], k_norm_g[layer], rel_bias, conv_w[layer], conv_b[layer],
                    conv_norm_g[layer], w_attn_out[layer], w_conv_out[layer], w_o[layer],
                    w_router[layer], b_router[layer], w_up[layer], b_up[layer], w_down[layer], b_down[layer])
    return xs.reshape(x.shape)
```

```python
import functools
import math

import numpy as np
import jax
import jax.numpy as jnp
from jax import lax
from jax.experimental import pallas as pl
from jax.experimental.pallas import tpu as pltpu

D_MODEL = 4096
SEQ = 8192
HEAD_DIM = 128
ATTN_GROUPS = ((128, 1), (512, 4), (2048, 16))
N_GROUPS = len(ATTN_GROUPS)
HEADS_PER_GROUP = 8
GROUP_WIDTH = HEADS_PER_GROUP * HEAD_DIM
ATTN_WIDTH = N_GROUPS * GROUP_WIDTH
CONV_WIDTH = D_MODEL // 2
CONV_KERNEL = 31
CONV_OFF = 3 * ATTN_WIDTH
GATE_OFF = CONV_OFF + 2 * CONV_WIDTH
IN_WIDTH = GATE_OFF + 2 * D_MODEL
REL_BUCKETS = 32
REL_MAX_DISTANCE = 1024
N_EXPERTS = 32
TOP_K = 4
EXPERT_FF = 1536
SWIGLU_ALPHA = 1.702
SWIGLU_LIMIT = 7.0
NORM_EPS = 1e-6
NEG_INF = -1e30
N_SIDE = 64

V7X_VMEM_LIMIT = 56 * 1024 * 1024

SLOT_BLOCK = 256
BLOCKS_PER_TILE = 5
N_ASSIGN = SEQ * TOP_K
TILE_ROWS = SLOT_BLOCK * BLOCKS_PER_TILE
TILE_ROW_VARIANTS = (4, 5)
N_SLOT_BLOCKS = (N_ASSIGN + N_EXPERTS * (SLOT_BLOCK - 1)) // SLOT_BLOCK + 1
MAX_TILES = (N_SLOT_BLOCKS + N_EXPERTS * (BLOCKS_PER_TILE - 1)) // BLOCKS_PER_TILE
HALF = D_MODEL // 2

F32 = jnp.float32
BF16 = jnp.bfloat16
U32 = jnp.uint32


def _params(sem, vmem=V7X_VMEM_LIMIT):
    return pltpu.CompilerParams(dimension_semantics=sem, vmem_limit_bytes=vmem)


def _pack_halves(v):
    n = v.shape[-1] // 2
    lo = pltpu.bitcast(v[:, :n].astype(BF16).astype(F32), U32) >> 16
    hi = pltpu.bitcast(v[:, n:].astype(BF16).astype(F32), U32) & jnp.uint32(0xFFFF0000)
    return lo | hi


def _unpack_halves(p):
    lo = pltpu.bitcast(p << 16, F32)
    hi = pltpu.bitcast(p & jnp.uint32(0xFFFF0000), F32)
    return lo, hi


TOKEN_TILE = 8


def _store_token_tiles(ref, row0, rows, packed):
    for half in range(2):
        for c in range(TOKEN_TILE):
            col = (half * TOKEN_TILE + c) * 128
            ref[half, pl.ds(row0 * TOKEN_TILE + c, rows, stride=TOKEN_TILE), :] = packed[:, col:col + 128]


def _load_token_tile_column(ref, lead, row0, rows, half, c):
    return ref[(*lead, half, pl.ds(row0 * TOKEN_TILE + c, rows, stride=TOKEN_TILE), slice(None))]


def _ada_kernel(c_ref, w_ref, b_ref, o_ref, cs_ref):
    @pl.when(pl.program_id(0) == 0)
    def _():
        c = c_ref[...]
        cs_ref[...] = c * jax.nn.sigmoid(c)

    tn = o_ref.shape[-1]
    rows_per_step = 64

    def body(r, acc):
        rows = pl.ds(pl.multiple_of(r * rows_per_step, rows_per_step), rows_per_step)
        cs = cs_ref[rows, :]
        prod = w_ref[rows, :] * jnp.concatenate([cs] * (tn // 128), axis=1)
        return acc + prod.reshape(rows_per_step // 8, 8, tn).sum(axis=0)

    acc = lax.fori_loop(0, D_MODEL // rows_per_step, body, jnp.zeros((8, tn), F32))
    o_ref[...] = jnp.sum(acc, axis=0, keepdims=True) + b_ref[...]


def _ada(c, w_ada, b_ada):
    tn = 512
    n = w_ada.shape[-1]
    c_lanes = jnp.broadcast_to(c.reshape(D_MODEL, 1), (D_MODEL, 128))
    return pl.pallas_call(
        _ada_kernel,
        grid=(n // tn,),
        in_specs=[pl.BlockSpec((D_MODEL, 128), lambda j: (0, 0)),
                  pl.BlockSpec((D_MODEL, tn), lambda j: (0, j)),
                  pl.BlockSpec((1, tn), lambda j: (0, j))],
        out_specs=pl.BlockSpec((1, tn), lambda j: (0, j)),
        out_shape=jax.ShapeDtypeStruct((1, n), F32),
        scratch_shapes=[pltpu.VMEM((D_MODEL, 128), F32)],
        compiler_params=_params(("arbitrary",)),
        name="ada_mod",
    )(c_lanes, w_ada, b_ada.reshape(1, n))


def _modulated_norm(x, g, scale, shift):
    ms = jnp.mean(x * x, axis=-1, keepdims=True)
    y = x * lax.rsqrt(ms + NORM_EPS) * g
    return y * (1.0 + scale) + shift


def _norm1_kernel(x_ref, g_ref, sc_ref, sh_ref, o_ref):
    o_ref[...] = _modulated_norm(x_ref[...], g_ref[...], sc_ref[...], sh_ref[...]).astype(o_ref.dtype)


def _norm1(x, g, mod):
    tm = 256
    vec = lambda k: pl.BlockSpec((1, D_MODEL), lambda i, k=k: (0, k))
    return pl.pallas_call(
        _norm1_kernel,
        grid=(SEQ // tm,),
        in_specs=[pl.BlockSpec((tm, D_MODEL), lambda i: (i, 0)), vec(0), vec(1), vec(0)],
        out_specs=pl.BlockSpec((tm, D_MODEL), lambda i: (i, 0)),
        out_shape=jax.ShapeDtypeStruct((SEQ, D_MODEL), BF16),
        compiler_params=_params(("arbitrary",)),
        name="norm1",
    )(x, g.reshape(1, D_MODEL), mod, mod)


def _qkv_kernel(h_ref, w_ref, gq_ref, gk_ref, o_ref, wb_ref, acc_ref, *, dil):
    which = pl.program_id(0)
    tm = h_ref.shape[0]

    @pl.when(pl.program_id(1) == 0)
    def _():
        wb_ref[...] = w_ref[...].astype(BF16)

    acc = jnp.dot(h_ref[...], wb_ref[...], preferred_element_type=F32)

    is_v = which == 2
    gain = jnp.where(is_v, 1.0, jnp.where(which == 0, gq_ref[...] * HEAD_DIM ** -0.5, gk_ref[...]))
    for h in range(HEADS_PER_GROUP):
        sl = slice(h * HEAD_DIM, (h + 1) * HEAD_DIM)
        blk = acc[:, sl]
        ms = jnp.mean(blk * blk, axis=-1, keepdims=True)
        vals = blk * jnp.where(is_v, 1.0, lax.rsqrt(ms + NORM_EPS)) * gain
        if dil == 1:
            o_ref[0, :, sl] = vals.astype(BF16)
        else:
            acc_ref[h] = vals
            for r in range(dil):
                o_ref[r, :, sl] = acc_ref[h, pl.ds(r, tm // dil, stride=dil), :].astype(BF16)


def _qkv_proj(h, w_in, gq, gk, group, dil):
    tm = 512
    seg = SEQ // dil
    return pl.pallas_call(
        functools.partial(_qkv_kernel, dil=dil),
        grid=(3, SEQ // tm),
        in_specs=[pl.BlockSpec((tm, D_MODEL), lambda w, i: (i, 0)),
                  pl.BlockSpec((D_MODEL, GROUP_WIDTH), lambda w, i: (0, w * N_GROUPS + group),
                               pipeline_mode=pl.Buffered(1)),
                  pl.BlockSpec((1, HEAD_DIM), lambda w, i: (0, 0)),
                  pl.BlockSpec((1, HEAD_DIM), lambda w, i: (0, 0))],
        out_specs=pl.BlockSpec((None, dil, tm // dil, GROUP_WIDTH), lambda w, i: (w, 0, i, 0)),
        out_shape=jax.ShapeDtypeStruct((3, dil, seg, GROUP_WIDTH), BF16),
        scratch_shapes=[pltpu.VMEM((D_MODEL, GROUP_WIDTH), BF16), pltpu.VMEM((HEADS_PER_GROUP, tm, HEAD_DIM), F32)],
        compiler_params=_params(("arbitrary", "arbitrary")),
        name=f"qkv_proj_g{group}",
    )(h, w_in, gq.reshape(1, HEAD_DIM), gk.reshape(1, HEAD_DIM))


def _glu_kernel(h_ref, wa_ref, wg_ref, o_ref, wab_ref, wgb_ref):
    @pl.when(pl.program_id(1) == 0)
    def _():
        wab_ref[...] = wa_ref[...].astype(BF16)
        wgb_ref[...] = wg_ref[...].astype(BF16)

    h = h_ref[...]
    a = jnp.dot(h, wab_ref[...], preferred_element_type=F32)
    g = jnp.dot(h, wgb_ref[...], preferred_element_type=F32)
    o_ref[...] = (a * jax.nn.sigmoid(g)).astype(o_ref.dtype)


def _conv_glu_proj(h, w_in):
    tm, tn = 512, 512
    a0 = CONV_OFF // tn
    g0 = (CONV_OFF + CONV_WIDTH) // tn
    return pl.pallas_call(
        _glu_kernel,
        grid=(CONV_WIDTH // tn, SEQ // tm),
        in_specs=[pl.BlockSpec((tm, D_MODEL), lambda j, i: (i, 0)),
                  pl.BlockSpec((D_MODEL, tn), lambda j, i: (0, a0 + j)),
                  pl.BlockSpec((D_MODEL, tn), lambda j, i: (0, g0 + j))],
        out_specs=pl.BlockSpec((tm, tn), lambda j, i: (i, j)),
        out_shape=jax.ShapeDtypeStruct((SEQ, CONV_WIDTH), BF16),
        scratch_shapes=[pltpu.VMEM((D_MODEL, tn), BF16), pltpu.VMEM((D_MODEL, tn), BF16)],
        compiler_params=_params(("arbitrary", "arbitrary")),
        name="conv_glu_proj",
    )(h, w_in, w_in)


def _gate_kernel(h_ref, w_ref, o_ref, wb_ref):
    @pl.when(pl.program_id(1) == 0)
    def _():
        wb_ref[...] = w_ref[...].astype(BF16)

    acc = jnp.dot(h_ref[...], wb_ref[...], preferred_element_type=F32)
    o_ref[...] = jax.nn.sigmoid(acc).astype(o_ref.dtype)


def _gate_proj(h, w_in):
    tm, tn = 1024, 512
    c0 = GATE_OFF // tn
    return pl.pallas_call(
        _gate_kernel,
        grid=(2 * D_MODEL // tn, SEQ // tm),
        in_specs=[pl.BlockSpec((tm, D_MODEL), lambda j, i: (i, 0)),
                  pl.BlockSpec((D_MODEL, tn), lambda j, i: (0, c0 + j))],
        out_specs=pl.BlockSpec((tm, tn), lambda j, i: (i, j)),
        out_shape=jax.ShapeDtypeStruct((SEQ, 2 * D_MODEL), BF16),
        scratch_shapes=[pltpu.VMEM((D_MODEL, tn), BF16)],
        compiler_params=_params(("arbitrary", "arbitrary")),
        name="gate_proj",
    )(h, w_in)


ATTN_TQ = 128
ATTN_TK = ATTN_TQ + 2 * N_SIDE
LSE_LANES = 128


def _bucket_tile(dil):
    a = np.arange(ATTN_TQ)[:, None]
    j = np.arange(ATTN_TK)[None, :]
    steps = j - N_SIDE - a
    rel = steps * dil
    nb = REL_BUCKETS // 2
    max_exact = nb // 2
    n = np.abs(rel)
    side = np.where(rel > 0, nb, 0)
    nf = np.maximum(n, 1).astype(np.float32)
    large = max_exact + (np.log(nf / np.float32(max_exact)) / np.float32(math.log(REL_MAX_DISTANCE / max_exact))
                         * np.float32(nb - max_exact)).astype(np.int32)
    large = np.minimum(large, nb - 1)
    bucket = side + np.where(n < max_exact, n, large)
    return np.where(np.abs(steps) <= N_SIDE, bucket, -1).astype(np.int32)


def _bias_kernel(tbl_ref, idx_ref, o_ref, *, group):
    h = pl.program_id(0)
    idx = idx_ref[...]
    acc = jnp.full(idx.shape, NEG_INF, F32)
    for b in range(REL_BUCKETS):
        acc = jnp.where(idx == b, tbl_ref[b, group * HEADS_PER_GROUP + h], acc)
    o_ref[...] = acc


def _bias_tiles(rel_bias, group, dil):
    return pl.pallas_call(
        functools.partial(_bias_kernel, group=group),
        grid=(HEADS_PER_GROUP,),
        in_specs=[pl.BlockSpec(memory_space=pltpu.SMEM),
                  pl.BlockSpec((ATTN_TQ, ATTN_TK), lambda h: (0, 0))],
        out_specs=pl.BlockSpec((None, ATTN_TQ, ATTN_TK), lambda h: (h, 0, 0)),
        out_shape=jax.ShapeDtypeStruct((HEADS_PER_GROUP, ATTN_TQ, ATTN_TK), F32),
        compiler_params=_params(("arbitrary",)),
        name=f"attn_bias_g{group}",
    )(rel_bias, jnp.asarray(_bucket_tile(dil)))


def _attn_kernel(q_ref, kp_ref, kc_ref, kn_ref, vp_ref, vc_ref, vn_ref, bias_ref, o_ref, lse_ref, *, seg):
    i = pl.program_id(1)
    q = q_ref[...]
    k = jnp.concatenate([kp_ref[...], kc_ref[...], kn_ref[...]], axis=0)
    v = jnp.concatenate([vp_ref[...], vc_ref[...], vn_ref[...]], axis=0)
    kpos = i * ATTN_TQ - N_SIDE + lax.broadcasted_iota(jnp.int32, (1, ATTN_TK), 1)
    valid = (kpos >= 0) & (kpos < seg)
    lane = lax.broadcasted_iota(jnp.int32, (ATTN_TQ, LSE_LANES), 1)
    lse_all = jnp.zeros((ATTN_TQ, LSE_LANES), F32)
    for h in range(HEADS_PER_GROUP):
        sl = slice(h * HEAD_DIM, (h + 1) * HEAD_DIM)
        s = lax.dot_general(q[:, sl], k[:, sl], (((1,), (1,)), ((), ())), preferred_element_type=F32)
        s = jnp.where(valid, s + bias_ref[h], NEG_INF)
        m = jnp.max(s, axis=-1, keepdims=True)
        p = jnp.exp(s - m)
        l = jnp.sum(p, axis=-1, keepdims=True)
        o = jnp.dot(p.astype(BF16), v[:, sl], preferred_element_type=F32)
        o_ref[:, sl] = (o / l).astype(o_ref.dtype)
        lse_all = jnp.where(lane == h, m + jnp.log(l), lse_all)
    lse_ref[...] = lse_all


def _band_attention(qkv, bias, dil):
    seg = SEQ // dil
    sub = ATTN_TQ // N_SIDE
    last = seg // N_SIDE - 1

    def cur(which):
        return pl.BlockSpec((None, None, ATTN_TQ, GROUP_WIDTH), lambda r, i, w=which: (w, r, i, 0))

    def prev(which):
        return pl.BlockSpec((None, None, N_SIDE, GROUP_WIDTH),
                            lambda r, i, w=which: (w, r, jnp.maximum(i * sub - 1, 0), 0))

    def nxt(which):
        return pl.BlockSpec((None, None, N_SIDE, GROUP_WIDTH),
                            lambda r, i, w=which: (w, r, jnp.minimum((i + 1) * sub, last), 0))

    return pl.pallas_call(
        functools.partial(_attn_kernel, seg=seg),
        grid=(dil, seg // ATTN_TQ),
        in_specs=[cur(0), prev(1), cur(1), nxt(1), prev(2), cur(2), nxt(2),
                  pl.BlockSpec((HEADS_PER_GROUP, ATTN_TQ, ATTN_TK), lambda r, i: (0, 0, 0))],
        out_specs=[pl.BlockSpec((None, ATTN_TQ, GROUP_WIDTH), lambda r, i: (r, i, 0)),
                   pl.BlockSpec((None, ATTN_TQ, LSE_LANES), lambda r, i: (r, i, 0))],
        out_shape=[jax.ShapeDtypeStruct((dil, seg, GROUP_WIDTH), BF16),
                   jax.ShapeDtypeStruct((dil, seg, LSE_LANES), F32)],
        compiler_params=_params(("arbitrary", "arbitrary")),
        name=f"band_attn_d{dil}",
    )(qkv, qkv, qkv, qkv, qkv, qkv, qkv, bias)


def _merge_kernel(o0_ref, o1_ref, o2_ref, l0_ref, l1_ref, l2_ref, out_ref, buf_ref, lbuf_ref):
    tm = out_ref.shape[0]
    for gi, (o_ref, l_ref) in enumerate(((o1_ref, l1_ref), (o2_ref, l2_ref))):
        dil = o_ref.shape[0]
        for r in range(dil):
            rows = pl.ds(r, tm // dil, stride=dil)
            lbuf_ref[gi, rows, :] = l_ref[r]
            for h in range(HEADS_PER_GROUP):
                buf_ref[gi * HEADS_PER_GROUP + h, rows, :] = (
                    o_ref[r, :, h * HEAD_DIM:(h + 1) * HEAD_DIM].astype(F32))
    lse0, lse1, lse2 = l0_ref[0], lbuf_ref[0], lbuf_ref[1]
    top = jnp.maximum(jnp.maximum(lse0, lse1), lse2)
    w0, w1, w2 = jnp.exp(lse0 - top), jnp.exp(lse1 - top), jnp.exp(lse2 - top)
    den = w0 + w1 + w2
    for h in range(HEADS_PER_GROUP):
        sl = slice(h * HEAD_DIM, (h + 1) * HEAD_DIM)
        col = slice(h, h + 1)
        num = (w0[:, col] * o0_ref[0, :, sl].astype(F32) + w1[:, col] * buf_ref[h]
               + w2[:, col] * buf_ref[HEADS_PER_GROUP + h])
        out_ref[:, sl] = (num / den[:, col]).astype(out_ref.dtype)


def _merge_groups(outs, lses):
    tm = 256
    d1, d2 = ATTN_GROUPS[1][1], ATTN_GROUPS[2][1]

    def spec(dil, width):
        return pl.BlockSpec((dil, tm // dil, width), lambda i: (0, i, 0))

    return pl.pallas_call(
        _merge_kernel,
        grid=(SEQ // tm,),
        in_specs=[spec(1, GROUP_WIDTH), spec(d1, GROUP_WIDTH), spec(d2, GROUP_WIDTH),
                  spec(1, LSE_LANES), spec(d1, LSE_LANES), spec(d2, LSE_LANES)],
        out_specs=pl.BlockSpec((tm, GROUP_WIDTH), lambda i: (i, 0)),
        out_shape=jax.ShapeDtypeStruct((SEQ, GROUP_WIDTH), BF16),
        scratch_shapes=[pltpu.VMEM((2 * HEADS_PER_GROUP, tm, HEAD_DIM), F32), pltpu.VMEM((2, tm, LSE_LANES), F32)],
        compiler_params=_params(("arbitrary",)),
        name="attn_merge",
    )(*outs, *lses)


CONV_HALO = 16


def _conv_kernel(up_ref, uc_ref, un_ref, w_ref, b_ref, g_ref, o_ref, buf_ref, y_ref):
    i = pl.program_id(0)
    tm = uc_ref.shape[0]
    first = i == 0
    last = i == pl.num_programs(0) - 1
    buf_ref[0:CONV_HALO, :] = jnp.where(first, 0.0, up_ref[...].astype(F32))
    buf_ref[CONV_HALO:CONV_HALO + tm, :] = uc_ref[...].astype(F32)
    buf_ref[CONV_HALO + tm:, :] = jnp.where(last, 0.0, un_ref[...].astype(F32))
    pad = CONV_KERNEL // 2
    lanes = 256
    for c0 in range(0, CONV_WIDTH, lanes):
        cs = slice(c0, c0 + lanes)
        part = jnp.broadcast_to(b_ref[:, cs], (tm, lanes))
        first = CONV_HALO - pad
        for s in range(8):
            group = None
            for tap in range(CONV_KERNEL):
                off = first + tap
                if off % 8 != s:
                    continue
                term = w_ref[tap:tap + 1, cs] * buf_ref[pl.ds(off - s, tm + 8), cs]
                group = term if group is None else group + term
            part = part + group[s:s + tm, :]
        y_ref[:, cs] = part
    acc = y_ref[...]
    ms = jnp.mean(acc * acc, axis=-1, keepdims=True)
    y = acc * lax.rsqrt(ms + NORM_EPS) * g_ref[...]
    o_ref[...] = (y * jax.nn.sigmoid(y)).astype(o_ref.dtype)


def _conv_module(u, conv_w, conv_b, conv_norm_g):
    tm = 128
    sub = tm // CONV_HALO
    last = SEQ // CONV_HALO - 1
    return pl.pallas_call(
        _conv_kernel,
        grid=(SEQ // tm,),
        in_specs=[pl.BlockSpec((CONV_HALO, CONV_WIDTH), lambda i: (jnp.maximum(i * sub - 1, 0), 0)),
                  pl.BlockSpec((tm, CONV_WIDTH), lambda i: (i, 0)),
                  pl.BlockSpec((CONV_HALO, CONV_WIDTH), lambda i: (jnp.minimum((i + 1) * sub, last), 0)),
                  pl.BlockSpec((CONV_KERNEL, CONV_WIDTH), lambda i: (0, 0)),
                  pl.BlockSpec((1, CONV_WIDTH), lambda i: (0, 0)),
                  pl.BlockSpec((1, CONV_WIDTH), lambda i: (0, 0))],
        out_specs=pl.BlockSpec((tm, CONV_WIDTH), lambda i: (i, 0)),
        out_shape=jax.ShapeDtypeStruct((SEQ, CONV_WIDTH), BF16),
        scratch_shapes=[pltpu.VMEM((tm + 2 * CONV_HALO, CONV_WIDTH), F32), pltpu.VMEM((tm, CONV_WIDTH), F32)],
        compiler_params=_params(("arbitrary",)),
        name="conv_module",
    )(u, u, u, conv_w, conv_b.reshape(1, CONV_WIDTH), conv_norm_g.reshape(1, CONV_WIDTH))


def _branch_kernel(a_ref, u_ref, wa_ref, wc_ref, ga_ref, gc_ref, o_ref, wab_ref, wcb_ref):
    @pl.when(pl.program_id(1) == 0)
    def _():
        wab_ref[...] = wa_ref[...].astype(BF16)
        wcb_ref[...] = wc_ref[...].astype(BF16)

    attn = jnp.dot(a_ref[...], wab_ref[...], preferred_element_type=F32)
    conv = jnp.dot(u_ref[...], wcb_ref[...], preferred_element_type=F32)
    o_ref[...] = (ga_ref[...].astype(F32) * attn + gc_ref[...].astype(F32) * conv).astype(o_ref.dtype)


def _branch_merge(attn, conv, w_attn_out, w_conv_out, gates):
    tm, tn = 1024, 512
    nj = D_MODEL // tn
    return pl.pallas_call(
        _branch_kernel,
        grid=(nj, SEQ // tm),
        in_specs=[pl.BlockSpec((tm, GROUP_WIDTH), lambda j, i: (i, 0)),
                  pl.BlockSpec((tm, CONV_WIDTH), lambda j, i: (i, 0)),
                  pl.BlockSpec((GROUP_WIDTH, tn), lambda j, i: (0, j)),
                  pl.BlockSpec((CONV_WIDTH, tn), lambda j, i: (0, j)),
                  pl.BlockSpec((tm, tn), lambda j, i: (i, j)),
                  pl.BlockSpec((tm, tn), lambda j, i: (i, nj + j))],
        out_specs=pl.BlockSpec((tm, tn), lambda j, i: (i, j)),
        out_shape=jax.ShapeDtypeStruct((SEQ, D_MODEL), BF16),
        scratch_shapes=[pltpu.VMEM((GROUP_WIDTH, tn), BF16), pltpu.VMEM((CONV_WIDTH, tn), BF16)],
        compiler_params=_params(("arbitrary", "arbitrary")),
        name="branch_merge",
    )(attn, conv, w_attn_out, w_conv_out, gates, gates)


def _wo_kernel(m_ref, w_ref, x_ref, g_ref, o_ref, wb_ref):
    @pl.when(pl.program_id(1) == 0)
    def _():
        wb_ref[...] = w_ref[...].astype(BF16)

    y = jnp.dot(m_ref[...], wb_ref[...], preferred_element_type=F32)
    o_ref[...] = x_ref[...] + g_ref[...] * y


def _wo_residual(merged, w_o, x, mod):
    tm, tn = 1024, 512
    nj = D_MODEL // tn
    return pl.pallas_call(
        _wo_kernel,
        grid=(nj, SEQ // tm),
        in_specs=[pl.BlockSpec((tm, D_MODEL), lambda j, i: (i, 0)),
                  pl.BlockSpec((D_MODEL, tn), lambda j, i: (0, j)),
                  pl.BlockSpec((tm, tn), lambda j, i: (i, j)),
                  pl.BlockSpec((1, tn), lambda j, i: (0, 2 * nj + j))],
        out_specs=pl.BlockSpec((tm, tn), lambda j, i: (i, j)),
        out_shape=jax.ShapeDtypeStruct((SEQ, D_MODEL), F32),
        scratch_shapes=[pltpu.VMEM((D_MODEL, tn), BF16)],
        compiler_params=_params(("arbitrary", "arbitrary")),
        name="wo_residual",
    )(merged, w_o, x, mod)


def _router_kernel(x_ref, g_ref, sc_ref, sh_ref, wr_ref, br_ref, hp_ref, idx_ref, wt_ref):
    tm = x_ref.shape[0]
    h = _modulated_norm(x_ref[...], g_ref[...], sc_ref[...], sh_ref[...])
    hp_ref[...] = _pack_halves(h)

    h_hi = h.astype(BF16)
    h_lo = (h - h_hi.astype(F32)).astype(BF16)
    w = wr_ref[...]
    w_hi = w.astype(BF16)
    w_lo = (w - w_hi.astype(F32)).astype(BF16)
    logits = (jnp.dot(h_hi, w_hi, preferred_element_type=F32) + jnp.dot(h_lo, w_hi, preferred_element_type=F32)
              + jnp.dot(h_hi, w_lo, preferred_element_type=F32) + br_ref[...])

    lane = lax.broadcasted_iota(jnp.int32, (tm, N_EXPERTS), 1)
    work = logits
    vals, idxs = [], []
    for _ in range(TOP_K):
        m = jnp.max(work, axis=-1, keepdims=True)
        idx = jnp.min(jnp.where(work == m, lane, N_EXPERTS), axis=-1, keepdims=True)
        vals.append(m)
        idxs.append(idx)
        work = jnp.where(lane == idx, -jnp.inf, work)
    exps = [jnp.exp(v - vals[0]) for v in vals]
    den = exps[0] + exps[1] + exps[2] + exps[3]
    col = lax.broadcasted_iota(jnp.int32, (tm, TOP_K), 1)
    idx_out = jnp.zeros((tm, TOP_K), jnp.int32)
    wt_out = jnp.zeros((tm, TOP_K), F32)
    for k in range(TOP_K):
        idx_out = jnp.where(col == k, idxs[k], idx_out)
        wt_out = jnp.where(col == k, exps[k] / den, wt_out)
    idx_ref[...] = idx_out
    wt_ref[...] = wt_out


def _norm2_router(x1, g, mod, w_router, b_router):
    tm = 256
    vec = lambda k: pl.BlockSpec((1, D_MODEL), lambda i, k=k: (0, k))
    return pl.pallas_call(
        _router_kernel,
        grid=(SEQ // tm,),
        in_specs=[pl.BlockSpec((tm, D_MODEL), lambda i: (i, 0)), vec(0), vec(4), vec(3),
                  pl.BlockSpec((D_MODEL, N_EXPERTS), lambda i: (0, 0)),
                  pl.BlockSpec((1, N_EXPERTS), lambda i: (0, 0))],
        out_specs=[pl.BlockSpec((tm, HALF), lambda i: (i, 0)),
                   pl.BlockSpec((tm, TOP_K), lambda i: (i, 0)),
                   pl.BlockSpec((tm, TOP_K), lambda i: (i, 0))],
        out_shape=[jax.ShapeDtypeStruct((SEQ, HALF), U32),
                   jax.ShapeDtypeStruct((SEQ, TOP_K), jnp.int32),
                   jax.ShapeDtypeStruct((SEQ, TOP_K), F32)],
        compiler_params=_params(("arbitrary",)),
        name="norm2_router",
    )(x1, g.reshape(1, D_MODEL), mod, mod, w_router, b_router.reshape(1, N_EXPERTS))


def _routing_tables(top_idx):
    i32 = jnp.int32
    e_flat = top_idx.reshape(-1)
    onehot = (e_flat[:, None] == jnp.arange(N_EXPERTS, dtype=i32)[None, :]).astype(i32)
    csum = jnp.cumsum(onehot, axis=0)
    rank = jnp.sum(onehot * csum, axis=1) - 1
    counts = csum[-1]
    nblk = (counts + SLOT_BLOCK - 1) // SLOT_BLOCK
    blk_end = jnp.cumsum(nblk)
    blk_start = blk_end - nblk
    ntile = (nblk + BLOCKS_PER_TILE - 1) // BLOCKS_PER_TILE
    tile_end = jnp.cumsum(ntile)
    tile_start = tile_end - ntile
    n_tiles = tile_end[-1]
    n_blocks = blk_end[-1]

    dest = (jnp.sum(onehot * (tile_start * TILE_ROWS)[None, :], axis=1) + rank).astype(i32)
    packed = jnp.sum(onehot * (blk_start * SLOT_BLOCK)[None, :], axis=1) + rank
    tok = jnp.arange(N_ASSIGN, dtype=i32) // TOP_K
    slot_tok = jnp.zeros((N_SLOT_BLOCKS * SLOT_BLOCK,), i32).at[packed].set(tok, unique_indices=True)

    s_raw = jnp.arange(MAX_TILES, dtype=i32)
    s = jnp.minimum(s_raw, n_tiles - 1)
    tile_e = jnp.minimum(jnp.searchsorted(tile_end, s, side='right'), N_EXPERTS - 1).astype(i32)
    local = s - tile_start[tile_e]
    tile_nb = jnp.clip(nblk[tile_e] - BLOCKS_PER_TILE * local, 0, BLOCKS_PER_TILE)
    tile_nb = jnp.where(s_raw < n_tiles, tile_nb, 0).astype(i32)
    tile_b0 = blk_start[tile_e] + BLOCKS_PER_TILE * local
    tile_src = s.astype(i32)
    rows = tile_b0[:, None] * SLOT_BLOCK + jnp.arange(TILE_ROWS, dtype=i32)[None, :]
    tile_tok = slot_tok[jnp.minimum(rows, N_SLOT_BLOCKS * SLOT_BLOCK - 1)].reshape(MAX_TILES, 1, TILE_ROWS)
    return dest, tile_tok, tile_e, tile_src, tile_nb


UP_TF = 256


def _deinterleave_matrix():
    r = lax.broadcasted_iota(jnp.int32, (256, 256), 0)
    c = lax.broadcasted_iota(jnp.int32, (256, 256), 1)
    src = jnp.where(c < 128, 2 * c, 2 * (c - 128) + 1)
    return jnp.where(r == src, 1.0, 0.0).astype(BF16)


def _up_kernel(e_ref, nb_ref, first_ref, next_ref, h_hbm, w_ref, b_ref, o_ref, wb_ref, xb_ref, g_ref, sem):
    s = pl.program_id(0)
    j = pl.program_id(1)
    nb = nb_ref[s]

    def row_copy(r, tok):
        return pltpu.make_async_copy(h_hbm.at[pl.ds(tok, 1), :], g_ref.at[pl.ds(r, 1), :], sem)

    def request_block(table_ref, k):
        def issue(i, carry):
            for u in range(2):
                r = k * SLOT_BLOCK + 2 * i + u
                row_copy(r, table_ref[0, 0, r]).start(priority=1)
            return carry
        lax.fori_loop(0, SLOT_BLOCK // 2, issue, 0, unroll=4)

    @pl.when((s == 0) & (j == 0))
    def _():
        for k in range(BLOCKS_PER_TILE):
            @pl.when(k < nb)
            def _(k=k):
                request_block(first_ref, k)

    @pl.when(j == 0)
    def _():
        def drain_row(r, carry):
            row_copy(0, 0).wait()
            return carry

        def drain_block(k, carry):
            return lax.fori_loop(0, SLOT_BLOCK, drain_row, carry, unroll=8)

        lax.fori_loop(0, nb, drain_block, 0)

        for k in range(BLOCKS_PER_TILE):
            rows = slice(k * SLOT_BLOCK, (k + 1) * SLOT_BLOCK)

            @pl.when(k < nb)
            def _(k=k, rows=rows):
                lo, hi = _unpack_halves(g_ref[rows, :])
                xb_ref[rows, :HALF] = lo.astype(BF16)
                xb_ref[rows, HALF:] = hi.astype(BF16)

            @pl.when(k >= nb)
            def _(rows=rows):
                xb_ref[rows, :] = jnp.zeros((SLOT_BLOCK, D_MODEL), BF16)

    nxt = jnp.minimum(s + 1, MAX_TILES - 1)

    @pl.when((j >= 1) & (s + 1 < MAX_TILES) & (j - 1 < nb_ref[nxt]))
    def _():
        request_block(next_ref, j - 1)

    def run(n_rows):
        if n_rows == 0:
            o_ref[...] = jnp.zeros((TILE_ROWS, UP_TF), BF16)
            return
        wb_ref[...] = w_ref[...].astype(BF16)
        perm = _deinterleave_matrix()
        gu = jnp.dot(xb_ref[0:n_rows, :], wb_ref[...], preferred_element_type=F32) + b_ref[...]
        for c in range(2 * UP_TF // 256):
            gl = jnp.dot(gu[:, c * 256:(c + 1) * 256].astype(BF16), perm, preferred_element_type=F32)
            glu = jnp.minimum(gl[:, :128], SWIGLU_LIMIT)
            lin = jnp.clip(gl[:, 128:], -SWIGLU_LIMIT, SWIGLU_LIMIT)
            act = glu * jax.nn.sigmoid(SWIGLU_ALPHA * glu) * (lin + 1.0)
            o_ref[0:n_rows, c * 128:(c + 1) * 128] = act.astype(BF16)
        if n_rows < TILE_ROWS:
            o_ref[n_rows:, :] = jnp.zeros((TILE_ROWS - n_rows, UP_TF), BF16)

    _run_row_variant(nb, run)


def _run_row_variant(nb, run):
    lo = 0
    for blocks in TILE_ROW_VARIANTS:
        @pl.when((nb > lo) & (nb <= blocks))
        def _(blocks=blocks):
            run(blocks * SLOT_BLOCK)
        lo = blocks

    @pl.when(nb == 0)
    def _():
        run(0)


def _expert_up(hp, tile_tok, w_up, b_up, tile_e, tile_nb):
    nj = EXPERT_FF // UP_TF
    assert BLOCKS_PER_TILE <= nj - 1

    def w_index(s, j, e, nb):
        return (e[s], 0, jnp.where(nb[s] > 0, j, nj - 1))

    grid_spec = pltpu.PrefetchScalarGridSpec(
        num_scalar_prefetch=2,
        grid=(MAX_TILES, nj),
        in_specs=[pl.BlockSpec((1, 1, TILE_ROWS), lambda s, j, e, nb: (0, 0, 0), memory_space=pltpu.SMEM),
                  pl.BlockSpec((1, 1, TILE_ROWS), lambda s, j, e, nb: (jnp.minimum(s + 1, MAX_TILES - 1), 0, 0),
                               memory_space=pltpu.SMEM),
                  pl.BlockSpec(memory_space=pl.ANY),
                  pl.BlockSpec((None, D_MODEL, 2 * UP_TF), w_index),
                  pl.BlockSpec((None, 1, 2 * UP_TF), w_index)],
        out_specs=pl.BlockSpec((TILE_ROWS, UP_TF), lambda s, j, e, nb: (s, j)),
        scratch_shapes=[pltpu.VMEM((D_MODEL, 2 * UP_TF), BF16),
                        pltpu.VMEM((TILE_ROWS, D_MODEL), BF16),
                        pltpu.VMEM((TILE_ROWS, HALF), U32),
                        pltpu.SemaphoreType.DMA(())],
    )
    return pl.pallas_call(
        _up_kernel,
        grid_spec=grid_spec,
        out_shape=jax.ShapeDtypeStruct((MAX_TILES * TILE_ROWS, EXPERT_FF), BF16),
        compiler_params=_params(("arbitrary", "arbitrary")),
        name="expert_up",
    )(tile_e, tile_nb, tile_tok, tile_tok, hp, w_up, b_up.reshape(N_EXPERTS, 1, 2 * EXPERT_FF))


DOWN_TN = TOKEN_TILE * 128


def _down_kernel(e_ref, src_ref, nb_ref, a_ref, wl_ref, wh_ref, bl_ref, bh_ref, o_ref, wlb_ref, whb_ref):
    nb = nb_ref[pl.program_id(0)]

    def run(n_rows):
        if n_rows == 0:
            o_ref[...] = jnp.zeros((TILE_ROWS * TOKEN_TILE, 128), U32)
            return
        wlb_ref[...] = wl_ref[...].astype(BF16)
        whb_ref[...] = wh_ref[...].astype(BF16)
        a = a_ref[0:n_rows, :]
        y_lo = jnp.dot(a, wlb_ref[...], preferred_element_type=F32) + bl_ref[...]
        y_hi = jnp.dot(a, whb_ref[...], preferred_element_type=F32) + bh_ref[...]
        packed = _pack_halves(jnp.concatenate([y_lo, y_hi], axis=1))
        for c in range(TOKEN_TILE):
            o_ref[pl.ds(c, n_rows, stride=TOKEN_TILE), :] = packed[:, c * 128:(c + 1) * 128]
        if n_rows < TILE_ROWS:
            o_ref[n_rows * TOKEN_TILE:, :] = jnp.zeros(((TILE_ROWS - n_rows) * TOKEN_TILE, 128), U32)

    _run_row_variant(nb, run)


def _expert_down(act, w_down, b_down, tile_e, tile_src, tile_nb):
    nj = HALF // DOWN_TN

    def w_index(hi):
        return lambda s, j, e, src, nb: (e[s], 0, hi * nj + jnp.where(nb[s] > 0, j, nj - 1))

    grid_spec = pltpu.PrefetchScalarGridSpec(
        num_scalar_prefetch=3,
        grid=(MAX_TILES, nj),
        in_specs=[pl.BlockSpec((TILE_ROWS, EXPERT_FF), lambda s, j, e, src, nb: (src[s], 0)),
                  pl.BlockSpec((None, EXPERT_FF, DOWN_TN), w_index(0)),
                  pl.BlockSpec((None, EXPERT_FF, DOWN_TN), w_index(1)),
                  pl.BlockSpec((None, 1, DOWN_TN), w_index(0)),
                  pl.BlockSpec((None, 1, DOWN_TN), w_index(1))],
        out_specs=pl.BlockSpec((None, TILE_ROWS * TOKEN_TILE, 128), lambda s, j, e, src, nb: (j, s, 0)),
        scratch_shapes=[pltpu.VMEM((EXPERT_FF, DOWN_TN), BF16), pltpu.VMEM((EXPERT_FF, DOWN_TN), BF16)],
    )
    b3 = b_down.reshape(N_EXPERTS, 1, D_MODEL)
    return pl.pallas_call(
        _down_kernel,
        grid_spec=grid_spec,
        out_shape=jax.ShapeDtypeStruct((2, MAX_TILES * TILE_ROWS * TOKEN_TILE, 128), U32),
        compiler_params=_params(("arbitrary", "arbitrary")),
        name="expert_down",
    )(tile_e, tile_src, tile_nb, act, w_down, w_down, b3, b3)


COMBINE_TM = 128


def _combine_kernel(dest_ref, next_ref, y_hbm, x_ref, wt_ref, g_ref, o_ref, buf_ref, sems):
    i = pl.program_id(0)
    cur = i % 2

    def tile_copy(buf, t, k, half, slot):
        src = y_hbm.at[half, pl.ds(pl.multiple_of(slot * TOKEN_TILE, TOKEN_TILE), TOKEN_TILE), :]
        dst = buf_ref.at[buf, k, half, pl.ds(pl.multiple_of(t * TOKEN_TILE, TOKEN_TILE), TOKEN_TILE), :]
        return pltpu.make_async_copy(src, dst, sems.at[buf])

    def issue_all(table_ref, buf):
        def issue(t, carry):
            for k in range(TOP_K):
                slot = table_ref[0, 0, t * TOP_K + k]
                tile_copy(buf, t, k, 0, slot).start(priority=0)
                tile_copy(buf, t, k, 1, slot).start(priority=1)
            return carry
        lax.fori_loop(0, COMBINE_TM, issue, 0, unroll=4)

    def drain(t, carry):
        for k in range(TOP_K):
            tile_copy(cur, t, k, 0, 0).wait()
            tile_copy(cur, t, k, 1, 0).wait()
        return carry

    @pl.when(i == 0)
    def _():
        issue_all(dest_ref, 0)

    @pl.when(i + 1 < pl.num_programs(0))
    def _():
        issue_all(next_ref, 1 - cur)

    lax.fori_loop(0, COMBINE_TM, drain, 0, unroll=4)

    weights = [wt_ref[:, k:k + 1] for k in range(TOP_K)]
    for half in range(2):
        for c in range(TOKEN_TILE):
            col = (half * TOKEN_TILE + c) * 128
            acc_lo = jnp.zeros((COMBINE_TM, 128), F32)
            acc_hi = jnp.zeros((COMBINE_TM, 128), F32)
            for k in range(TOP_K):
                lo, hi = _unpack_halves(_load_token_tile_column(buf_ref, (cur, k), 0, COMBINE_TM, half, c))
                acc_lo = acc_lo + weights[k] * lo
                acc_hi = acc_hi + weights[k] * hi
            o_ref[:, col:col + 128] = x_ref[:, col:col + 128] + g_ref[:, col:col + 128] * acc_lo
            hcol = HALF + col
            o_ref[:, hcol:hcol + 128] = x_ref[:, hcol:hcol + 128] + g_ref[:, hcol:hcol + 128] * acc_hi


def _combine(y_slots, dest, x1, top_w, mod):
    nt = SEQ // COMBINE_TM
    return pl.pallas_call(
        _combine_kernel,
        grid=(nt,),
        in_specs=[pl.BlockSpec((1, 1, COMBINE_TM * TOP_K), lambda i: (i, 0, 0), memory_space=pltpu.SMEM),
                  pl.BlockSpec((1, 1, COMBINE_TM * TOP_K), lambda i: (jnp.minimum(i + 1, nt - 1), 0, 0),
                               memory_space=pltpu.SMEM),
                  pl.BlockSpec(memory_space=pl.ANY),
                  pl.BlockSpec((COMBINE_TM, D_MODEL), lambda i: (i, 0)),
                  pl.BlockSpec((COMBINE_TM, TOP_K), lambda i: (i, 0)),
                  pl.BlockSpec((1, D_MODEL), lambda i: (0, 5))],
        out_specs=pl.BlockSpec((COMBINE_TM, D_MODEL), lambda i: (i, 0)),
        out_shape=jax.ShapeDtypeStruct((SEQ, D_MODEL), F32),
        scratch_shapes=[pltpu.VMEM((2, TOP_K, 2, COMBINE_TM * TOKEN_TILE, 128), U32),
                        pltpu.SemaphoreType.DMA((2,))],
        compiler_params=_params(("arbitrary",)),
        name="moe_combine",
    )(dest.reshape(nt, 1, COMBINE_TM * TOP_K), dest.reshape(nt, 1, COMBINE_TM * TOP_K), y_slots, x1, top_w, mod)


def _layer(x, c, norm1_g, norm2_g, w_ada, b_ada, w_in, q_norm_g, k_norm_g, rel_bias, conv_w, conv_b,
           conv_norm_g, w_attn_out, w_conv_out, w_o, w_router, b_router, w_up, b_up, w_down, b_down):
    mod = _ada(c, w_ada, b_ada)
    h = _norm1(x, norm1_g, mod)

    outs, lses = [], []
    for group, (_, dil) in enumerate(ATTN_GROUPS):
        qkv = _qkv_proj(h, w_in, q_norm_g, k_norm_g, group, dil)
        o, lse = _band_attention(qkv, _bias_tiles(rel_bias, group, dil), dil)
        outs.append(o)
        lses.append(lse)
    attn = _merge_groups(outs, lses)

    u = _conv_module(_conv_glu_proj(h, w_in), conv_w, conv_b, conv_norm_g)
    gates = _gate_proj(h, w_in)
    merged = _branch_merge(attn, u, w_attn_out, w_conv_out, gates)
    x1 = _wo_residual(merged, w_o, x, mod)

    hp, top_idx, top_w = _norm2_router(x1, norm2_g, mod, w_router, b_router)
    dest, tile_tok, tile_e, tile_src, tile_nb = _routing_tables(top_idx)
    act = _expert_up(hp, tile_tok, w_up, b_up, tile_e, tile_nb)
    y_slots = _expert_down(act, w_down, b_down, tile_e, tile_src, tile_nb)
    return _combine(y_slots, dest, x1, top_w, mod)


def kernel(x, c, norm1_g, norm2_g, w_ada, b_ada, w_in, q_norm_g, k_norm_g, rel_bias, conv_w, conv_b,
           conv_norm_g, w_attn_out, w_conv_out, w_o, w_router, b_router, w_up, b_up, w_down, b_down):
    batch = x.shape[0]
    xs = x.reshape(batch * SEQ, D_MODEL)
    for layer in range(w_in.shape[0]):
        xs = _layer(xs, c, norm1_g[layer], norm2_g[layer], w_ada[layer], b_ada[layer], w_in[layer],
                    q_norm_g[layer], k_norm_g[layer], rel_bias, conv_w[layer], conv_b[layer],
                    conv_norm_g[layer], w_attn_out[layer], w_conv_out[layer], w_o[layer],
                    w_router[layer], b_router[layer], w_up[layer], b_up[layer], w_down[layer], b_down[layer])
    return xs.reshape(x.shape)
```

```python
import functools
import math

import numpy as np
import jax
import jax.numpy as jnp
from jax import lax
from jax.experimental import pallas as pl
from jax.experimental.pallas import tpu as pltpu

D_MODEL = 4096
SEQ = 8192
HEAD_DIM = 128
ATTN_GROUPS = ((128, 1), (512, 4), (2048, 16))
N_GROUPS = len(ATTN_GROUPS)
HEADS_PER_GROUP = 8
GROUP_WIDTH = HEADS_PER_GROUP * HEAD_DIM
ATTN_WIDTH = N_GROUPS * GROUP_WIDTH
CONV_WIDTH = D_MODEL // 2
CONV_KERNEL = 31
CONV_OFF = 3 * ATTN_WIDTH
GATE_OFF = CONV_OFF + 2 * CONV_WIDTH
IN_WIDTH = GATE_OFF + 2 * D_MODEL
REL_BUCKETS = 32
REL_MAX_DISTANCE = 1024
N_EXPERTS = 32
TOP_K = 4
EXPERT_FF = 1536
SWIGLU_ALPHA = 1.702
SWIGLU_LIMIT = 7.0
NORM_EPS = 1e-6
NEG_INF = -1e30
N_SIDE = 64

V7X_VMEM_LIMIT = 56 * 1024 * 1024

SLOT_BLOCK = 128
BLOCKS_PER_TILE = 9
N_ASSIGN = SEQ * TOP_K
TILE_ROWS = SLOT_BLOCK * BLOCKS_PER_TILE
TILE_ROW_VARIANTS = (8, 9)
GATHER_CHUNK = 384
CHUNKS_PER_TILE = TILE_ROWS // GATHER_CHUNK
N_SLOT_BLOCKS = (N_ASSIGN + N_EXPERTS * (SLOT_BLOCK - 1)) // SLOT_BLOCK + 1
MAX_TILES = (N_SLOT_BLOCKS + N_EXPERTS * (BLOCKS_PER_TILE - 1)) // BLOCKS_PER_TILE
HALF = D_MODEL // 2

F32 = jnp.float32
BF16 = jnp.bfloat16
U32 = jnp.uint32


def _params(sem, vmem=V7X_VMEM_LIMIT):
    return pltpu.CompilerParams(dimension_semantics=sem, vmem_limit_bytes=vmem)


def _pack_halves(v):
    n = v.shape[-1] // 2
    lo = pltpu.bitcast(v[:, :n].astype(BF16).astype(F32), U32) >> 16
    hi = pltpu.bitcast(v[:, n:].astype(BF16).astype(F32), U32) & jnp.uint32(0xFFFF0000)
    return lo | hi


def _unpack_halves(p):
    lo = pltpu.bitcast(p << 16, F32)
    hi = pltpu.bitcast(p & jnp.uint32(0xFFFF0000), F32)
    return lo, hi


TOKEN_TILE = 8


def _store_token_tiles(ref, row0, rows, packed):
    for half in range(2):
        for c in range(TOKEN_TILE):
            col = (half * TOKEN_TILE + c) * 128
            ref[half, pl.ds(row0 * TOKEN_TILE + c, rows, stride=TOKEN_TILE), :] = packed[:, col:col + 128]


def _load_token_tile_column(ref, lead, row0, rows, half, c):
    return ref[(*lead, half, pl.ds(row0 * TOKEN_TILE + c, rows, stride=TOKEN_TILE), slice(None))]


def _ada_kernel(c_ref, w_ref, b_ref, o_ref, cs_ref):
    @pl.when(pl.program_id(0) == 0)
    def _():
        c = c_ref[...]
        cs_ref[...] = c * jax.nn.sigmoid(c)

    tn = o_ref.shape[-1]
    rows_per_step = 64

    def body(r, acc):
        rows = pl.ds(pl.multiple_of(r * rows_per_step, rows_per_step), rows_per_step)
        cs = cs_ref[rows, :]
        prod = w_ref[rows, :] * jnp.concatenate([cs] * (tn // 128), axis=1)
        return acc + prod.reshape(rows_per_step // 8, 8, tn).sum(axis=0)

    acc = lax.fori_loop(0, D_MODEL // rows_per_step, body, jnp.zeros((8, tn), F32))
    o_ref[...] = jnp.sum(acc, axis=0, keepdims=True) + b_ref[...]


def _ada(c, w_ada, b_ada):
    tn = 512
    n = w_ada.shape[-1]
    c_lanes = jnp.broadcast_to(c.reshape(D_MODEL, 1), (D_MODEL, 128))
    return pl.pallas_call(
        _ada_kernel,
        grid=(n // tn,),
        in_specs=[pl.BlockSpec((D_MODEL, 128), lambda j: (0, 0)),
                  pl.BlockSpec((D_MODEL, tn), lambda j: (0, j)),
                  pl.BlockSpec((1, tn), lambda j: (0, j))],
        out_specs=pl.BlockSpec((1, tn), lambda j: (0, j)),
        out_shape=jax.ShapeDtypeStruct((1, n), F32),
        scratch_shapes=[pltpu.VMEM((D_MODEL, 128), F32)],
        compiler_params=_params(("arbitrary",)),
        name="ada_mod",
    )(c_lanes, w_ada, b_ada.reshape(1, n))


def _modulated_norm(x, g, scale, shift):
    ms = jnp.mean(x * x, axis=-1, keepdims=True)
    y = x * lax.rsqrt(ms + NORM_EPS) * g
    return y * (1.0 + scale) + shift


def _norm1_kernel(x_ref, g_ref, sc_ref, sh_ref, o_ref):
    o_ref[...] = _modulated_norm(x_ref[...], g_ref[...], sc_ref[...], sh_ref[...]).astype(o_ref.dtype)


def _norm1(x, g, mod):
    tm = 256
    vec = lambda k: pl.BlockSpec((1, D_MODEL), lambda i, k=k: (0, k))
    return pl.pallas_call(
        _norm1_kernel,
        grid=(SEQ // tm,),
        in_specs=[pl.BlockSpec((tm, D_MODEL), lambda i: (i, 0)), vec(0), vec(1), vec(0)],
        out_specs=pl.BlockSpec((tm, D_MODEL), lambda i: (i, 0)),
        out_shape=jax.ShapeDtypeStruct((SEQ, D_MODEL), BF16),
        compiler_params=_params(("arbitrary",)),
        name="norm1",
    )(x, g.reshape(1, D_MODEL), mod, mod)


def _qkv_kernel(h_ref, w_ref, gq_ref, gk_ref, o_ref, wb_ref, acc_ref, *, dil):
    which = pl.program_id(0)
    tm = h_ref.shape[0]

    @pl.when(pl.program_id(1) == 0)
    def _():
        wb_ref[...] = w_ref[...].astype(BF16)

    acc = jnp.dot(h_ref[...], wb_ref[...], preferred_element_type=F32)

    is_v = which == 2
    gain = jnp.where(is_v, 1.0, jnp.where(which == 0, gq_ref[...] * HEAD_DIM ** -0.5, gk_ref[...]))
    for h in range(HEADS_PER_GROUP):
        sl = slice(h * HEAD_DIM, (h + 1) * HEAD_DIM)
        blk = acc[:, sl]
        ms = jnp.mean(blk * blk, axis=-1, keepdims=True)
        vals = blk * jnp.where(is_v, 1.0, lax.rsqrt(ms + NORM_EPS)) * gain
        if dil == 1:
            o_ref[0, :, sl] = vals.astype(BF16)
        else:
            acc_ref[h] = vals
            for r in range(dil):
                o_ref[r, :, sl] = acc_ref[h, pl.ds(r, tm // dil, stride=dil), :].astype(BF16)


def _qkv_proj(h, w_in, gq, gk, group, dil):
    tm = 512
    seg = SEQ // dil
    return pl.pallas_call(
        functools.partial(_qkv_kernel, dil=dil),
        grid=(3, SEQ // tm),
        in_specs=[pl.BlockSpec((tm, D_MODEL), lambda w, i: (i, 0)),
                  pl.BlockSpec((D_MODEL, GROUP_WIDTH), lambda w, i: (0, w * N_GROUPS + group),
                               pipeline_mode=pl.Buffered(1)),
                  pl.BlockSpec((1, HEAD_DIM), lambda w, i: (0, 0)),
                  pl.BlockSpec((1, HEAD_DIM), lambda w, i: (0, 0))],
        out_specs=pl.BlockSpec((None, dil, tm // dil, GROUP_WIDTH), lambda w, i: (w, 0, i, 0)),
        out_shape=jax.ShapeDtypeStruct((3, dil, seg, GROUP_WIDTH), BF16),
        scratch_shapes=[pltpu.VMEM((D_MODEL, GROUP_WIDTH), BF16), pltpu.VMEM((HEADS_PER_GROUP, tm, HEAD_DIM), F32)],
        compiler_params=_params(("arbitrary", "arbitrary")),
        name=f"qkv_proj_g{group}",
    )(h, w_in, gq.reshape(1, HEAD_DIM), gk.reshape(1, HEAD_DIM))


def _glu_kernel(h_ref, wa_ref, wg_ref, o_ref, wab_ref, wgb_ref):
    @pl.when(pl.program_id(1) == 0)
    def _():
        wab_ref[...] = wa_ref[...].astype(BF16)
        wgb_ref[...] = wg_ref[...].astype(BF16)

    h = h_ref[...]
    a = jnp.dot(h, wab_ref[...], preferred_element_type=F32)
    g = jnp.dot(h, wgb_ref[...], preferred_element_type=F32)
    o_ref[...] = (a * jax.nn.sigmoid(g)).astype(o_ref.dtype)


def _conv_glu_proj(h, w_in):
    tm, tn = 512, 512
    a0 = CONV_OFF // tn
    g0 = (CONV_OFF + CONV_WIDTH) // tn
    return pl.pallas_call(
        _glu_kernel,
        grid=(CONV_WIDTH // tn, SEQ // tm),
        in_specs=[pl.BlockSpec((tm, D_MODEL), lambda j, i: (i, 0)),
                  pl.BlockSpec((D_MODEL, tn), lambda j, i: (0, a0 + j)),
                  pl.BlockSpec((D_MODEL, tn), lambda j, i: (0, g0 + j))],
        out_specs=pl.BlockSpec((tm, tn), lambda j, i: (i, j)),
        out_shape=jax.ShapeDtypeStruct((SEQ, CONV_WIDTH), BF16),
        scratch_shapes=[pltpu.VMEM((D_MODEL, tn), BF16), pltpu.VMEM((D_MODEL, tn), BF16)],
        compiler_params=_params(("arbitrary", "arbitrary")),
        name="conv_glu_proj",
    )(h, w_in, w_in)


def _gate_kernel(h_ref, w_ref, o_ref, wb_ref):
    @pl.when(pl.program_id(1) == 0)
    def _():
        wb_ref[...] = w_ref[...].astype(BF16)

    acc = jnp.dot(h_ref[...], wb_ref[...], preferred_element_type=F32)
    o_ref[...] = jax.nn.sigmoid(acc).astype(o_ref.dtype)


def _gate_proj(h, w_in):
    tm, tn = 1024, 512
    c0 = GATE_OFF // tn
    return pl.pallas_call(
        _gate_kernel,
        grid=(2 * D_MODEL // tn, SEQ // tm),
        in_specs=[pl.BlockSpec((tm, D_MODEL), lambda j, i: (i, 0)),
                  pl.BlockSpec((D_MODEL, tn), lambda j, i: (0, c0 + j))],
        out_specs=pl.BlockSpec((tm, tn), lambda j, i: (i, j)),
        out_shape=jax.ShapeDtypeStruct((SEQ, 2 * D_MODEL), BF16),
        scratch_shapes=[pltpu.VMEM((D_MODEL, tn), BF16)],
        compiler_params=_params(("arbitrary", "arbitrary")),
        name="gate_proj",
    )(h, w_in)


ATTN_TQ = 128
ATTN_TK = ATTN_TQ + 2 * N_SIDE
LSE_LANES = 128


def _bucket_tile(dil):
    a = np.arange(ATTN_TQ)[:, None]
    j = np.arange(ATTN_TK)[None, :]
    steps = j - N_SIDE - a
    rel = steps * dil
    nb = REL_BUCKETS // 2
    max_exact = nb // 2
    n = np.abs(rel)
    side = np.where(rel > 0, nb, 0)
    nf = np.maximum(n, 1).astype(np.float32)
    large = max_exact + (np.log(nf / np.float32(max_exact)) / np.float32(math.log(REL_MAX_DISTANCE / max_exact))
                         * np.float32(nb - max_exact)).astype(np.int32)
    large = np.minimum(large, nb - 1)
    bucket = side + np.where(n < max_exact, n, large)
    return np.where(np.abs(steps) <= N_SIDE, bucket, -1).astype(np.int32)


def _bias_kernel(tbl_ref, idx_ref, o_ref, *, group):
    h = pl.program_id(0)
    idx = idx_ref[...]
    acc = jnp.full(idx.shape, NEG_INF, F32)
    for b in range(REL_BUCKETS):
        acc = jnp.where(idx == b, tbl_ref[b, group * HEADS_PER_GROUP + h], acc)
    o_ref[...] = acc


def _bias_tiles(rel_bias, group, dil):
    return pl.pallas_call(
        functools.partial(_bias_kernel, group=group),
        grid=(HEADS_PER_GROUP,),
        in_specs=[pl.BlockSpec(memory_space=pltpu.SMEM),
                  pl.BlockSpec((ATTN_TQ, ATTN_TK), lambda h: (0, 0))],
        out_specs=pl.BlockSpec((None, ATTN_TQ, ATTN_TK), lambda h: (h, 0, 0)),
        out_shape=jax.ShapeDtypeStruct((HEADS_PER_GROUP, ATTN_TQ, ATTN_TK), F32),
        compiler_params=_params(("arbitrary",)),
        name=f"attn_bias_g{group}",
    )(rel_bias, jnp.asarray(_bucket_tile(dil)))


def _attn_kernel(q_ref, kp_ref, kc_ref, kn_ref, vp_ref, vc_ref, vn_ref, bias_ref, o_ref, lse_ref, *, seg):
    i = pl.program_id(1)
    q = q_ref[...]
    k = jnp.concatenate([kp_ref[...], kc_ref[...], kn_ref[...]], axis=0)
    v = jnp.concatenate([vp_ref[...], vc_ref[...], vn_ref[...]], axis=0)
    kpos = i * ATTN_TQ - N_SIDE + lax.broadcasted_iota(jnp.int32, (1, ATTN_TK), 1)
    valid = (kpos >= 0) & (kpos < seg)
    lane = lax.broadcasted_iota(jnp.int32, (ATTN_TQ, LSE_LANES), 1)
    lse_all = jnp.zeros((ATTN_TQ, LSE_LANES), F32)
    for h in range(HEADS_PER_GROUP):
        sl = slice(h * HEAD_DIM, (h + 1) * HEAD_DIM)
        s = lax.dot_general(q[:, sl], k[:, sl], (((1,), (1,)), ((), ())), preferred_element_type=F32)
        s = jnp.where(valid, s + bias_ref[h], NEG_INF)
        m = jnp.max(s, axis=-1, keepdims=True)
        p = jnp.exp(s - m)
        l = jnp.sum(p, axis=-1, keepdims=True)
        o = jnp.dot(p.astype(BF16), v[:, sl], preferred_element_type=F32)
        o_ref[:, sl] = (o / l).astype(o_ref.dtype)
        lse_all = jnp.where(lane == h, m + jnp.log(l), lse_all)
    lse_ref[...] = lse_all


def _band_attention(qkv, bias, dil):
    seg = SEQ // dil
    sub = ATTN_TQ // N_SIDE
    last = seg // N_SIDE - 1

    def cur(which):
        return pl.BlockSpec((None, None, ATTN_TQ, GROUP_WIDTH), lambda r, i, w=which: (w, r, i, 0))

    def prev(which):
        return pl.BlockSpec((None, None, N_SIDE, GROUP_WIDTH),
                            lambda r, i, w=which: (w, r, jnp.maximum(i * sub - 1, 0), 0))

    def nxt(which):
        return pl.BlockSpec((None, None, N_SIDE, GROUP_WIDTH),
                            lambda r, i, w=which: (w, r, jnp.minimum((i + 1) * sub, last), 0))

    return pl.pallas_call(
        functools.partial(_attn_kernel, seg=seg),
        grid=(dil, seg // ATTN_TQ),
        in_specs=[cur(0), prev(1), cur(1), nxt(1), prev(2), cur(2), nxt(2),
                  pl.BlockSpec((HEADS_PER_GROUP, ATTN_TQ, ATTN_TK), lambda r, i: (0, 0, 0))],
        out_specs=[pl.BlockSpec((None, ATTN_TQ, GROUP_WIDTH), lambda r, i: (r, i, 0)),
                   pl.BlockSpec((None, ATTN_TQ, LSE_LANES), lambda r, i: (r, i, 0))],
        out_shape=[jax.ShapeDtypeStruct((dil, seg, GROUP_WIDTH), BF16),
                   jax.ShapeDtypeStruct((dil, seg, LSE_LANES), F32)],
        compiler_params=_params(("arbitrary", "arbitrary")),
        name=f"band_attn_d{dil}",
    )(qkv, qkv, qkv, qkv, qkv, qkv, qkv, bias)


def _merge_kernel(o0_ref, o1_ref, o2_ref, l0_ref, l1_ref, l2_ref, out_ref, buf_ref, lbuf_ref):
    tm = out_ref.shape[0]
    for gi, (o_ref, l_ref) in enumerate(((o1_ref, l1_ref), (o2_ref, l2_ref))):
        dil = o_ref.shape[0]
        for r in range(dil):
            rows = pl.ds(r, tm // dil, stride=dil)
            lbuf_ref[gi, rows, :] = l_ref[r]
            for h in range(HEADS_PER_GROUP):
                buf_ref[gi * HEADS_PER_GROUP + h, rows, :] = (
                    o_ref[r, :, h * HEAD_DIM:(h + 1) * HEAD_DIM].astype(F32))
    lse0, lse1, lse2 = l0_ref[0], lbuf_ref[0], lbuf_ref[1]
    top = jnp.maximum(jnp.maximum(lse0, lse1), lse2)
    w0, w1, w2 = jnp.exp(lse0 - top), jnp.exp(lse1 - top), jnp.exp(lse2 - top)
    den = w0 + w1 + w2
    for h in range(HEADS_PER_GROUP):
        sl = slice(h * HEAD_DIM, (h + 1) * HEAD_DIM)
        col = slice(h, h + 1)
        num = (w0[:, col] * o0_ref[0, :, sl].astype(F32) + w1[:, col] * buf_ref[h]
               + w2[:, col] * buf_ref[HEADS_PER_GROUP + h])
        out_ref[:, sl] = (num / den[:, col]).astype(out_ref.dtype)


def _merge_groups(outs, lses):
    tm = 256
    d1, d2 = ATTN_GROUPS[1][1], ATTN_GROUPS[2][1]

    def spec(dil, width):
        return pl.BlockSpec((dil, tm // dil, width), lambda i: (0, i, 0))

    return pl.pallas_call(
        _merge_kernel,
        grid=(SEQ // tm,),
        in_specs=[spec(1, GROUP_WIDTH), spec(d1, GROUP_WIDTH), spec(d2, GROUP_WIDTH),
                  spec(1, LSE_LANES), spec(d1, LSE_LANES), spec(d2, LSE_LANES)],
        out_specs=pl.BlockSpec((tm, GROUP_WIDTH), lambda i: (i, 0)),
        out_shape=jax.ShapeDtypeStruct((SEQ, GROUP_WIDTH), BF16),
        scratch_shapes=[pltpu.VMEM((2 * HEADS_PER_GROUP, tm, HEAD_DIM), F32), pltpu.VMEM((2, tm, LSE_LANES), F32)],
        compiler_params=_params(("arbitrary",)),
        name="attn_merge",
    )(*outs, *lses)


CONV_HALO = 16


def _conv_kernel(up_ref, uc_ref, un_ref, w_ref, b_ref, g_ref, o_ref, buf_ref, y_ref):
    i = pl.program_id(0)
    tm = uc_ref.shape[0]
    first = i == 0
    last = i == pl.num_programs(0) - 1
    buf_ref[0:CONV_HALO, :] = jnp.where(first, 0.0, up_ref[...].astype(F32))
    buf_ref[CONV_HALO:CONV_HALO + tm, :] = uc_ref[...].astype(F32)
    buf_ref[CONV_HALO + tm:, :] = jnp.where(last, 0.0, un_ref[...].astype(F32))
    pad = CONV_KERNEL // 2
    lanes = 256
    for c0 in range(0, CONV_WIDTH, lanes):
        cs = slice(c0, c0 + lanes)
        part = jnp.broadcast_to(b_ref[:, cs], (tm, lanes))
        first = CONV_HALO - pad
        for s in range(8):
            group = None
            for tap in range(CONV_KERNEL):
                off = first + tap
                if off % 8 != s:
                    continue
                term = w_ref[tap:tap + 1, cs] * buf_ref[pl.ds(off - s, tm + 8), cs]
                group = term if group is None else group + term
            part = part + group[s:s + tm, :]
        y_ref[:, cs] = part
    acc = y_ref[...]
    ms = jnp.mean(acc * acc, axis=-1, keepdims=True)
    y = acc * lax.rsqrt(ms + NORM_EPS) * g_ref[...]
    o_ref[...] = (y * jax.nn.sigmoid(y)).astype(o_ref.dtype)


def _conv_module(u, conv_w, conv_b, conv_norm_g):
    tm = 128
    sub = tm // CONV_HALO
    last = SEQ // CONV_HALO - 1
    return pl.pallas_call(
        _conv_kernel,
        grid=(SEQ // tm,),
        in_specs=[pl.BlockSpec((CONV_HALO, CONV_WIDTH), lambda i: (jnp.maximum(i * sub - 1, 0), 0)),
                  pl.BlockSpec((tm, CONV_WIDTH), lambda i: (i, 0)),
                  pl.BlockSpec((CONV_HALO, CONV_WIDTH), lambda i: (jnp.minimum((i + 1) * sub, last), 0)),
                  pl.BlockSpec((CONV_KERNEL, CONV_WIDTH), lambda i: (0, 0)),
                  pl.BlockSpec((1, CONV_WIDTH), lambda i: (0, 0)),
                  pl.BlockSpec((1, CONV_WIDTH), lambda i: (0, 0))],
        out_specs=pl.BlockSpec((tm, CONV_WIDTH), lambda i: (i, 0)),
        out_shape=jax.ShapeDtypeStruct((SEQ, CONV_WIDTH), BF16),
        scratch_shapes=[pltpu.VMEM((tm + 2 * CONV_HALO, CONV_WIDTH), F32), pltpu.VMEM((tm, CONV_WIDTH), F32)],
        compiler_params=_params(("arbitrary",)),
        name="conv_module",
    )(u, u, u, conv_w, conv_b.reshape(1, CONV_WIDTH), conv_norm_g.reshape(1, CONV_WIDTH))


def _branch_kernel(a_ref, u_ref, wa_ref, wc_ref, ga_ref, gc_ref, o_ref, wab_ref, wcb_ref):
    @pl.when(pl.program_id(1) == 0)
    def _():
        wab_ref[...] = wa_ref[...].astype(BF16)
        wcb_ref[...] = wc_ref[...].astype(BF16)

    attn = jnp.dot(a_ref[...], wab_ref[...], preferred_element_type=F32)
    conv = jnp.dot(u_ref[...], wcb_ref[...], preferred_element_type=F32)
    o_ref[...] = (ga_ref[...].astype(F32) * attn + gc_ref[...].astype(F32) * conv).astype(o_ref.dtype)


def _branch_merge(attn, conv, w_attn_out, w_conv_out, gates):
    tm, tn = 1024, 512
    nj = D_MODEL // tn
    return pl.pallas_call(
        _branch_kernel,
        grid=(nj, SEQ // tm),
        in_specs=[pl.BlockSpec((tm, GROUP_WIDTH), lambda j, i: (i, 0)),
                  pl.BlockSpec((tm, CONV_WIDTH), lambda j, i: (i, 0)),
                  pl.BlockSpec((GROUP_WIDTH, tn), lambda j, i: (0, j)),
                  pl.BlockSpec((CONV_WIDTH, tn), lambda j, i: (0, j)),
                  pl.BlockSpec((tm, tn), lambda j, i: (i, j)),
                  pl.BlockSpec((tm, tn), lambda j, i: (i, nj + j))],
        out_specs=pl.BlockSpec((tm, tn), lambda j, i: (i, j)),
        out_shape=jax.ShapeDtypeStruct((SEQ, D_MODEL), BF16),
        scratch_shapes=[pltpu.VMEM((GROUP_WIDTH, tn), BF16), pltpu.VMEM((CONV_WIDTH, tn), BF16)],
        compiler_params=_params(("arbitrary", "arbitrary")),
        name="branch_merge",
    )(attn, conv, w_attn_out, w_conv_out, gates, gates)


def _wo_kernel(m_ref, w_ref, x_ref, g_ref, o_ref, wb_ref):
    @pl.when(pl.program_id(1) == 0)
    def _():
        wb_ref[...] = w_ref[...].astype(BF16)

    y = jnp.dot(m_ref[...], wb_ref[...], preferred_element_type=F32)
    o_ref[...] = x_ref[...] + g_ref[...] * y


def _wo_residual(merged, w_o, x, mod):
    tm, tn = 1024, 512
    nj = D_MODEL // tn
    return pl.pallas_call(
        _wo_kernel,
        grid=(nj, SEQ // tm),
        in_specs=[pl.BlockSpec((tm, D_MODEL), lambda j, i: (i, 0)),
                  pl.BlockSpec((D_MODEL, tn), lambda j, i: (0, j)),
                  pl.BlockSpec((tm, tn), lambda j, i: (i, j)),
                  pl.BlockSpec((1, tn), lambda j, i: (0, 2 * nj + j))],
        out_specs=pl.BlockSpec((tm, tn), lambda j, i: (i, j)),
        out_shape=jax.ShapeDtypeStruct((SEQ, D_MODEL), F32),
        scratch_shapes=[pltpu.VMEM((D_MODEL, tn), BF16)],
        compiler_params=_params(("arbitrary", "arbitrary")),
        name="wo_residual",
    )(merged, w_o, x, mod)


def _router_kernel(x_ref, g_ref, sc_ref, sh_ref, wr_ref, br_ref, hp_ref, idx_ref, wt_ref):
    tm = x_ref.shape[0]
    h = _modulated_norm(x_ref[...], g_ref[...], sc_ref[...], sh_ref[...])
    hp_ref[...] = _pack_halves(h)

    h_hi = h.astype(BF16)
    h_lo = (h - h_hi.astype(F32)).astype(BF16)
    w = wr_ref[...]
    w_hi = w.astype(BF16)
    w_lo = (w - w_hi.astype(F32)).astype(BF16)
    logits = (jnp.dot(h_hi, w_hi, preferred_element_type=F32) + jnp.dot(h_lo, w_hi, preferred_element_type=F32)
              + jnp.dot(h_hi, w_lo, preferred_element_type=F32) + br_ref[...])

    lane = lax.broadcasted_iota(jnp.int32, (tm, N_EXPERTS), 1)
    work = logits
    vals, idxs = [], []
    for _ in range(TOP_K):
        m = jnp.max(work, axis=-1, keepdims=True)
        idx = jnp.min(jnp.where(work == m, lane, N_EXPERTS), axis=-1, keepdims=True)
        vals.append(m)
        idxs.append(idx)
        work = jnp.where(lane == idx, -jnp.inf, work)
    exps = [jnp.exp(v - vals[0]) for v in vals]
    den = exps[0] + exps[1] + exps[2] + exps[3]
    col = lax.broadcasted_iota(jnp.int32, (tm, TOP_K), 1)
    idx_out = jnp.zeros((tm, TOP_K), jnp.int32)
    wt_out = jnp.zeros((tm, TOP_K), F32)
    for k in range(TOP_K):
        idx_out = jnp.where(col == k, idxs[k], idx_out)
        wt_out = jnp.where(col == k, exps[k] / den, wt_out)
    idx_ref[...] = idx_out
    wt_ref[...] = wt_out


def _norm2_router(x1, g, mod, w_router, b_router):
    tm = 256
    vec = lambda k: pl.BlockSpec((1, D_MODEL), lambda i, k=k: (0, k))
    return pl.pallas_call(
        _router_kernel,
        grid=(SEQ // tm,),
        in_specs=[pl.BlockSpec((tm, D_MODEL), lambda i: (i, 0)), vec(0), vec(4), vec(3),
                  pl.BlockSpec((D_MODEL, N_EXPERTS), lambda i: (0, 0)),
                  pl.BlockSpec((1, N_EXPERTS), lambda i: (0, 0))],
        out_specs=[pl.BlockSpec((tm, HALF), lambda i: (i, 0)),
                   pl.BlockSpec((tm, TOP_K), lambda i: (i, 0)),
                   pl.BlockSpec((tm, TOP_K), lambda i: (i, 0))],
        out_shape=[jax.ShapeDtypeStruct((SEQ, HALF), U32),
                   jax.ShapeDtypeStruct((SEQ, TOP_K), jnp.int32),
                   jax.ShapeDtypeStruct((SEQ, TOP_K), F32)],
        compiler_params=_params(("arbitrary",)),
        name="norm2_router",
    )(x1, g.reshape(1, D_MODEL), mod, mod, w_router, b_router.reshape(1, N_EXPERTS))


def _routing_tables(top_idx):
    i32 = jnp.int32
    e_flat = top_idx.reshape(-1)
    onehot = (e_flat[:, None] == jnp.arange(N_EXPERTS, dtype=i32)[None, :]).astype(i32)
    csum = jnp.cumsum(onehot, axis=0)
    rank = jnp.sum(onehot * csum, axis=1) - 1
    counts = csum[-1]
    nblk = (counts + SLOT_BLOCK - 1) // SLOT_BLOCK
    blk_end = jnp.cumsum(nblk)
    blk_start = blk_end - nblk
    ntile = (nblk + BLOCKS_PER_TILE - 1) // BLOCKS_PER_TILE
    tile_end = jnp.cumsum(ntile)
    tile_start = tile_end - ntile
    n_tiles = tile_end[-1]
    n_blocks = blk_end[-1]

    dest = (jnp.sum(onehot * (tile_start * TILE_ROWS)[None, :], axis=1) + rank).astype(i32)
    packed = jnp.sum(onehot * (blk_start * SLOT_BLOCK)[None, :], axis=1) + rank
    tok = jnp.arange(N_ASSIGN, dtype=i32) // TOP_K
    slot_tok = jnp.zeros((N_SLOT_BLOCKS * SLOT_BLOCK,), i32).at[packed].set(tok, unique_indices=True)

    s_raw = jnp.arange(MAX_TILES, dtype=i32)
    s = jnp.minimum(s_raw, n_tiles - 1)
    tile_e = jnp.minimum(jnp.searchsorted(tile_end, s, side='right'), N_EXPERTS - 1).astype(i32)
    local = s - tile_start[tile_e]
    tile_nb = jnp.clip(nblk[tile_e] - BLOCKS_PER_TILE * local, 0, BLOCKS_PER_TILE)
    tile_nb = jnp.where(s_raw < n_tiles, tile_nb, 0).astype(i32)
    tile_b0 = blk_start[tile_e] + BLOCKS_PER_TILE * local
    tile_src = s.astype(i32)
    rows = tile_b0[:, None] * SLOT_BLOCK + jnp.arange(TILE_ROWS, dtype=i32)[None, :]
    tile_tok = slot_tok[jnp.minimum(rows, N_SLOT_BLOCKS * SLOT_BLOCK - 1)].reshape(MAX_TILES, 1, TILE_ROWS)
    return dest, tile_tok, tile_e, tile_src, tile_nb


UP_TF = 256


def _deinterleave_matrix():
    r = lax.broadcasted_iota(jnp.int32, (256, 256), 0)
    c = lax.broadcasted_iota(jnp.int32, (256, 256), 1)
    src = jnp.where(c < 128, 2 * c, 2 * (c - 128) + 1)
    return jnp.where(r == src, 1.0, 0.0).astype(BF16)


def _up_kernel(e_ref, nb_ref, first_ref, next_ref, h_hbm, w_ref, b_ref, o_ref, wb_ref, xb_ref, g_ref, sem):
    s = pl.program_id(0)
    j = pl.program_id(1)
    nb = nb_ref[s]

    def row_copy(r, tok):
        return pltpu.make_async_copy(h_hbm.at[pl.ds(tok, 1), :], g_ref.at[pl.ds(r, 1), :], sem)

    def request_chunk(table_ref, c):
        def issue(i, carry):
            for u in range(2):
                r = c * GATHER_CHUNK + 2 * i + u
                row_copy(r, table_ref[0, 0, r]).start()
            return carry
        lax.fori_loop(0, GATHER_CHUNK // 2, issue, 0, unroll=4)

    def chunks_of(blocks):
        return (blocks * SLOT_BLOCK + GATHER_CHUNK - 1) // GATHER_CHUNK

    @pl.when((s == 0) & (j == 0))
    def _():
        for c in range(CHUNKS_PER_TILE):
            @pl.when(c < chunks_of(nb))
            def _(c=c):
                request_chunk(first_ref, c)

    @pl.when(j == 0)
    def _():
        def drain_row(r, carry):
            row_copy(0, 0).wait()
            return carry

        def drain_chunk(c, carry):
            return lax.fori_loop(0, GATHER_CHUNK, drain_row, carry, unroll=8)

        lax.fori_loop(0, chunks_of(nb), drain_chunk, 0)

        for c in range(CHUNKS_PER_TILE):
            rows = slice(c * GATHER_CHUNK, (c + 1) * GATHER_CHUNK)

            @pl.when(c < chunks_of(nb))
            def _(rows=rows):
                lo, hi = _unpack_halves(g_ref[rows, :])
                xb_ref[rows, :HALF] = lo.astype(BF16)
                xb_ref[rows, HALF:] = hi.astype(BF16)

            @pl.when(c >= chunks_of(nb))
            def _(rows=rows):
                xb_ref[rows, :] = jnp.zeros((GATHER_CHUNK, D_MODEL), BF16)

    nxt = jnp.minimum(s + 1, MAX_TILES - 1)

    @pl.when((j >= 1) & (s + 1 < MAX_TILES) & (j - 1 < chunks_of(nb_ref[nxt])))
    def _():
        request_chunk(next_ref, j - 1)

    def run(n_rows):
        if n_rows == 0:
            o_ref[...] = jnp.zeros((TILE_ROWS, UP_TF), BF16)
            return
        wb_ref[...] = w_ref[...].astype(BF16)
        perm = _deinterleave_matrix()
        gu = jnp.dot(xb_ref[0:n_rows, :], wb_ref[...], preferred_element_type=F32) + b_ref[...]
        for c in range(2 * UP_TF // 256):
            gl = jnp.dot(gu[:, c * 256:(c + 1) * 256].astype(BF16), perm, preferred_element_type=F32)
            glu = jnp.minimum(gl[:, :128], SWIGLU_LIMIT)
            lin = jnp.clip(gl[:, 128:], -SWIGLU_LIMIT, SWIGLU_LIMIT)
            act = glu * jax.nn.sigmoid(SWIGLU_ALPHA * glu) * (lin + 1.0)
            o_ref[0:n_rows, c * 128:(c + 1) * 128] = act.astype(BF16)
        if n_rows < TILE_ROWS:
            o_ref[n_rows:, :] = jnp.zeros((TILE_ROWS - n_rows, UP_TF), BF16)

    _run_row_variant(nb, run)


def _run_row_variant(nb, run):
    lo = 0
    for blocks in TILE_ROW_VARIANTS:
        @pl.when((nb > lo) & (nb <= blocks))
        def _(blocks=blocks):
            run(blocks * SLOT_BLOCK)
        lo = blocks

    @pl.when(nb == 0)
    def _():
        run(0)


def _expert_up(hp, tile_tok, w_up, b_up, tile_e, tile_nb):
    nj = EXPERT_FF // UP_TF
    assert CHUNKS_PER_TILE <= nj - 1
    assert CHUNKS_PER_TILE * GATHER_CHUNK == TILE_ROWS

    def w_index(s, j, e, nb):
        return (e[s], 0, jnp.where(nb[s] > 0, j, nj - 1))

    grid_spec = pltpu.PrefetchScalarGridSpec(
        num_scalar_prefetch=2,
        grid=(MAX_TILES, nj),
        in_specs=[pl.BlockSpec((1, 1, TILE_ROWS), lambda s, j, e, nb: (0, 0, 0), memory_space=pltpu.SMEM),
                  pl.BlockSpec((1, 1, TILE_ROWS), lambda s, j, e, nb: (jnp.minimum(s + 1, MAX_TILES - 1), 0, 0),
                               memory_space=pltpu.SMEM),
                  pl.BlockSpec(memory_space=pl.ANY),
                  pl.BlockSpec((None, D_MODEL, 2 * UP_TF), w_index),
                  pl.BlockSpec((None, 1, 2 * UP_TF), w_index)],
        out_specs=pl.BlockSpec((TILE_ROWS, UP_TF), lambda s, j, e, nb: (s, j)),
        scratch_shapes=[pltpu.VMEM((D_MODEL, 2 * UP_TF), BF16),
                        pltpu.VMEM((TILE_ROWS, D_MODEL), BF16),
                        pltpu.VMEM((TILE_ROWS, HALF), U32),
                        pltpu.SemaphoreType.DMA(())],
    )
    return pl.pallas_call(
        _up_kernel,
        grid_spec=grid_spec,
        out_shape=jax.ShapeDtypeStruct((MAX_TILES * TILE_ROWS, EXPERT_FF), BF16),
        compiler_params=_params(("arbitrary", "arbitrary")),
        name="expert_up",
    )(tile_e, tile_nb, tile_tok, tile_tok, hp, w_up, b_up.reshape(N_EXPERTS, 1, 2 * EXPERT_FF))


DOWN_TN = TOKEN_TILE * 128


def _down_kernel(e_ref, src_ref, nb_ref, a_ref, wl_ref, wh_ref, bl_ref, bh_ref, o_ref, wlb_ref, whb_ref):
    nb = nb_ref[pl.program_id(0)]

    def run(n_rows):
        if n_rows == 0:
            o_ref[...] = jnp.zeros((TILE_ROWS * TOKEN_TILE, 128), U32)
            return
        wlb_ref[...] = wl_ref[...].astype(BF16)
        whb_ref[...] = wh_ref[...].astype(BF16)
        a = a_ref[0:n_rows, :]
        y_lo = jnp.dot(a, wlb_ref[...], preferred_element_type=F32) + bl_ref[...]
        y_hi = jnp.dot(a, whb_ref[...], preferred_element_type=F32) + bh_ref[...]
        packed = _pack_halves(jnp.concatenate([y_lo, y_hi], axis=1))
        for c in range(TOKEN_TILE):
            o_ref[pl.ds(c, n_rows, stride=TOKEN_TILE), :] = packed[:, c * 128:(c + 1) * 128]
        if n_rows < TILE_ROWS:
            o_ref[n_rows * TOKEN_TILE:, :] = jnp.zeros(((TILE_ROWS - n_rows) * TOKEN_TILE, 128), U32)

    _run_row_variant(nb, run)


def _expert_down(act, w_down, b_down, tile_e, tile_src, tile_nb):
    nj = HALF // DOWN_TN

    def w_index(hi):
        return lambda s, j, e, src, nb: (e[s], 0, hi * nj + jnp.where(nb[s] > 0, j, nj - 1))

    grid_spec = pltpu.PrefetchScalarGridSpec(
        num_scalar_prefetch=3,
        grid=(MAX_TILES, nj),
        in_specs=[pl.BlockSpec((TILE_ROWS, EXPERT_FF), lambda s, j, e, src, nb: (src[s], 0)),
                  pl.BlockSpec((None, EXPERT_FF, DOWN_TN), w_index(0)),
                  pl.BlockSpec((None, EXPERT_FF, DOWN_TN), w_index(1)),
                  pl.BlockSpec((None, 1, DOWN_TN), w_index(0)),
                  pl.BlockSpec((None, 1, DOWN_TN), w_index(1))],
        out_specs=pl.BlockSpec((None, TILE_ROWS * TOKEN_TILE, 128), lambda s, j, e, src, nb: (j, s, 0)),
        scratch_shapes=[pltpu.VMEM((EXPERT_FF, DOWN_TN), BF16), pltpu.VMEM((EXPERT_FF, DOWN_TN), BF16)],
    )
    b3 = b_down.reshape(N_EXPERTS, 1, D_MODEL)
    return pl.pallas_call(
        _down_kernel,
        grid_spec=grid_spec,
        out_shape=jax.ShapeDtypeStruct((2, MAX_TILES * TILE_ROWS * TOKEN_TILE, 128), U32),
        compiler_params=_params(("arbitrary", "arbitrary")),
        name="expert_down",
    )(tile_e, tile_src, tile_nb, act, w_down, w_down, b3, b3)


COMBINE_TM = 64


def _combine_kernel(dest_ref, next_ref, y_hbm, x_ref, wt_ref, g_ref, o_ref, buf_ref, sems):
    i = pl.program_id(0)
    cur = i % 2

    def tile_copy(buf, t, k, half, slot):
        src = y_hbm.at[half, pl.ds(pl.multiple_of(slot * TOKEN_TILE, TOKEN_TILE), TOKEN_TILE), :]
        dst = buf_ref.at[buf, k, half, pl.ds(pl.multiple_of(t * TOKEN_TILE, TOKEN_TILE), TOKEN_TILE), :]
        return pltpu.make_async_copy(src, dst, sems.at[buf])

    def issue_all(table_ref, buf):
        def issue(t, carry):
            for k in range(TOP_K):
                slot = table_ref[0, 0, t * TOP_K + k]
                tile_copy(buf, t, k, 0, slot).start(priority=0)
                tile_copy(buf, t, k, 1, slot).start(priority=1)
            return carry
        lax.fori_loop(0, COMBINE_TM, issue, 0, unroll=4)

    def drain(t, carry):
        for k in range(TOP_K):
            tile_copy(cur, t, k, 0, 0).wait()
            tile_copy(cur, t, k, 1, 0).wait()
        return carry

    @pl.when(i == 0)
    def _():
        issue_all(dest_ref, 0)

    @pl.when(i + 1 < pl.num_programs(0))
    def _():
        issue_all(next_ref, 1 - cur)

    lax.fori_loop(0, COMBINE_TM, drain, 0, unroll=4)

    weights = [wt_ref[:, k:k + 1] for k in range(TOP_K)]
    for half in range(2):
        for c in range(TOKEN_TILE):
            col = (half * TOKEN_TILE + c) * 128
            acc_lo = jnp.zeros((COMBINE_TM, 128), F32)
            acc_hi = jnp.zeros((COMBINE_TM, 128), F32)
            for k in range(TOP_K):
                lo, hi = _unpack_halves(_load_token_tile_column(buf_ref, (cur, k), 0, COMBINE_TM, half, c))
                acc_lo = acc_lo + weights[k] * lo
                acc_hi = acc_hi + weights[k] * hi
            o_ref[:, col:col + 128] = x_ref[:, col:col + 128] + g_ref[:, col:col + 128] * acc_lo
            hcol = HALF + col
            o_ref[:, hcol:hcol + 128] = x_ref[:, hcol:hcol + 128] + g_ref[:, hcol:hcol + 128] * acc_hi


def _combine(y_slots, dest, x1, top_w, mod):
    nt = SEQ // COMBINE_TM
    return pl.pallas_call(
        _combine_kernel,
        grid=(nt,),
        in_specs=[pl.BlockSpec((1, 1, COMBINE_TM * TOP_K), lambda i: (i, 0, 0), memory_space=pltpu.SMEM),
                  pl.BlockSpec((1, 1, COMBINE_TM * TOP_K), lambda i: (jnp.minimum(i + 1, nt - 1), 0, 0),
                               memory_space=pltpu.SMEM),
                  pl.BlockSpec(memory_space=pl.ANY),
                  pl.BlockSpec((COMBINE_TM, D_MODEL), lambda i: (i, 0)),
                  pl.BlockSpec((COMBINE_TM, TOP_K), lambda i: (i, 0)),
                  pl.BlockSpec((1, D_MODEL), lambda i: (0, 5))],
        out_specs=pl.BlockSpec((COMBINE_TM, D_MODEL), lambda i: (i, 0)),
        out_shape=jax.ShapeDtypeStruct((SEQ, D_MODEL), F32),
        scratch_shapes=[pltpu.VMEM((2, TOP_K, 2, COMBINE_TM * TOKEN_TILE, 128), U32),
                        pltpu.SemaphoreType.DMA((2,))],
        compiler_params=_params(("arbitrary",)),
        name="moe_combine",
    )(dest.reshape(nt, 1, COMBINE_TM * TOP_K), dest.reshape(nt, 1, COMBINE_TM * TOP_K), y_slots, x1, top_w, mod)


def _layer(x, c, norm1_g, norm2_g, w_ada, b_ada, w_in, q_norm_g, k_norm_g, rel_bias, conv_w, conv_b,
           conv_norm_g, w_attn_out, w_conv_out, w_o, w_router, b_router, w_up, b_up, w_down, b_down):
    mod = _ada(c, w_ada, b_ada)
    h = _norm1(x, norm1_g, mod)

    outs, lses = [], []
    for group, (_, dil) in enumerate(ATTN_GROUPS):
        qkv = _qkv_proj(h, w_in, q_norm_g, k_norm_g, group, dil)
        o, lse = _band_attention(qkv, _bias_tiles(rel_bias, group, dil), dil)
        outs.append(o)
        lses.append(lse)
    attn = _merge_groups(outs, lses)

    u = _conv_module(_conv_glu_proj(h, w_in), conv_w, conv_b, conv_norm_g)
    gates = _gate_proj(h, w_in)
    merged = _branch_merge(attn, u, w_attn_out, w_conv_out, gates)
    x1 = _wo_residual(merged, w_o, x, mod)

    hp, top_idx, top_w = _norm2_router(x1, norm2_g, mod, w_router, b_router)
    dest, tile_tok, tile_e, tile_src, tile_nb = _routing_tables(top_idx)
    act = _expert_up(hp, tile_tok, w_up, b_up, tile_e, tile_nb)
    y_slots = _expert_down(act, w_down, b_down, tile_e, tile_src, tile_nb)
    return _combine(y_slots, dest, x1, top_w, mod)


def kernel(x, c, norm1_g, norm2_g, w_ada, b_ada, w_in, q_norm_g, k_norm_g, rel_bias, conv_w, conv_b,
           conv_norm_g, w_attn_out, w_conv_out, w_o, w_router, b_router, w_up, b_up, w_down, b_down):
    batch = x.shape[0]
    xs = x.reshape(batch * SEQ, D_MODEL)
    for layer in range(w_in.shape[0]):
        xs = _layer(xs, c, norm1_g[layer], norm2_g[layer], w_ada[layer], b_ada[layer], w_in[layer],
                    q_norm_g[layer], k_norm_g[layer], rel_bias, conv_w[layer], conv_b[layer],
                    conv_norm_g[layer], w_attn_out[layer], w_conv_out[layer], w_o[layer],
                    w_router[layer], b_router[layer], w_up[layer], b_up[layer], w_down[layer], b_down[layer])
    return xs.reshape(x.shape)
```

```python
import functools
import math

import numpy as np
import jax
import jax.numpy as jnp
from jax import lax
from jax.experimental import pallas as pl
from jax.experimental.pallas import tpu as pltpu

D_MODEL = 4096
SEQ = 8192
HEAD_DIM = 128
ATTN_GROUPS = ((128, 1), (512, 4), (2048, 16))
N_GROUPS = len(ATTN_GROUPS)
HEADS_PER_GROUP = 8
GROUP_WIDTH = HEADS_PER_GROUP * HEAD_DIM
ATTN_WIDTH = N_GROUPS * GROUP_WIDTH
CONV_WIDTH = D_MODEL // 2
CONV_KERNEL = 31
CONV_OFF = 3 * ATTN_WIDTH
GATE_OFF = CONV_OFF + 2 * CONV_WIDTH
IN_WIDTH = GATE_OFF + 2 * D_MODEL
REL_BUCKETS = 32
REL_MAX_DISTANCE = 1024
N_EXPERTS = 32
TOP_K = 4
EXPERT_FF = 1536
SWIGLU_ALPHA = 1.702
SWIGLU_LIMIT = 7.0
NORM_EPS = 1e-6
NEG_INF = -1e30
N_SIDE = 64

V7X_VMEM_LIMIT = 56 * 1024 * 1024

SLOT_BLOCK = 256
BLOCKS_PER_TILE = 5
N_ASSIGN = SEQ * TOP_K
TILE_ROWS = SLOT_BLOCK * BLOCKS_PER_TILE
TILE_ROW_VARIANTS = (4, 5)
GATHER_CHUNK = 256
CHUNKS_PER_TILE = TILE_ROWS // GATHER_CHUNK
N_SLOT_BLOCKS = (N_ASSIGN + N_EXPERTS * (SLOT_BLOCK - 1)) // SLOT_BLOCK + 1
MAX_TILES = (N_SLOT_BLOCKS + N_EXPERTS * (BLOCKS_PER_TILE - 1)) // BLOCKS_PER_TILE
HALF = D_MODEL // 2

F32 = jnp.float32
BF16 = jnp.bfloat16
U32 = jnp.uint32


def _params(sem, vmem=V7X_VMEM_LIMIT):
    return pltpu.CompilerParams(dimension_semantics=sem, vmem_limit_bytes=vmem)


def _pack_halves(v):
    n = v.shape[-1] // 2
    lo = pltpu.bitcast(v[:, :n].astype(BF16).astype(F32), U32) >> 16
    hi = pltpu.bitcast(v[:, n:].astype(BF16).astype(F32), U32) & jnp.uint32(0xFFFF0000)
    return lo | hi


def _unpack_halves(p):
    lo = pltpu.bitcast(p << 16, F32)
    hi = pltpu.bitcast(p & jnp.uint32(0xFFFF0000), F32)
    return lo, hi


TOKEN_TILE = 8


def _store_token_tiles(ref, row0, rows, packed):
    for half in range(2):
        for c in range(TOKEN_TILE):
            col = (half * TOKEN_TILE + c) * 128
            ref[half, pl.ds(row0 * TOKEN_TILE + c, rows, stride=TOKEN_TILE), :] = packed[:, col:col + 128]


def _load_token_tile_column(ref, lead, row0, rows, half, c):
    return ref[(*lead, half, pl.ds(row0 * TOKEN_TILE + c, rows, stride=TOKEN_TILE), slice(None))]


def _ada_kernel(c_ref, w_ref, b_ref, o_ref, cs_ref):
    @pl.when(pl.program_id(0) == 0)
    def _():
        c = c_ref[...]
        cs_ref[...] = c * jax.nn.sigmoid(c)

    tn = o_ref.shape[-1]
    rows_per_step = 64

    def body(r, acc):
        rows = pl.ds(pl.multiple_of(r * rows_per_step, rows_per_step), rows_per_step)
        cs = cs_ref[rows, :]
        prod = w_ref[rows, :] * jnp.concatenate([cs] * (tn // 128), axis=1)
        return acc + prod.reshape(rows_per_step // 8, 8, tn).sum(axis=0)

    acc = lax.fori_loop(0, D_MODEL // rows_per_step, body, jnp.zeros((8, tn), F32))
    o_ref[...] = jnp.sum(acc, axis=0, keepdims=True) + b_ref[...]


def _ada(c, w_ada, b_ada):
    tn = 512
    n = w_ada.shape[-1]
    c_lanes = jnp.broadcast_to(c.reshape(D_MODEL, 1), (D_MODEL, 128))
    return pl.pallas_call(
        _ada_kernel,
        grid=(n // tn,),
        in_specs=[pl.BlockSpec((D_MODEL, 128), lambda j: (0, 0)),
                  pl.BlockSpec((D_MODEL, tn), lambda j: (0, j)),
                  pl.BlockSpec((1, tn), lambda j: (0, j))],
        out_specs=pl.BlockSpec((1, tn), lambda j: (0, j)),
        out_shape=jax.ShapeDtypeStruct((1, n), F32),
        scratch_shapes=[pltpu.VMEM((D_MODEL, 128), F32)],
        compiler_params=_params(("arbitrary",)),
        name="ada_mod",
    )(c_lanes, w_ada, b_ada.reshape(1, n))


def _modulated_norm(x, g, scale, shift):
    ms = jnp.mean(x * x, axis=-1, keepdims=True)
    y = x * lax.rsqrt(ms + NORM_EPS) * g
    return y * (1.0 + scale) + shift


def _norm1_kernel(x_ref, g_ref, sc_ref, sh_ref, o_ref):
    o_ref[...] = _modulated_norm(x_ref[...], g_ref[...], sc_ref[...], sh_ref[...]).astype(o_ref.dtype)


def _norm1(x, g, mod):
    tm = 256
    vec = lambda k: pl.BlockSpec((1, D_MODEL), lambda i, k=k: (0, k))
    return pl.pallas_call(
        _norm1_kernel,
        grid=(SEQ // tm,),
        in_specs=[pl.BlockSpec((tm, D_MODEL), lambda i: (i, 0)), vec(0), vec(1), vec(0)],
        out_specs=pl.BlockSpec((tm, D_MODEL), lambda i: (i, 0)),
        out_shape=jax.ShapeDtypeStruct((SEQ, D_MODEL), BF16),
        compiler_params=_params(("arbitrary",)),
        name="norm1",
    )(x, g.reshape(1, D_MODEL), mod, mod)


def _qkv_kernel(h_ref, w_ref, gq_ref, gk_ref, o_ref, wb_ref, acc_ref, *, dil):
    which = pl.program_id(0)
    tm = h_ref.shape[0]

    @pl.when(pl.program_id(1) == 0)
    def _():
        wb_ref[...] = w_ref[...].astype(BF16)

    acc = jnp.dot(h_ref[...], wb_ref[...], preferred_element_type=F32)

    is_v = which == 2
    gain = jnp.where(is_v, 1.0, jnp.where(which == 0, gq_ref[...] * HEAD_DIM ** -0.5, gk_ref[...]))
    for h in range(HEADS_PER_GROUP):
        sl = slice(h * HEAD_DIM, (h + 1) * HEAD_DIM)
        blk = acc[:, sl]
        ms = jnp.mean(blk * blk, axis=-1, keepdims=True)
        vals = blk * jnp.where(is_v, 1.0, lax.rsqrt(ms + NORM_EPS)) * gain
        if dil == 1:
            o_ref[0, :, sl] = vals.astype(BF16)
        else:
            acc_ref[h] = vals
            for r in range(dil):
                o_ref[r, :, sl] = acc_ref[h, pl.ds(r, tm // dil, stride=dil), :].astype(BF16)


def _qkv_proj(h, w_in, gq, gk, group, dil):
    tm = 512
    seg = SEQ // dil
    return pl.pallas_call(
        functools.partial(_qkv_kernel, dil=dil),
        grid=(3, SEQ // tm),
        in_specs=[pl.BlockSpec((tm, D_MODEL), lambda w, i: (i, 0)),
                  pl.BlockSpec((D_MODEL, GROUP_WIDTH), lambda w, i: (0, w * N_GROUPS + group),
                               pipeline_mode=pl.Buffered(1)),
                  pl.BlockSpec((1, HEAD_DIM), lambda w, i: (0, 0)),
                  pl.BlockSpec((1, HEAD_DIM), lambda w, i: (0, 0))],
        out_specs=pl.BlockSpec((None, dil, tm // dil, GROUP_WIDTH), lambda w, i: (w, 0, i, 0)),
        out_shape=jax.ShapeDtypeStruct((3, dil, seg, GROUP_WIDTH), BF16),
        scratch_shapes=[pltpu.VMEM((D_MODEL, GROUP_WIDTH), BF16), pltpu.VMEM((HEADS_PER_GROUP, tm, HEAD_DIM), F32)],
        compiler_params=_params(("arbitrary", "arbitrary")),
        name=f"qkv_proj_g{group}",
    )(h, w_in, gq.reshape(1, HEAD_DIM), gk.reshape(1, HEAD_DIM))


def _glu_kernel(h_ref, wa_ref, wg_ref, o_ref, wab_ref, wgb_ref):
    @pl.when(pl.program_id(1) == 0)
    def _():
        wab_ref[...] = wa_ref[...].astype(BF16)
        wgb_ref[...] = wg_ref[...].astype(BF16)

    h = h_ref[...]
    a = jnp.dot(h, wab_ref[...], preferred_element_type=F32)
    g = jnp.dot(h, wgb_ref[...], preferred_element_type=F32)
    o_ref[...] = (a * jax.nn.sigmoid(g)).astype(o_ref.dtype)


def _conv_glu_proj(h, w_in):
    tm, tn = 512, 512
    a0 = CONV_OFF // tn
    g0 = (CONV_OFF + CONV_WIDTH) // tn
    return pl.pallas_call(
        _glu_kernel,
        grid=(CONV_WIDTH // tn, SEQ // tm),
        in_specs=[pl.BlockSpec((tm, D_MODEL), lambda j, i: (i, 0)),
                  pl.BlockSpec((D_MODEL, tn), lambda j, i: (0, a0 + j)),
                  pl.BlockSpec((D_MODEL, tn), lambda j, i: (0, g0 + j))],
        out_specs=pl.BlockSpec((tm, tn), lambda j, i: (i, j)),
        out_shape=jax.ShapeDtypeStruct((SEQ, CONV_WIDTH), BF16),
        scratch_shapes=[pltpu.VMEM((D_MODEL, tn), BF16), pltpu.VMEM((D_MODEL, tn), BF16)],
        compiler_params=_params(("arbitrary", "arbitrary")),
        name="conv_glu_proj",
    )(h, w_in, w_in)


def _gate_kernel(h_ref, w_ref, o_ref, wb_ref):
    @pl.when(pl.program_id(1) == 0)
    def _():
        wb_ref[...] = w_ref[...].astype(BF16)

    acc = jnp.dot(h_ref[...], wb_ref[...], preferred_element_type=F32)
    o_ref[...] = jax.nn.sigmoid(acc).astype(o_ref.dtype)


def _gate_proj(h, w_in):
    tm, tn = 1024, 512
    c0 = GATE_OFF // tn
    return pl.pallas_call(
        _gate_kernel,
        grid=(2 * D_MODEL // tn, SEQ // tm),
        in_specs=[pl.BlockSpec((tm, D_MODEL), lambda j, i: (i, 0)),
                  pl.BlockSpec((D_MODEL, tn), lambda j, i: (0, c0 + j))],
        out_specs=pl.BlockSpec((tm, tn), lambda j, i: (i, j)),
        out_shape=jax.ShapeDtypeStruct((SEQ, 2 * D_MODEL), BF16),
        scratch_shapes=[pltpu.VMEM((D_MODEL, tn), BF16)],
        compiler_params=_params(("arbitrary", "arbitrary")),
        name="gate_proj",
    )(h, w_in)


ATTN_TQ = 128
ATTN_TK = ATTN_TQ + 2 * N_SIDE
LSE_LANES = 128


def _bucket_tile(dil):
    a = np.arange(ATTN_TQ)[:, None]
    j = np.arange(ATTN_TK)[None, :]
    steps = j - N_SIDE - a
    rel = steps * dil
    nb = REL_BUCKETS // 2
    max_exact = nb // 2
    n = np.abs(rel)
    side = np.where(rel > 0, nb, 0)
    nf = np.maximum(n, 1).astype(np.float32)
    large = max_exact + (np.log(nf / np.float32(max_exact)) / np.float32(math.log(REL_MAX_DISTANCE / max_exact))
                         * np.float32(nb - max_exact)).astype(np.int32)
    large = np.minimum(large, nb - 1)
    bucket = side + np.where(n < max_exact, n, large)
    return np.where(np.abs(steps) <= N_SIDE, bucket, -1).astype(np.int32)


def _bias_kernel(tbl_ref, idx_ref, o_ref, *, group):
    h = pl.program_id(0)
    idx = idx_ref[...]
    acc = jnp.full(idx.shape, NEG_INF, F32)
    for b in range(REL_BUCKETS):
        acc = jnp.where(idx == b, tbl_ref[b, group * HEADS_PER_GROUP + h], acc)
    o_ref[...] = acc


def _bias_tiles(rel_bias, group, dil):
    return pl.pallas_call(
        functools.partial(_bias_kernel, group=group),
        grid=(HEADS_PER_GROUP,),
        in_specs=[pl.BlockSpec(memory_space=pltpu.SMEM),
                  pl.BlockSpec((ATTN_TQ, ATTN_TK), lambda h: (0, 0))],
        out_specs=pl.BlockSpec((None, ATTN_TQ, ATTN_TK), lambda h: (h, 0, 0)),
        out_shape=jax.ShapeDtypeStruct((HEADS_PER_GROUP, ATTN_TQ, ATTN_TK), F32),
        compiler_params=_params(("arbitrary",)),
        name=f"attn_bias_g{group}",
    )(rel_bias, jnp.asarray(_bucket_tile(dil)))


def _attn_kernel(q_ref, kp_ref, kc_ref, kn_ref, vp_ref, vc_ref, vn_ref, bias_ref, o_ref, lse_ref, *, seg):
    i = pl.program_id(1)
    q = q_ref[...]
    k = jnp.concatenate([kp_ref[...], kc_ref[...], kn_ref[...]], axis=0)
    v = jnp.concatenate([vp_ref[...], vc_ref[...], vn_ref[...]], axis=0)
    kpos = i * ATTN_TQ - N_SIDE + lax.broadcasted_iota(jnp.int32, (1, ATTN_TK), 1)
    valid = (kpos >= 0) & (kpos < seg)
    lane = lax.broadcasted_iota(jnp.int32, (ATTN_TQ, LSE_LANES), 1)
    lse_all = jnp.zeros((ATTN_TQ, LSE_LANES), F32)
    for h in range(HEADS_PER_GROUP):
        sl = slice(h * HEAD_DIM, (h + 1) * HEAD_DIM)
        s = lax.dot_general(q[:, sl], k[:, sl], (((1,), (1,)), ((), ())), preferred_element_type=F32)
        s = jnp.where(valid, s + bias_ref[h], NEG_INF)
        m = jnp.max(s, axis=-1, keepdims=True)
        p = jnp.exp(s - m)
        l = jnp.sum(p, axis=-1, keepdims=True)
        o = jnp.dot(p.astype(BF16), v[:, sl], preferred_element_type=F32)
        o_ref[:, sl] = (o / l).astype(o_ref.dtype)
        lse_all = jnp.where(lane == h, m + jnp.log(l), lse_all)
    lse_ref[...] = lse_all


def _band_attention(qkv, bias, dil):
    seg = SEQ // dil
    sub = ATTN_TQ // N_SIDE
    last = seg // N_SIDE - 1

    def cur(which):
        return pl.BlockSpec((None, None, ATTN_TQ, GROUP_WIDTH), lambda r, i, w=which: (w, r, i, 0))

    def prev(which):
        return pl.BlockSpec((None, None, N_SIDE, GROUP_WIDTH),
                            lambda r, i, w=which: (w, r, jnp.maximum(i * sub - 1, 0), 0))

    def nxt(which):
        return pl.BlockSpec((None, None, N_SIDE, GROUP_WIDTH),
                            lambda r, i, w=which: (w, r, jnp.minimum((i + 1) * sub, last), 0))

    return pl.pallas_call(
        functools.partial(_attn_kernel, seg=seg),
        grid=(dil, seg // ATTN_TQ),
        in_specs=[cur(0), prev(1), cur(1), nxt(1), prev(2), cur(2), nxt(2),
                  pl.BlockSpec((HEADS_PER_GROUP, ATTN_TQ, ATTN_TK), lambda r, i: (0, 0, 0))],
        out_specs=[pl.BlockSpec((None, ATTN_TQ, GROUP_WIDTH), lambda r, i: (r, i, 0)),
                   pl.BlockSpec((None, ATTN_TQ, LSE_LANES), lambda r, i: (r, i, 0))],
        out_shape=[jax.ShapeDtypeStruct((dil, seg, GROUP_WIDTH), BF16),
                   jax.ShapeDtypeStruct((dil, seg, LSE_LANES), F32)],
        compiler_params=_params(("arbitrary", "arbitrary")),
        name=f"band_attn_d{dil}",
    )(qkv, qkv, qkv, qkv, qkv, qkv, qkv, bias)


def _merge_kernel(o0_ref, o1_ref, o2_ref, l0_ref, l1_ref, l2_ref, out_ref, buf_ref, lbuf_ref):
    tm = out_ref.shape[0]
    for gi, (o_ref, l_ref) in enumerate(((o1_ref, l1_ref), (o2_ref, l2_ref))):
        dil = o_ref.shape[0]
        for r in range(dil):
            rows = pl.ds(r, tm // dil, stride=dil)
            lbuf_ref[gi, rows, :] = l_ref[r]
            for h in range(HEADS_PER_GROUP):
                buf_ref[gi * HEADS_PER_GROUP + h, rows, :] = (
                    o_ref[r, :, h * HEAD_DIM:(h + 1) * HEAD_DIM].astype(F32))
    lse0, lse1, lse2 = l0_ref[0], lbuf_ref[0], lbuf_ref[1]
    top = jnp.maximum(jnp.maximum(lse0, lse1), lse2)
    w0, w1, w2 = jnp.exp(lse0 - top), jnp.exp(lse1 - top), jnp.exp(lse2 - top)
    den = w0 + w1 + w2
    for h in range(HEADS_PER_GROUP):
        sl = slice(h * HEAD_DIM, (h + 1) * HEAD_DIM)
        col = slice(h, h + 1)
        num = (w0[:, col] * o0_ref[0, :, sl].astype(F32) + w1[:, col] * buf_ref[h]
               + w2[:, col] * buf_ref[HEADS_PER_GROUP + h])
        out_ref[:, sl] = (num / den[:, col]).astype(out_ref.dtype)


def _merge_groups(outs, lses):
    tm = 256
    d1, d2 = ATTN_GROUPS[1][1], ATTN_GROUPS[2][1]

    def spec(dil, width):
        return pl.BlockSpec((dil, tm // dil, width), lambda i: (0, i, 0))

    return pl.pallas_call(
        _merge_kernel,
        grid=(SEQ // tm,),
        in_specs=[spec(1, GROUP_WIDTH), spec(d1, GROUP_WIDTH), spec(d2, GROUP_WIDTH),
                  spec(1, LSE_LANES), spec(d1, LSE_LANES), spec(d2, LSE_LANES)],
        out_specs=pl.BlockSpec((tm, GROUP_WIDTH), lambda i: (i, 0)),
        out_shape=jax.ShapeDtypeStruct((SEQ, GROUP_WIDTH), BF16),
        scratch_shapes=[pltpu.VMEM((2 * HEADS_PER_GROUP, tm, HEAD_DIM), F32), pltpu.VMEM((2, tm, LSE_LANES), F32)],
        compiler_params=_params(("arbitrary",)),
        name="attn_merge",
    )(*outs, *lses)


CONV_HALO = 16


def _conv_kernel(up_ref, uc_ref, un_ref, w_ref, b_ref, g_ref, o_ref, buf_ref, y_ref):
    i = pl.program_id(0)
    tm = uc_ref.shape[0]
    first = i == 0
    last = i == pl.num_programs(0) - 1
    buf_ref[0:CONV_HALO, :] = jnp.where(first, 0.0, up_ref[...].astype(F32))
    buf_ref[CONV_HALO:CONV_HALO + tm, :] = uc_ref[...].astype(F32)
    buf_ref[CONV_HALO + tm:, :] = jnp.where(last, 0.0, un_ref[...].astype(F32))
    pad = CONV_KERNEL // 2
    lanes = 256
    for c0 in range(0, CONV_WIDTH, lanes):
        cs = slice(c0, c0 + lanes)
        part = jnp.broadcast_to(b_ref[:, cs], (tm, lanes))
        first = CONV_HALO - pad
        for s in range(8):
            group = None
            for tap in range(CONV_KERNEL):
                off = first + tap
                if off % 8 != s:
                    continue
                term = w_ref[tap:tap + 1, cs] * buf_ref[pl.ds(off - s, tm + 8), cs]
                group = term if group is None else group + term
            part = part + group[s:s + tm, :]
        y_ref[:, cs] = part
    acc = y_ref[...]
    ms = jnp.mean(acc * acc, axis=-1, keepdims=True)
    y = acc * lax.rsqrt(ms + NORM_EPS) * g_ref[...]
    o_ref[...] = (y * jax.nn.sigmoid(y)).astype(o_ref.dtype)


def _conv_module(u, conv_w, conv_b, conv_norm_g):
    tm = 128
    sub = tm // CONV_HALO
    last = SEQ // CONV_HALO - 1
    return pl.pallas_call(
        _conv_kernel,
        grid=(SEQ // tm,),
        in_specs=[pl.BlockSpec((CONV_HALO, CONV_WIDTH), lambda i: (jnp.maximum(i * sub - 1, 0), 0)),
                  pl.BlockSpec((tm, CONV_WIDTH), lambda i: (i, 0)),
                  pl.BlockSpec((CONV_HALO, CONV_WIDTH), lambda i: (jnp.minimum((i + 1) * sub, last), 0)),
                  pl.BlockSpec((CONV_KERNEL, CONV_WIDTH), lambda i: (0, 0)),
                  pl.BlockSpec((1, CONV_WIDTH), lambda i: (0, 0)),
                  pl.BlockSpec((1, CONV_WIDTH), lambda i: (0, 0))],
        out_specs=pl.BlockSpec((tm, CONV_WIDTH), lambda i: (i, 0)),
        out_shape=jax.ShapeDtypeStruct((SEQ, CONV_WIDTH), BF16),
        scratch_shapes=[pltpu.VMEM((tm + 2 * CONV_HALO, CONV_WIDTH), F32), pltpu.VMEM((tm, CONV_WIDTH), F32)],
        compiler_params=_params(("arbitrary",)),
        name="conv_module",
    )(u, u, u, conv_w, conv_b.reshape(1, CONV_WIDTH), conv_norm_g.reshape(1, CONV_WIDTH))


def _branch_kernel(a_ref, u_ref, wa_ref, wc_ref, ga_ref, gc_ref, o_ref, wab_ref, wcb_ref):
    @pl.when(pl.program_id(1) == 0)
    def _():
        wab_ref[...] = wa_ref[...].astype(BF16)
        wcb_ref[...] = wc_ref[...].astype(BF16)

    attn = jnp.dot(a_ref[...], wab_ref[...], preferred_element_type=F32)
    conv = jnp.dot(u_ref[...], wcb_ref[...], preferred_element_type=F32)
    o_ref[...] = (ga_ref[...].astype(F32) * attn + gc_ref[...].astype(F32) * conv).astype(o_ref.dtype)


def _branch_merge(attn, conv, w_attn_out, w_conv_out, gates):
    tm, tn = 1024, 512
    nj = D_MODEL // tn
    return pl.pallas_call(
        _branch_kernel,
        grid=(nj, SEQ // tm),
        in_specs=[pl.BlockSpec((tm, GROUP_WIDTH), lambda j, i: (i, 0)),
                  pl.BlockSpec((tm, CONV_WIDTH), lambda j, i: (i, 0)),
                  pl.BlockSpec((GROUP_WIDTH, tn), lambda j, i: (0, j)),
                  pl.BlockSpec((CONV_WIDTH, tn), lambda j, i: (0, j)),
                  pl.BlockSpec((tm, tn), lambda j, i: (i, j)),
                  pl.BlockSpec((tm, tn), lambda j, i: (i, nj + j))],
        out_specs=pl.BlockSpec((tm, tn), lambda j, i: (i, j)),
        out_shape=jax.ShapeDtypeStruct((SEQ, D_MODEL), BF16),
        scratch_shapes=[pltpu.VMEM((GROUP_WIDTH, tn), BF16), pltpu.VMEM((CONV_WIDTH, tn), BF16)],
        compiler_params=_params(("arbitrary", "arbitrary")),
        name="branch_merge",
    )(attn, conv, w_attn_out, w_conv_out, gates, gates)


def _wo_kernel(m_ref, w_ref, x_ref, g_ref, o_ref, wb_ref):
    @pl.when(pl.program_id(1) == 0)
    def _():
        wb_ref[...] = w_ref[...].astype(BF16)

    y = jnp.dot(m_ref[...], wb_ref[...], preferred_element_type=F32)
    o_ref[...] = x_ref[...] + g_ref[...] * y


def _wo_residual(merged, w_o, x, mod):
    tm, tn = 1024, 512
    nj = D_MODEL // tn
    return pl.pallas_call(
        _wo_kernel,
        grid=(nj, SEQ // tm),
        in_specs=[pl.BlockSpec((tm, D_MODEL), lambda j, i: (i, 0)),
                  pl.BlockSpec((D_MODEL, tn), lambda j, i: (0, j)),
                  pl.BlockSpec((tm, tn), lambda j, i: (i, j)),
                  pl.BlockSpec((1, tn), lambda j, i: (0, 2 * nj + j))],
        out_specs=pl.BlockSpec((tm, tn), lambda j, i: (i, j)),
        out_shape=jax.ShapeDtypeStruct((SEQ, D_MODEL), F32),
        scratch_shapes=[pltpu.VMEM((D_MODEL, tn), BF16)],
        compiler_params=_params(("arbitrary", "arbitrary")),
        name="wo_residual",
    )(merged, w_o, x, mod)


def _router_kernel(x_ref, g_ref, sc_ref, sh_ref, wr_ref, br_ref, hp_ref, idx_ref, wt_ref):
    tm = x_ref.shape[0]
    h = _modulated_norm(x_ref[...], g_ref[...], sc_ref[...], sh_ref[...])
    hp_ref[...] = _pack_halves(h)

    h_hi = h.astype(BF16)
    h_lo = (h - h_hi.astype(F32)).astype(BF16)
    w = wr_ref[...]
    w_hi = w.astype(BF16)
    w_lo = (w - w_hi.astype(F32)).astype(BF16)
    logits = (jnp.dot(h_hi, w_hi, preferred_element_type=F32) + jnp.dot(h_lo, w_hi, preferred_element_type=F32)
              + jnp.dot(h_hi, w_lo, preferred_element_type=F32) + br_ref[...])

    lane = lax.broadcasted_iota(jnp.int32, (tm, N_EXPERTS), 1)
    work = logits
    vals, idxs = [], []
    for _ in range(TOP_K):
        m = jnp.max(work, axis=-1, keepdims=True)
        idx = jnp.min(jnp.where(work == m, lane, N_EXPERTS), axis=-1, keepdims=True)
        vals.append(m)
        idxs.append(idx)
        work = jnp.where(lane == idx, -jnp.inf, work)
    exps = [jnp.exp(v - vals[0]) for v in vals]
    den = exps[0] + exps[1] + exps[2] + exps[3]
    col = lax.broadcasted_iota(jnp.int32, (tm, TOP_K), 1)
    idx_out = jnp.zeros((tm, TOP_K), jnp.int32)
    wt_out = jnp.zeros((tm, TOP_K), F32)
    for k in range(TOP_K):
        idx_out = jnp.where(col == k, idxs[k], idx_out)
        wt_out = jnp.where(col == k, exps[k] / den, wt_out)
    idx_ref[...] = idx_out
    wt_ref[...] = wt_out


def _norm2_router(x1, g, mod, w_router, b_router):
    tm = 256
    vec = lambda k: pl.BlockSpec((1, D_MODEL), lambda i, k=k: (0, k))
    return pl.pallas_call(
        _router_kernel,
        grid=(SEQ // tm,),
        in_specs=[pl.BlockSpec((tm, D_MODEL), lambda i: (i, 0)), vec(0), vec(4), vec(3),
                  pl.BlockSpec((D_MODEL, N_EXPERTS), lambda i: (0, 0)),
                  pl.BlockSpec((1, N_EXPERTS), lambda i: (0, 0))],
        out_specs=[pl.BlockSpec((tm, HALF), lambda i: (i, 0)),
                   pl.BlockSpec((tm, TOP_K), lambda i: (i, 0)),
                   pl.BlockSpec((tm, TOP_K), lambda i: (i, 0))],
        out_shape=[jax.ShapeDtypeStruct((SEQ, HALF), U32),
                   jax.ShapeDtypeStruct((SEQ, TOP_K), jnp.int32),
                   jax.ShapeDtypeStruct((SEQ, TOP_K), F32)],
        compiler_params=_params(("arbitrary",)),
        name="norm2_router",
    )(x1, g.reshape(1, D_MODEL), mod, mod, w_router, b_router.reshape(1, N_EXPERTS))


def _routing_tables(top_idx):
    i32 = jnp.int32
    e_flat = top_idx.reshape(-1)
    onehot = (e_flat[:, None] == jnp.arange(N_EXPERTS, dtype=i32)[None, :]).astype(i32)
    csum = jnp.cumsum(onehot, axis=0)
    rank = jnp.sum(onehot * csum, axis=1) - 1
    counts = csum[-1]
    nblk = (counts + SLOT_BLOCK - 1) // SLOT_BLOCK
    blk_end = jnp.cumsum(nblk)
    blk_start = blk_end - nblk
    ntile = (nblk + BLOCKS_PER_TILE - 1) // BLOCKS_PER_TILE
    tile_end = jnp.cumsum(ntile)
    tile_start = tile_end - ntile
    n_tiles = tile_end[-1]
    n_blocks = blk_end[-1]

    dest = (jnp.sum(onehot * (tile_start * TILE_ROWS)[None, :], axis=1) + rank).astype(i32)
    packed = jnp.sum(onehot * (blk_start * SLOT_BLOCK)[None, :], axis=1) + rank
    tok = jnp.arange(N_ASSIGN, dtype=i32) // TOP_K
    slot_tok = jnp.zeros((N_SLOT_BLOCKS * SLOT_BLOCK,), i32).at[packed].set(tok, unique_indices=True)

    s_raw = jnp.arange(MAX_TILES, dtype=i32)
    s = jnp.minimum(s_raw, n_tiles - 1)
    tile_e = jnp.minimum(jnp.searchsorted(tile_end, s, side='right'), N_EXPERTS - 1).astype(i32)
    local = s - tile_start[tile_e]
    tile_nb = jnp.clip(nblk[tile_e] - BLOCKS_PER_TILE * local, 0, BLOCKS_PER_TILE)
    tile_nb = jnp.where(s_raw < n_tiles, tile_nb, 0).astype(i32)
    tile_b0 = blk_start[tile_e] + BLOCKS_PER_TILE * local
    tile_src = s.astype(i32)
    rows = tile_b0[:, None] * SLOT_BLOCK + jnp.arange(TILE_ROWS, dtype=i32)[None, :]
    tile_tok = slot_tok[jnp.minimum(rows, N_SLOT_BLOCKS * SLOT_BLOCK - 1)].reshape(MAX_TILES, 1, TILE_ROWS)
    return dest, tile_tok, tile_e, tile_src, tile_nb, n_tiles.astype(i32)


UP_TF = 256
UP_ROW_CHUNK = 256


def _deinterleave_matrix():
    r = lax.broadcasted_iota(jnp.int32, (256, 256), 0)
    c = lax.broadcasted_iota(jnp.int32, (256, 256), 1)
    src = jnp.where(c < 128, 2 * c, 2 * (c - 128) + 1)
    return jnp.where(r == src, 1.0, 0.0).astype(BF16)


def _up_kernel(e_ref, nb_ref, first_ref, next_ref, h_hbm, w_ref, b_ref, o_ref, wb_ref, xb_ref, g_ref, sem):
    s = pl.program_id(0)
    j = pl.program_id(1)
    nb = nb_ref[s]

    def row_copy(r, tok):
        return pltpu.make_async_copy(h_hbm.at[pl.ds(tok, 1), :], g_ref.at[pl.ds(r, 1), :], sem)

    def request_rows(table_ref, first, count):
        def issue(i, carry):
            for u in range(2):
                r = first + 2 * i + u
                row_copy(r, table_ref[0, 0, r]).start(priority=u)
            return carry
        lax.fori_loop(0, count // 2, issue, 0, unroll=4)

    def request_chunk(table_ref, c):
        request_rows(table_ref, c * GATHER_CHUNK, GATHER_CHUNK)

    def chunks_of(blocks):
        return (blocks * SLOT_BLOCK + GATHER_CHUNK - 1) // GATHER_CHUNK

    @pl.when((s == 0) & (j == 0))
    def _():
        for c in range(CHUNKS_PER_TILE):
            @pl.when(c < chunks_of(nb))
            def _(c=c):
                request_chunk(first_ref, c)

    @pl.when(j == 0)
    def _():
        def drain_row(r, carry):
            row_copy(0, 0).wait()
            return carry

        def drain_chunk(c, carry):
            return lax.fori_loop(0, GATHER_CHUNK, drain_row, carry, unroll=8)

        lax.fori_loop(0, chunks_of(nb), drain_chunk, 0)

        for c in range(CHUNKS_PER_TILE):
            rows = slice(c * GATHER_CHUNK, (c + 1) * GATHER_CHUNK)

            @pl.when(c < chunks_of(nb))
            def _(rows=rows):
                lo, hi = _unpack_halves(g_ref[rows, :])
                xb_ref[rows, :HALF] = lo.astype(BF16)
                xb_ref[rows, HALF:] = hi.astype(BF16)

            @pl.when(c >= chunks_of(nb))
            def _(rows=rows):
                xb_ref[rows, :] = jnp.zeros((GATHER_CHUNK, D_MODEL), BF16)

    nxt = jnp.minimum(s + 1, MAX_TILES - 1)

    @pl.when((j >= 1) & (s + 1 < pl.num_programs(0)) & (j - 1 < chunks_of(nb_ref[nxt])))
    def _():
        request_chunk(next_ref, j - 1)

    def run(n_rows):
        if n_rows == 0:
            o_ref[...] = jnp.zeros((TILE_ROWS, UP_TF), BF16)
            return
        wb_ref[...] = w_ref[...].astype(BF16)
        perm = _deinterleave_matrix()
        for m0 in range(0, n_rows, UP_ROW_CHUNK):
            rows = slice(m0, m0 + UP_ROW_CHUNK)
            gu = jnp.dot(xb_ref[rows, :], wb_ref[...], preferred_element_type=F32) + b_ref[...]
            for c in range(2 * UP_TF // 256):
                gl = jnp.dot(gu[:, c * 256:(c + 1) * 256].astype(BF16), perm, preferred_element_type=F32)
                glu = jnp.minimum(gl[:, :128], SWIGLU_LIMIT)
                lin = jnp.clip(gl[:, 128:], -SWIGLU_LIMIT, SWIGLU_LIMIT)
                act = glu * jax.nn.sigmoid(SWIGLU_ALPHA * glu) * (lin + 1.0)
                o_ref[rows, c * 128:(c + 1) * 128] = act.astype(BF16)
        if n_rows < TILE_ROWS:
            o_ref[n_rows:, :] = jnp.zeros((TILE_ROWS - n_rows, UP_TF), BF16)

    _run_row_variant(nb, run)


def _run_row_variant(nb, run):
    lo = 0
    for blocks in TILE_ROW_VARIANTS:
        @pl.when((nb > lo) & (nb <= blocks))
        def _(blocks=blocks):
            run(blocks * SLOT_BLOCK)
        lo = blocks

    @pl.when(nb == 0)
    def _():
        run(0)


def _expert_up(hp, tile_tok, w_up, b_up, tile_e, tile_nb, n_tiles):
    nj = EXPERT_FF // UP_TF
    assert CHUNKS_PER_TILE <= nj - 1
    assert CHUNKS_PER_TILE * GATHER_CHUNK == TILE_ROWS

    def w_index(s, j, e, nb):
        return (e[s], 0, jnp.where(nb[s] > 0, j, nj - 1))

    grid_spec = pltpu.PrefetchScalarGridSpec(
        num_scalar_prefetch=2,
        grid=(n_tiles, nj),
        in_specs=[pl.BlockSpec((1, 1, TILE_ROWS), lambda s, j, e, nb: (0, 0, 0), memory_space=pltpu.SMEM),
                  pl.BlockSpec((1, 1, TILE_ROWS), lambda s, j, e, nb: (jnp.minimum(s + 1, MAX_TILES - 1), 0, 0),
                               memory_space=pltpu.SMEM),
                  pl.BlockSpec(memory_space=pl.ANY),
                  pl.BlockSpec((None, D_MODEL, 2 * UP_TF), w_index),
                  pl.BlockSpec((None, 1, 2 * UP_TF), w_index)],
        out_specs=pl.BlockSpec((TILE_ROWS, UP_TF), lambda s, j, e, nb: (s, j)),
        scratch_shapes=[pltpu.VMEM((D_MODEL, 2 * UP_TF), BF16),
                        pltpu.VMEM((TILE_ROWS, D_MODEL), BF16),
                        pltpu.VMEM((TILE_ROWS, HALF), U32),
                        pltpu.SemaphoreType.DMA(())],
    )
    return pl.pallas_call(
        _up_kernel,
        grid_spec=grid_spec,
        out_shape=jax.ShapeDtypeStruct((MAX_TILES * TILE_ROWS, EXPERT_FF), BF16),
        compiler_params=_params(("arbitrary", "arbitrary")),
        name="expert_up",
    )(tile_e, tile_nb, tile_tok, tile_tok, hp, w_up, b_up.reshape(N_EXPERTS, 1, 2 * EXPERT_FF))


def _zero_kernel(nt_ref, x_hbm, o_ref):
    o_ref[...] = jnp.zeros_like(o_ref)


def _zero_unused_tiles(arr, n_tiles, block_shape, tile_index):
    grid_spec = pltpu.PrefetchScalarGridSpec(
        num_scalar_prefetch=1,
        grid=(MAX_TILES - n_tiles,),
        in_specs=[pl.BlockSpec(memory_space=pl.ANY)],
        out_specs=pl.BlockSpec(block_shape, lambda i, nt: tile_index(nt[0] + i)),
    )
    return pl.pallas_call(
        _zero_kernel,
        grid_spec=grid_spec,
        out_shape=jax.ShapeDtypeStruct(arr.shape, arr.dtype),
        input_output_aliases={1: 0},
        compiler_params=_params(("arbitrary",)),
        name="zero_unused_tiles",
    )(n_tiles.reshape(1), arr)


DOWN_TN = TOKEN_TILE * 128


def _down_kernel(e_ref, src_ref, nb_ref, a_ref, wl_ref, wh_ref, bl_ref, bh_ref, o_ref, wlb_ref, whb_ref):
    nb = nb_ref[pl.program_id(0)]

    def run(n_rows):
        if n_rows == 0:
            o_ref[...] = jnp.zeros((TILE_ROWS * TOKEN_TILE, 128), U32)
            return
        wlb_ref[...] = wl_ref[...].astype(BF16)
        whb_ref[...] = wh_ref[...].astype(BF16)
        for m0 in range(0, n_rows, UP_ROW_CHUNK):
            a = a_ref[m0:m0 + UP_ROW_CHUNK, :]
            y_lo = jnp.dot(a, wlb_ref[...], preferred_element_type=F32) + bl_ref[...]
            y_hi = jnp.dot(a, whb_ref[...], preferred_element_type=F32) + bh_ref[...]
            packed = _pack_halves(jnp.concatenate([y_lo, y_hi], axis=1))
            for c in range(TOKEN_TILE):
                o_ref[pl.ds(m0 * TOKEN_TILE + c, UP_ROW_CHUNK, stride=TOKEN_TILE), :] = (
                    packed[:, c * 128:(c + 1) * 128])
        if n_rows < TILE_ROWS:
            o_ref[n_rows * TOKEN_TILE:, :] = jnp.zeros(((TILE_ROWS - n_rows) * TOKEN_TILE, 128), U32)

    _run_row_variant(nb, run)


def _expert_down(act, w_down, b_down, tile_e, tile_src, tile_nb, n_tiles):
    nj = HALF // DOWN_TN

    def w_index(hi):
        return lambda s, j, e, src, nb: (e[s], 0, hi * nj + jnp.where(nb[s] > 0, j, nj - 1))

    grid_spec = pltpu.PrefetchScalarGridSpec(
        num_scalar_prefetch=3,
        grid=(n_tiles, nj),
        in_specs=[pl.BlockSpec((TILE_ROWS, EXPERT_FF), lambda s, j, e, src, nb: (src[s], 0)),
                  pl.BlockSpec((None, EXPERT_FF, DOWN_TN), w_index(0)),
                  pl.BlockSpec((None, EXPERT_FF, DOWN_TN), w_index(1)),
                  pl.BlockSpec((None, 1, DOWN_TN), w_index(0)),
                  pl.BlockSpec((None, 1, DOWN_TN), w_index(1))],
        out_specs=pl.BlockSpec((None, TILE_ROWS * TOKEN_TILE, 128), lambda s, j, e, src, nb: (j, s, 0)),
        scratch_shapes=[pltpu.VMEM((EXPERT_FF, DOWN_TN), BF16), pltpu.VMEM((EXPERT_FF, DOWN_TN), BF16)],
    )
    b3 = b_down.reshape(N_EXPERTS, 1, D_MODEL)
    return pl.pallas_call(
        _down_kernel,
        grid_spec=grid_spec,
        out_shape=jax.ShapeDtypeStruct((2, MAX_TILES * TILE_ROWS * TOKEN_TILE, 128), U32),
        compiler_params=_params(("arbitrary", "arbitrary")),
        name="expert_down",
    )(tile_e, tile_src, tile_nb, act, w_down, w_down, b3, b3)


COMBINE_TM = 64


def _combine_kernel(dest_ref, next_ref, y_hbm, x_ref, wt_ref, g_ref, o_ref, buf_ref, sems):
    i = pl.program_id(0)
    cur = i % 2

    def tile_copy(buf, t, k, half, slot):
        src = y_hbm.at[half, pl.ds(pl.multiple_of(slot * TOKEN_TILE, TOKEN_TILE), TOKEN_TILE), :]
        dst = buf_ref.at[buf, k, half, pl.ds(pl.multiple_of(t * TOKEN_TILE, TOKEN_TILE), TOKEN_TILE), :]
        return pltpu.make_async_copy(src, dst, sems.at[buf])

    def issue_all(table_ref, buf):
        def issue(t, carry):
            for k in range(TOP_K):
                slot = table_ref[0, 0, t * TOP_K + k]
                tile_copy(buf, t, k, 0, slot).start(priority=0)
                tile_copy(buf, t, k, 1, slot).start(priority=1)
            return carry
        lax.fori_loop(0, COMBINE_TM, issue, 0, unroll=4)

    def drain(t, carry):
        for k in range(TOP_K):
            tile_copy(cur, t, k, 0, 0).wait()
            tile_copy(cur, t, k, 1, 0).wait()
        return carry

    @pl.when(i == 0)
    def _():
        issue_all(dest_ref, 0)

    @pl.when(i + 1 < pl.num_programs(0))
    def _():
        issue_all(next_ref, 1 - cur)

    lax.fori_loop(0, COMBINE_TM, drain, 0, unroll=4)

    weights = [wt_ref[:, k:k + 1] for k in range(TOP_K)]
    for half in range(2):
        for c in range(TOKEN_TILE):
            col = (half * TOKEN_TILE + c) * 128
            acc_lo = jnp.zeros((COMBINE_TM, 128), F32)
            acc_hi = jnp.zeros((COMBINE_TM, 128), F32)
            for k in range(TOP_K):
                lo, hi = _unpack_halves(_load_token_tile_column(buf_ref, (cur, k), 0, COMBINE_TM, half, c))
                acc_lo = acc_lo + weights[k] * lo
                acc_hi = acc_hi + weights[k] * hi
            o_ref[:, col:col + 128] = x_ref[:, col:col + 128] + g_ref[:, col:col + 128] * acc_lo
            hcol = HALF + col
            o_ref[:, hcol:hcol + 128] = x_ref[:, hcol:hcol + 128] + g_ref[:, hcol:hcol + 128] * acc_hi


def _combine(y_slots, dest, x1, top_w, mod):
    nt = SEQ // COMBINE_TM
    return pl.pallas_call(
        _combine_kernel,
        grid=(nt,),
        in_specs=[pl.BlockSpec((1, 1, COMBINE_TM * TOP_K), lambda i: (i, 0, 0), memory_space=pltpu.SMEM),
                  pl.BlockSpec((1, 1, COMBINE_TM * TOP_K), lambda i: (jnp.minimum(i + 1, nt - 1), 0, 0),
                               memory_space=pltpu.SMEM),
                  pl.BlockSpec(memory_space=pl.ANY),
                  pl.BlockSpec((COMBINE_TM, D_MODEL), lambda i: (i, 0)),
                  pl.BlockSpec((COMBINE_TM, TOP_K), lambda i: (i, 0)),
                  pl.BlockSpec((1, D_MODEL), lambda i: (0, 5))],
        out_specs=pl.BlockSpec((COMBINE_TM, D_MODEL), lambda i: (i, 0)),
        out_shape=jax.ShapeDtypeStruct((SEQ, D_MODEL), F32),
        scratch_shapes=[pltpu.VMEM((2, TOP_K, 2, COMBINE_TM * TOKEN_TILE, 128), U32),
                        pltpu.SemaphoreType.DMA((2,))],
        compiler_params=_params(("arbitrary",)),
        name="moe_combine",
    )(dest.reshape(nt, 1, COMBINE_TM * TOP_K), dest.reshape(nt, 1, COMBINE_TM * TOP_K), y_slots, x1, top_w, mod)


def _layer(x, c, norm1_g, norm2_g, w_ada, b_ada, w_in, q_norm_g, k_norm_g, rel_bias, conv_w, conv_b,
           conv_norm_g, w_attn_out, w_conv_out, w_o, w_router, b_router, w_up, b_up, w_down, b_down):
    mod = _ada(c, w_ada, b_ada)
    h = _norm1(x, norm1_g, mod)

    outs, lses = [], []
    for group, (_, dil) in enumerate(ATTN_GROUPS):
        qkv = _qkv_proj(h, w_in, q_norm_g, k_norm_g, group, dil)
        o, lse = _band_attention(qkv, _bias_tiles(rel_bias, group, dil), dil)
        outs.append(o)
        lses.append(lse)
    attn = _merge_groups(outs, lses)

    u = _conv_module(_conv_glu_proj(h, w_in), conv_w, conv_b, conv_norm_g)
    gates = _gate_proj(h, w_in)
    merged = _branch_merge(attn, u, w_attn_out, w_conv_out, gates)
    x1 = _wo_residual(merged, w_o, x, mod)

    hp, top_idx, top_w = _norm2_router(x1, norm2_g, mod, w_router, b_router)
    dest, tile_tok, tile_e, tile_src, tile_nb, n_tiles = _routing_tables(top_idx)
    act = _expert_up(hp, tile_tok, w_up, b_up, tile_e, tile_nb, n_tiles)
    act = _zero_unused_tiles(act, n_tiles, (TILE_ROWS, EXPERT_FF), lambda t: (t, 0))
    y_slots = _expert_down(act, w_down, b_down, tile_e, tile_src, tile_nb, n_tiles)
    y_slots = _zero_unused_tiles(y_slots, n_tiles, (2, TILE_ROWS * TOKEN_TILE, 128), lambda t: (0, t, 0))
    return _combine(y_slots, dest, x1, top_w, mod)


def kernel(x, c, norm1_g, norm2_g, w_ada, b_ada, w_in, q_norm_g, k_norm_g, rel_bias, conv_w, conv_b,
           conv_norm_g, w_attn_out, w_conv_out, w_o, w_router, b_router, w_up, b_up, w_down, b_down):
    batch = x.shape[0]
    xs = x.reshape(batch * SEQ, D_MODEL)
    for layer in range(w_in.shape[0]):
        xs = _layer(xs, c, norm1_g[layer], norm2_g[layer], w_ada[layer], b_ada[layer], w_in[layer],
                    q_norm_g[layer], k_norm_g[layer], rel_bias, conv_w[layer], conv_b[layer],
                    conv_norm_g[layer], w_attn_out[layer], w_conv_out[layer], w_o[layer],
                    w_router[layer], b_router[layer], w_up[layer], b_up[layer], w_down[layer], b_down[layer])
    return xs.reshape(x.shape)
```

```python
import functools
import math

import numpy as np
import jax
import jax.numpy as jnp
from jax import lax
from jax.experimental import pallas as pl
from jax.experimental.pallas import tpu as pltpu

D_MODEL = 4096
SEQ = 8192
HEAD_DIM = 128
ATTN_GROUPS = ((128, 1), (512, 4), (2048, 16))
N_GROUPS = len(ATTN_GROUPS)
HEADS_PER_GROUP = 8
GROUP_WIDTH = HEADS_PER_GROUP * HEAD_DIM
ATTN_WIDTH = N_GROUPS * GROUP_WIDTH
CONV_WIDTH = D_MODEL // 2
CONV_KERNEL = 31
CONV_OFF = 3 * ATTN_WIDTH
GATE_OFF = CONV_OFF + 2 * CONV_WIDTH
IN_WIDTH = GATE_OFF + 2 * D_MODEL
REL_BUCKETS = 32
REL_MAX_DISTANCE = 1024
N_EXPERTS = 32
TOP_K = 4
EXPERT_FF = 1536
SWIGLU_ALPHA = 1.702
SWIGLU_LIMIT = 7.0
NORM_EPS = 1e-6
NEG_INF = -1e30
N_SIDE = 64

V7X_VMEM_LIMIT = 56 * 1024 * 1024

SLOT_BLOCK = 256
BLOCKS_PER_TILE = 5
N_ASSIGN = SEQ * TOP_K
TILE_ROWS = SLOT_BLOCK * BLOCKS_PER_TILE
UP_ROW_VARIANTS = (2, 3, 4, 5)
DOWN_ROW_VARIANTS = (2, 4, 5)
GATHER_CHUNK = 256
CHUNKS_PER_TILE = TILE_ROWS // GATHER_CHUNK
N_SLOT_BLOCKS = (N_ASSIGN + N_EXPERTS * (SLOT_BLOCK - 1)) // SLOT_BLOCK + 1
MAX_TILES = (N_SLOT_BLOCKS + N_EXPERTS * (BLOCKS_PER_TILE - 1)) // BLOCKS_PER_TILE
BALANCED_TILES = (MAX_TILES + N_EXPERTS) // 2 + 1
HALF = D_MODEL // 2

F32 = jnp.float32
BF16 = jnp.bfloat16
U32 = jnp.uint32


def _params(sem, vmem=V7X_VMEM_LIMIT):
    return pltpu.CompilerParams(dimension_semantics=sem, vmem_limit_bytes=vmem)


def _pack_halves(v):
    n = v.shape[-1] // 2
    lo = pltpu.bitcast(v[:, :n].astype(BF16).astype(F32), U32) >> 16
    hi = pltpu.bitcast(v[:, n:].astype(BF16).astype(F32), U32) & jnp.uint32(0xFFFF0000)
    return lo | hi


def _unpack_halves(p):
    lo = pltpu.bitcast(p << 16, F32)
    hi = pltpu.bitcast(p & jnp.uint32(0xFFFF0000), F32)
    return lo, hi


TOKEN_TILE = 8


def _store_token_tiles(ref, row0, rows, packed):
    for half in range(2):
        for c in range(TOKEN_TILE):
            col = (half * TOKEN_TILE + c) * 128
            ref[half, pl.ds(row0 * TOKEN_TILE + c, rows, stride=TOKEN_TILE), :] = packed[:, col:col + 128]


def _load_token_tile_column(ref, lead, row0, rows, half, c):
    return ref[(*lead, half, pl.ds(row0 * TOKEN_TILE + c, rows, stride=TOKEN_TILE), slice(None))]


def _ada_kernel(c_ref, w_ref, b_ref, o_ref, cs_ref):
    @pl.when(pl.program_id(0) == 0)
    def _():
        c = c_ref[...]
        cs_ref[...] = c * jax.nn.sigmoid(c)

    tn = o_ref.shape[-1]
    rows_per_step = 64

    def body(r, acc):
        rows = pl.ds(pl.multiple_of(r * rows_per_step, rows_per_step), rows_per_step)
        cs = cs_ref[rows, :]
        prod = w_ref[rows, :] * jnp.concatenate([cs] * (tn // 128), axis=1)
        return acc + prod.reshape(rows_per_step // 8, 8, tn).sum(axis=0)

    acc = lax.fori_loop(0, D_MODEL // rows_per_step, body, jnp.zeros((8, tn), F32))
    o_ref[...] = jnp.sum(acc, axis=0, keepdims=True) + b_ref[...]


def _ada(c, w_ada, b_ada):
    tn = 512
    n = w_ada.shape[-1]
    c_lanes = jnp.broadcast_to(c.reshape(D_MODEL, 1), (D_MODEL, 128))
    return pl.pallas_call(
        _ada_kernel,
        grid=(n // tn,),
        in_specs=[pl.BlockSpec((D_MODEL, 128), lambda j: (0, 0)),
                  pl.BlockSpec((D_MODEL, tn), lambda j: (0, j)),
                  pl.BlockSpec((1, tn), lambda j: (0, j))],
        out_specs=pl.BlockSpec((1, tn), lambda j: (0, j)),
        out_shape=jax.ShapeDtypeStruct((1, n), F32),
        scratch_shapes=[pltpu.VMEM((D_MODEL, 128), F32)],
        compiler_params=_params(("arbitrary",)),
        name="ada_mod",
    )(c_lanes, w_ada, b_ada.reshape(1, n))


def _modulated_norm(x, g, scale, shift):
    ms = jnp.mean(x * x, axis=-1, keepdims=True)
    y = x * lax.rsqrt(ms + NORM_EPS) * g
    return y * (1.0 + scale) + shift


def _norm1_kernel(x_ref, g_ref, sc_ref, sh_ref, o_ref):
    o_ref[...] = _modulated_norm(x_ref[...], g_ref[...], sc_ref[...], sh_ref[...]).astype(o_ref.dtype)


def _norm1(x, g, mod):
    tm = 256
    vec = lambda k: pl.BlockSpec((1, D_MODEL), lambda i, k=k: (0, k))
    return pl.pallas_call(
        _norm1_kernel,
        grid=(SEQ // tm,),
        in_specs=[pl.BlockSpec((tm, D_MODEL), lambda i: (i, 0)), vec(0), vec(1), vec(0)],
        out_specs=pl.BlockSpec((tm, D_MODEL), lambda i: (i, 0)),
        out_shape=jax.ShapeDtypeStruct((SEQ, D_MODEL), BF16),
        compiler_params=_params(("arbitrary",)),
        name="norm1",
    )(x, g.reshape(1, D_MODEL), mod, mod)


def _qkv_kernel(h_ref, w_ref, gq_ref, gk_ref, o_ref, wb_ref, acc_ref, *, dil):
    which = pl.program_id(0)
    tm = h_ref.shape[0]

    @pl.when(pl.program_id(1) == 0)
    def _():
        wb_ref[...] = w_ref[...].astype(BF16)

    acc = jnp.dot(h_ref[...], wb_ref[...], preferred_element_type=F32)

    is_v = which == 2
    gain = jnp.where(is_v, 1.0, jnp.where(which == 0, gq_ref[...] * HEAD_DIM ** -0.5, gk_ref[...]))
    for h in range(HEADS_PER_GROUP):
        sl = slice(h * HEAD_DIM, (h + 1) * HEAD_DIM)
        blk = acc[:, sl]
        ms = jnp.mean(blk * blk, axis=-1, keepdims=True)
        vals = blk * jnp.where(is_v, 1.0, lax.rsqrt(ms + NORM_EPS)) * gain
        if dil == 1:
            o_ref[0, :, sl] = vals.astype(BF16)
        else:
            acc_ref[h] = vals
            for r in range(dil):
                o_ref[r, :, sl] = acc_ref[h, pl.ds(r, tm // dil, stride=dil), :].astype(BF16)


def _qkv_proj(h, w_in, gq, gk, group, dil):
    tm = 512
    seg = SEQ // dil
    return pl.pallas_call(
        functools.partial(_qkv_kernel, dil=dil),
        grid=(3, SEQ // tm),
        in_specs=[pl.BlockSpec((tm, D_MODEL), lambda w, i: (i, 0)),
                  pl.BlockSpec((D_MODEL, GROUP_WIDTH), lambda w, i: (0, w * N_GROUPS + group),
                               pipeline_mode=pl.Buffered(1)),
                  pl.BlockSpec((1, HEAD_DIM), lambda w, i: (0, 0)),
                  pl.BlockSpec((1, HEAD_DIM), lambda w, i: (0, 0))],
        out_specs=pl.BlockSpec((None, dil, tm // dil, GROUP_WIDTH), lambda w, i: (w, 0, i, 0)),
        out_shape=jax.ShapeDtypeStruct((3, dil, seg, GROUP_WIDTH), BF16),
        scratch_shapes=[pltpu.VMEM((D_MODEL, GROUP_WIDTH), BF16), pltpu.VMEM((HEADS_PER_GROUP, tm, HEAD_DIM), F32)],
        compiler_params=_params(("arbitrary", "arbitrary")),
        name=f"qkv_proj_g{group}",
    )(h, w_in, gq.reshape(1, HEAD_DIM), gk.reshape(1, HEAD_DIM))


def _glu_kernel(h_ref, wa_ref, wg_ref, o_ref, wab_ref, wgb_ref):
    @pl.when(pl.program_id(1) == 0)
    def _():
        wab_ref[...] = wa_ref[...].astype(BF16)
        wgb_ref[...] = wg_ref[...].astype(BF16)

    h = h_ref[...]
    a = jnp.dot(h, wab_ref[...], preferred_element_type=F32)
    g = jnp.dot(h, wgb_ref[...], preferred_element_type=F32)
    o_ref[...] = (a * jax.nn.sigmoid(g)).astype(o_ref.dtype)


def _conv_glu_proj(h, w_in):
    tm, tn = 512, 512
    a0 = CONV_OFF // tn
    g0 = (CONV_OFF + CONV_WIDTH) // tn
    return pl.pallas_call(
        _glu_kernel,
        grid=(CONV_WIDTH // tn, SEQ // tm),
        in_specs=[pl.BlockSpec((tm, D_MODEL), lambda j, i: (i, 0)),
                  pl.BlockSpec((D_MODEL, tn), lambda j, i: (0, a0 + j)),
                  pl.BlockSpec((D_MODEL, tn), lambda j, i: (0, g0 + j))],
        out_specs=pl.BlockSpec((tm, tn), lambda j, i: (i, j)),
        out_shape=jax.ShapeDtypeStruct((SEQ, CONV_WIDTH), BF16),
        scratch_shapes=[pltpu.VMEM((D_MODEL, tn), BF16), pltpu.VMEM((D_MODEL, tn), BF16)],
        compiler_params=_params(("arbitrary", "arbitrary")),
        name="conv_glu_proj",
    )(h, w_in, w_in)


def _gate_kernel(h_ref, w_ref, o_ref, wb_ref):
    @pl.when(pl.program_id(1) == 0)
    def _():
        wb_ref[...] = w_ref[...].astype(BF16)

    acc = jnp.dot(h_ref[...], wb_ref[...], preferred_element_type=F32)
    o_ref[...] = jax.nn.sigmoid(acc).astype(o_ref.dtype)


def _gate_proj(h, w_in):
    tm, tn = 1024, 512
    c0 = GATE_OFF // tn
    return pl.pallas_call(
        _gate_kernel,
        grid=(2 * D_MODEL // tn, SEQ // tm),
        in_specs=[pl.BlockSpec((tm, D_MODEL), lambda j, i: (i, 0)),
                  pl.BlockSpec((D_MODEL, tn), lambda j, i: (0, c0 + j))],
        out_specs=pl.BlockSpec((tm, tn), lambda j, i: (i, j)),
        out_shape=jax.ShapeDtypeStruct((SEQ, 2 * D_MODEL), BF16),
        scratch_shapes=[pltpu.VMEM((D_MODEL, tn), BF16)],
        compiler_params=_params(("arbitrary", "arbitrary")),
        name="gate_proj",
    )(h, w_in)


ATTN_TQ = 128
ATTN_TK = ATTN_TQ + 2 * N_SIDE
LSE_LANES = 128


def _bucket_tile(dil):
    a = np.arange(ATTN_TQ)[:, None]
    j = np.arange(ATTN_TK)[None, :]
    steps = j - N_SIDE - a
    rel = steps * dil
    nb = REL_BUCKETS // 2
    max_exact = nb // 2
    n = np.abs(rel)
    side = np.where(rel > 0, nb, 0)
    nf = np.maximum(n, 1).astype(np.float32)
    large = max_exact + (np.log(nf / np.float32(max_exact)) / np.float32(math.log(REL_MAX_DISTANCE / max_exact))
                         * np.float32(nb - max_exact)).astype(np.int32)
    large = np.minimum(large, nb - 1)
    bucket = side + np.where(n < max_exact, n, large)
    return np.where(np.abs(steps) <= N_SIDE, bucket, -1).astype(np.int32)


def _bias_kernel(tbl_ref, idx_ref, o_ref, *, group):
    h = pl.program_id(0)
    idx = idx_ref[...]
    acc = jnp.full(idx.shape, NEG_INF, F32)
    for b in range(REL_BUCKETS):
        acc = jnp.where(idx == b, tbl_ref[b, group * HEADS_PER_GROUP + h], acc)
    o_ref[...] = acc


def _bias_tiles(rel_bias, group, dil):
    return pl.pallas_call(
        functools.partial(_bias_kernel, group=group),
        grid=(HEADS_PER_GROUP,),
        in_specs=[pl.BlockSpec(memory_space=pltpu.SMEM),
                  pl.BlockSpec((ATTN_TQ, ATTN_TK), lambda h: (0, 0))],
        out_specs=pl.BlockSpec((None, ATTN_TQ, ATTN_TK), lambda h: (h, 0, 0)),
        out_shape=jax.ShapeDtypeStruct((HEADS_PER_GROUP, ATTN_TQ, ATTN_TK), F32),
        compiler_params=_params(("arbitrary",)),
        name=f"attn_bias_g{group}",
    )(rel_bias, jnp.asarray(_bucket_tile(dil)))


def _attn_kernel(q_ref, kp_ref, kc_ref, kn_ref, vp_ref, vc_ref, vn_ref, bias_ref, o_ref, lse_ref, *, seg):
    i = pl.program_id(1)
    q = q_ref[...]
    k = jnp.concatenate([kp_ref[...], kc_ref[...], kn_ref[...]], axis=0)
    v = jnp.concatenate([vp_ref[...], vc_ref[...], vn_ref[...]], axis=0)
    kpos = i * ATTN_TQ - N_SIDE + lax.broadcasted_iota(jnp.int32, (1, ATTN_TK), 1)
    valid = (kpos >= 0) & (kpos < seg)
    lane = lax.broadcasted_iota(jnp.int32, (ATTN_TQ, LSE_LANES), 1)
    lse_all = jnp.zeros((ATTN_TQ, LSE_LANES), F32)
    for h in range(HEADS_PER_GROUP):
        sl = slice(h * HEAD_DIM, (h + 1) * HEAD_DIM)
        s = lax.dot_general(q[:, sl], k[:, sl], (((1,), (1,)), ((), ())), preferred_element_type=F32)
        s = jnp.where(valid, s + bias_ref[h], NEG_INF)
        m = jnp.max(s, axis=-1, keepdims=True)
        p = jnp.exp(s - m)
        l = jnp.sum(p, axis=-1, keepdims=True)
        o = jnp.dot(p.astype(BF16), v[:, sl], preferred_element_type=F32)
        o_ref[:, sl] = (o / l).astype(o_ref.dtype)
        lse_all = jnp.where(lane == h, m + jnp.log(l), lse_all)
    lse_ref[...] = lse_all


def _band_attention(qkv, bias, dil):
    seg = SEQ // dil
    sub = ATTN_TQ // N_SIDE
    last = seg // N_SIDE - 1

    def cur(which):
        return pl.BlockSpec((None, None, ATTN_TQ, GROUP_WIDTH), lambda r, i, w=which: (w, r, i, 0))

    def prev(which):
        return pl.BlockSpec((None, None, N_SIDE, GROUP_WIDTH),
                            lambda r, i, w=which: (w, r, jnp.maximum(i * sub - 1, 0), 0))

    def nxt(which):
        return pl.BlockSpec((None, None, N_SIDE, GROUP_WIDTH),
                            lambda r, i, w=which: (w, r, jnp.minimum((i + 1) * sub, last), 0))

    return pl.pallas_call(
        functools.partial(_attn_kernel, seg=seg),
        grid=(dil, seg // ATTN_TQ),
        in_specs=[cur(0), prev(1), cur(1), nxt(1), prev(2), cur(2), nxt(2),
                  pl.BlockSpec((HEADS_PER_GROUP, ATTN_TQ, ATTN_TK), lambda r, i: (0, 0, 0))],
        out_specs=[pl.BlockSpec((None, ATTN_TQ, GROUP_WIDTH), lambda r, i: (r, i, 0)),
                   pl.BlockSpec((None, ATTN_TQ, LSE_LANES), lambda r, i: (r, i, 0))],
        out_shape=[jax.ShapeDtypeStruct((dil, seg, GROUP_WIDTH), BF16),
                   jax.ShapeDtypeStruct((dil, seg, LSE_LANES), F32)],
        compiler_params=_params(("arbitrary", "arbitrary")),
        name=f"band_attn_d{dil}",
    )(qkv, qkv, qkv, qkv, qkv, qkv, qkv, bias)


def _merge_kernel(o0_ref, o1_ref, o2_ref, l0_ref, l1_ref, l2_ref, out_ref, buf_ref, lbuf_ref):
    tm = out_ref.shape[0]
    for gi, (o_ref, l_ref) in enumerate(((o1_ref, l1_ref), (o2_ref, l2_ref))):
        dil = o_ref.shape[0]
        for r in range(dil):
            rows = pl.ds(r, tm // dil, stride=dil)
            lbuf_ref[gi, rows, :] = l_ref[r]
            for h in range(HEADS_PER_GROUP):
                buf_ref[gi * HEADS_PER_GROUP + h, rows, :] = (
                    o_ref[r, :, h * HEAD_DIM:(h + 1) * HEAD_DIM].astype(F32))
    lse0, lse1, lse2 = l0_ref[0], lbuf_ref[0], lbuf_ref[1]
    top = jnp.maximum(jnp.maximum(lse0, lse1), lse2)
    w0, w1, w2 = jnp.exp(lse0 - top), jnp.exp(lse1 - top), jnp.exp(lse2 - top)
    den = w0 + w1 + w2
    for h in range(HEADS_PER_GROUP):
        sl = slice(h * HEAD_DIM, (h + 1) * HEAD_DIM)
        col = slice(h, h + 1)
        num = (w0[:, col] * o0_ref[0, :, sl].astype(F32) + w1[:, col] * buf_ref[h]
               + w2[:, col] * buf_ref[HEADS_PER_GROUP + h])
        out_ref[:, sl] = (num / den[:, col]).astype(out_ref.dtype)


def _merge_groups(outs, lses):
    tm = 256
    d1, d2 = ATTN_GROUPS[1][1], ATTN_GROUPS[2][1]

    def spec(dil, width):
        return pl.BlockSpec((dil, tm // dil, width), lambda i: (0, i, 0))

    return pl.pallas_call(
        _merge_kernel,
        grid=(SEQ // tm,),
        in_specs=[spec(1, GROUP_WIDTH), spec(d1, GROUP_WIDTH), spec(d2, GROUP_WIDTH),
                  spec(1, LSE_LANES), spec(d1, LSE_LANES), spec(d2, LSE_LANES)],
        out_specs=pl.BlockSpec((tm, GROUP_WIDTH), lambda i: (i, 0)),
        out_shape=jax.ShapeDtypeStruct((SEQ, GROUP_WIDTH), BF16),
        scratch_shapes=[pltpu.VMEM((2 * HEADS_PER_GROUP, tm, HEAD_DIM), F32), pltpu.VMEM((2, tm, LSE_LANES), F32)],
        compiler_params=_params(("arbitrary",)),
        name="attn_merge",
    )(*outs, *lses)


CONV_HALO = 16


def _conv_kernel(up_ref, uc_ref, un_ref, w_ref, b_ref, g_ref, o_ref, buf_ref, y_ref):
    i = pl.program_id(0)
    tm = uc_ref.shape[0]
    first = i == 0
    last = i == pl.num_programs(0) - 1
    buf_ref[0:CONV_HALO, :] = jnp.where(first, 0.0, up_ref[...].astype(F32))
    buf_ref[CONV_HALO:CONV_HALO + tm, :] = uc_ref[...].astype(F32)
    buf_ref[CONV_HALO + tm:, :] = jnp.where(last, 0.0, un_ref[...].astype(F32))
    pad = CONV_KERNEL // 2
    lanes = 256
    for c0 in range(0, CONV_WIDTH, lanes):
        cs = slice(c0, c0 + lanes)
        part = jnp.broadcast_to(b_ref[:, cs], (tm, lanes))
        first = CONV_HALO - pad
        for s in range(8):
            group = None
            for tap in range(CONV_KERNEL):
                off = first + tap
                if off % 8 != s:
                    continue
                term = w_ref[tap:tap + 1, cs] * buf_ref[pl.ds(off - s, tm + 8), cs]
                group = term if group is None else group + term
            part = part + group[s:s + tm, :]
        y_ref[:, cs] = part
    acc = y_ref[...]
    ms = jnp.mean(acc * acc, axis=-1, keepdims=True)
    y = acc * lax.rsqrt(ms + NORM_EPS) * g_ref[...]
    o_ref[...] = (y * jax.nn.sigmoid(y)).astype(o_ref.dtype)


def _conv_module(u, conv_w, conv_b, conv_norm_g):
    tm = 128
    sub = tm // CONV_HALO
    last = SEQ // CONV_HALO - 1
    return pl.pallas_call(
        _conv_kernel,
        grid=(SEQ // tm,),
        in_specs=[pl.BlockSpec((CONV_HALO, CONV_WIDTH), lambda i: (jnp.maximum(i * sub - 1, 0), 0)),
                  pl.BlockSpec((tm, CONV_WIDTH), lambda i: (i, 0)),
                  pl.BlockSpec((CONV_HALO, CONV_WIDTH), lambda i: (jnp.minimum((i + 1) * sub, last), 0)),
                  pl.BlockSpec((CONV_KERNEL, CONV_WIDTH), lambda i: (0, 0)),
                  pl.BlockSpec((1, CONV_WIDTH), lambda i: (0, 0)),
                  pl.BlockSpec((1, CONV_WIDTH), lambda i: (0, 0))],
        out_specs=pl.BlockSpec((tm, CONV_WIDTH), lambda i: (i, 0)),
        out_shape=jax.ShapeDtypeStruct((SEQ, CONV_WIDTH), BF16),
        scratch_shapes=[pltpu.VMEM((tm + 2 * CONV_HALO, CONV_WIDTH), F32), pltpu.VMEM((tm, CONV_WIDTH), F32)],
        compiler_params=_params(("arbitrary",)),
        name="conv_module",
    )(u, u, u, conv_w, conv_b.reshape(1, CONV_WIDTH), conv_norm_g.reshape(1, CONV_WIDTH))


def _branch_kernel(a_ref, u_ref, wa_ref, wc_ref, ga_ref, gc_ref, o_ref, wab_ref, wcb_ref):
    @pl.when(pl.program_id(1) == 0)
    def _():
        wab_ref[...] = wa_ref[...].astype(BF16)
        wcb_ref[...] = wc_ref[...].astype(BF16)

    attn = jnp.dot(a_ref[...], wab_ref[...], preferred_element_type=F32)
    conv = jnp.dot(u_ref[...], wcb_ref[...], preferred_element_type=F32)
    o_ref[...] = (ga_ref[...].astype(F32) * attn + gc_ref[...].astype(F32) * conv).astype(o_ref.dtype)


def _branch_merge(attn, conv, w_attn_out, w_conv_out, gates):
    tm, tn = 1024, 512
    nj = D_MODEL // tn
    return pl.pallas_call(
        _branch_kernel,
        grid=(nj, SEQ // tm),
        in_specs=[pl.BlockSpec((tm, GROUP_WIDTH), lambda j, i: (i, 0)),
                  pl.BlockSpec((tm, CONV_WIDTH), lambda j, i: (i, 0)),
                  pl.BlockSpec((GROUP_WIDTH, tn), lambda j, i: (0, j)),
                  pl.BlockSpec((CONV_WIDTH, tn), lambda j, i: (0, j)),
                  pl.BlockSpec((tm, tn), lambda j, i: (i, j)),
                  pl.BlockSpec((tm, tn), lambda j, i: (i, nj + j))],
        out_specs=pl.BlockSpec((tm, tn), lambda j, i: (i, j)),
        out_shape=jax.ShapeDtypeStruct((SEQ, D_MODEL), BF16),
        scratch_shapes=[pltpu.VMEM((GROUP_WIDTH, tn), BF16), pltpu.VMEM((CONV_WIDTH, tn), BF16)],
        compiler_params=_params(("arbitrary", "arbitrary")),
        name="branch_merge",
    )(attn, conv, w_attn_out, w_conv_out, gates, gates)


def _wo_kernel(m_ref, w_ref, x_ref, g_ref, o_ref, wb_ref):
    @pl.when(pl.program_id(1) == 0)
    def _():
        wb_ref[...] = w_ref[...].astype(BF16)

    y = jnp.dot(m_ref[...], wb_ref[...], preferred_element_type=F32)
    o_ref[...] = x_ref[...] + g_ref[...] * y


def _wo_residual(merged, w_o, x, mod):
    tm, tn = 1024, 512
    nj = D_MODEL // tn
    return pl.pallas_call(
        _wo_kernel,
        grid=(nj, SEQ // tm),
        in_specs=[pl.BlockSpec((tm, D_MODEL), lambda j, i: (i, 0)),
                  pl.BlockSpec((D_MODEL, tn), lambda j, i: (0, j)),
                  pl.BlockSpec((tm, tn), lambda j, i: (i, j)),
                  pl.BlockSpec((1, tn), lambda j, i: (0, 2 * nj + j))],
        out_specs=pl.BlockSpec((tm, tn), lambda j, i: (i, j)),
        out_shape=jax.ShapeDtypeStruct((SEQ, D_MODEL), F32),
        scratch_shapes=[pltpu.VMEM((D_MODEL, tn), BF16)],
        compiler_params=_params(("arbitrary", "arbitrary")),
        name="wo_residual",
    )(merged, w_o, x, mod)


def _router_kernel(x_ref, g_ref, sc_ref, sh_ref, wr_ref, br_ref, hp_ref, idx_ref, wt_ref):
    tm = x_ref.shape[0]
    h = _modulated_norm(x_ref[...], g_ref[...], sc_ref[...], sh_ref[...])
    hp_ref[...] = _pack_halves(h)

    h_hi = h.astype(BF16)
    h_lo = (h - h_hi.astype(F32)).astype(BF16)
    w = wr_ref[...]
    w_hi = w.astype(BF16)
    w_lo = (w - w_hi.astype(F32)).astype(BF16)
    logits = (jnp.dot(h_hi, w_hi, preferred_element_type=F32) + jnp.dot(h_lo, w_hi, preferred_element_type=F32)
              + jnp.dot(h_hi, w_lo, preferred_element_type=F32) + br_ref[...])

    lane = lax.broadcasted_iota(jnp.int32, (tm, N_EXPERTS), 1)
    work = logits
    vals, idxs = [], []
    for _ in range(TOP_K):
        m = jnp.max(work, axis=-1, keepdims=True)
        idx = jnp.min(jnp.where(work == m, lane, N_EXPERTS), axis=-1, keepdims=True)
        vals.append(m)
        idxs.append(idx)
        work = jnp.where(lane == idx, -jnp.inf, work)
    exps = [jnp.exp(v - vals[0]) for v in vals]
    den = exps[0] + exps[1] + exps[2] + exps[3]
    col = lax.broadcasted_iota(jnp.int32, (tm, TOP_K), 1)
    idx_out = jnp.zeros((tm, TOP_K), jnp.int32)
    wt_out = jnp.zeros((tm, TOP_K), F32)
    for k in range(TOP_K):
        idx_out = jnp.where(col == k, idxs[k], idx_out)
        wt_out = jnp.where(col == k, exps[k] / den, wt_out)
    idx_ref[...] = idx_out
    wt_ref[...] = wt_out


def _norm2_router(x1, g, mod, w_router, b_router):
    tm = 256
    vec = lambda k: pl.BlockSpec((1, D_MODEL), lambda i, k=k: (0, k))
    return pl.pallas_call(
        _router_kernel,
        grid=(SEQ // tm,),
        in_specs=[pl.BlockSpec((tm, D_MODEL), lambda i: (i, 0)), vec(0), vec(4), vec(3),
                  pl.BlockSpec((D_MODEL, N_EXPERTS), lambda i: (0, 0)),
                  pl.BlockSpec((1, N_EXPERTS), lambda i: (0, 0))],
        out_specs=[pl.BlockSpec((tm, HALF), lambda i: (i, 0)),
                   pl.BlockSpec((tm, TOP_K), lambda i: (i, 0)),
                   pl.BlockSpec((tm, TOP_K), lambda i: (i, 0))],
        out_shape=[jax.ShapeDtypeStruct((SEQ, HALF), U32),
                   jax.ShapeDtypeStruct((SEQ, TOP_K), jnp.int32),
                   jax.ShapeDtypeStruct((SEQ, TOP_K), F32)],
        compiler_params=_params(("arbitrary",)),
        name="norm2_router",
    )(x1, g.reshape(1, D_MODEL), mod, mod, w_router, b_router.reshape(1, N_EXPERTS))


def _routing_tables(top_idx):
    i32 = jnp.int32
    e_flat = top_idx.reshape(-1)
    onehot = (e_flat[:, None] == jnp.arange(N_EXPERTS, dtype=i32)[None, :]).astype(i32)
    csum = jnp.cumsum(onehot, axis=0)
    rank = jnp.sum(onehot * csum, axis=1) - 1
    counts = csum[-1]
    nblk = (counts + SLOT_BLOCK - 1) // SLOT_BLOCK
    blk_end = jnp.cumsum(nblk)
    blk_start = blk_end - nblk
    ntile = (nblk + BLOCKS_PER_TILE - 1) // BLOCKS_PER_TILE
    tile_end = jnp.cumsum(ntile)
    tile_start = tile_end - ntile
    n_tiles = tile_end[-1]
    n_blocks = blk_end[-1]

    dest = (jnp.sum(onehot * (tile_start * TILE_ROWS)[None, :], axis=1) + rank).astype(i32)
    packed = jnp.sum(onehot * (blk_start * SLOT_BLOCK)[None, :], axis=1) + rank
    tok = jnp.arange(N_ASSIGN, dtype=i32) // TOP_K
    slot_tok = jnp.zeros((N_SLOT_BLOCKS * SLOT_BLOCK,), i32).at[packed].set(tok, unique_indices=True)

    s_raw = jnp.arange(MAX_TILES, dtype=i32)
    s = jnp.minimum(s_raw, n_tiles - 1)
    tile_e = jnp.minimum(jnp.searchsorted(tile_end, s, side='right'), N_EXPERTS - 1).astype(i32)
    local = s - tile_start[tile_e]
    tile_nb = jnp.clip(nblk[tile_e] - BLOCKS_PER_TILE * local, 0, BLOCKS_PER_TILE)
    tile_nb = jnp.where(s_raw < n_tiles, tile_nb, 0).astype(i32)
    tile_b0 = blk_start[tile_e] + BLOCKS_PER_TILE * local
    tile_src = s.astype(i32)
    rows = tile_b0[:, None] * SLOT_BLOCK + jnp.arange(TILE_ROWS, dtype=i32)[None, :]
    tile_tok = slot_tok[jnp.minimum(rows, N_SLOT_BLOCKS * SLOT_BLOCK - 1)].reshape(MAX_TILES, 1, TILE_ROWS)
    return dest, tile_tok, tile_e, tile_src, tile_nb, n_tiles.astype(i32)


UP_TF = 256
UP_ROW_CHUNK = 256


def _deinterleave_matrix():
    r = lax.broadcasted_iota(jnp.int32, (256, 256), 0)
    c = lax.broadcasted_iota(jnp.int32, (256, 256), 1)
    src = jnp.where(c < 128, 2 * c, 2 * (c - 128) + 1)
    return jnp.where(r == src, 1.0, 0.0).astype(BF16)


def _up_kernel(e_ref, nb_ref, first_ref, next_ref, h_hbm, w_ref, b_ref, o_ref, wb_ref, xb_ref, g_ref, sem):
    s = pl.program_id(0)
    j = pl.program_id(1)
    nb = nb_ref[s]

    def row_copy(r, tok):
        return pltpu.make_async_copy(h_hbm.at[pl.ds(tok, 1), :], g_ref.at[pl.ds(r, 1), :], sem)

    def request_rows(table_ref, first, count):
        def issue(i, carry):
            for u in range(2):
                r = first + 2 * i + u
                row_copy(r, table_ref[0, 0, r]).start(priority=u)
            return carry
        lax.fori_loop(0, count // 2, issue, 0, unroll=4)

    def request_chunk(table_ref, c):
        request_rows(table_ref, c * GATHER_CHUNK, GATHER_CHUNK)

    def chunks_of(blocks):
        return (blocks * SLOT_BLOCK + GATHER_CHUNK - 1) // GATHER_CHUNK

    @pl.when((s == 0) & (j == 0))
    def _():
        for c in range(CHUNKS_PER_TILE):
            @pl.when(c < chunks_of(nb))
            def _(c=c):
                request_chunk(first_ref, c)

    @pl.when(j == 0)
    def _():
        def drain_row(r, carry):
            row_copy(0, 0).wait()
            return carry

        def drain_chunk(c, carry):
            return lax.fori_loop(0, GATHER_CHUNK, drain_row, carry, unroll=8)

        lax.fori_loop(0, chunks_of(nb), drain_chunk, 0)

        for c in range(CHUNKS_PER_TILE):
            rows = slice(c * GATHER_CHUNK, (c + 1) * GATHER_CHUNK)

            @pl.when(c < chunks_of(nb))
            def _(rows=rows):
                lo, hi = _unpack_halves(g_ref[rows, :])
                xb_ref[rows, :HALF] = lo.astype(BF16)
                xb_ref[rows, HALF:] = hi.astype(BF16)

            @pl.when(c >= chunks_of(nb))
            def _(rows=rows):
                xb_ref[rows, :] = jnp.zeros((GATHER_CHUNK, D_MODEL), BF16)

    nxt = jnp.minimum(s + 1, MAX_TILES - 1)

    @pl.when((j >= 1) & (s + 1 < pl.num_programs(0)) & (j - 1 < chunks_of(nb_ref[nxt])))
    def _():
        request_chunk(next_ref, j - 1)

    def run(n_rows):
        if n_rows == 0:
            o_ref[...] = jnp.zeros((TILE_ROWS, UP_TF), BF16)
            return
        wb_ref[...] = w_ref[...].astype(BF16)
        perm = _deinterleave_matrix()
        for m0 in range(0, n_rows, UP_ROW_CHUNK):
            rows = slice(m0, m0 + UP_ROW_CHUNK)
            gu = jnp.dot(xb_ref[rows, :], wb_ref[...], preferred_element_type=F32) + b_ref[...]
            for c in range(2 * UP_TF // 256):
                gl = jnp.dot(gu[:, c * 256:(c + 1) * 256].astype(BF16), perm, preferred_element_type=F32)
                glu = jnp.minimum(gl[:, :128], SWIGLU_LIMIT)
                lin = jnp.clip(gl[:, 128:], -SWIGLU_LIMIT, SWIGLU_LIMIT)
                act = glu * jax.nn.sigmoid(SWIGLU_ALPHA * glu) * (lin + 1.0)
                o_ref[rows, c * 128:(c + 1) * 128] = act.astype(BF16)
        if n_rows < TILE_ROWS:
            o_ref[n_rows:, :] = jnp.zeros((TILE_ROWS - n_rows, UP_TF), BF16)

    _run_row_variant(nb, run, UP_ROW_VARIANTS)


def _run_row_variant(nb, run, variants):
    assert variants[-1] == BLOCKS_PER_TILE
    lo = 0
    for blocks in variants:
        @pl.when((nb > lo) & (nb <= blocks))
        def _(blocks=blocks):
            run(blocks * SLOT_BLOCK)
        lo = blocks

    @pl.when(nb == 0)
    def _():
        run(0)


def _expert_up(hp, tile_tok, w_up, b_up, tile_e, tile_nb, n_tiles):
    nj = EXPERT_FF // UP_TF
    assert CHUNKS_PER_TILE <= nj - 1
    assert CHUNKS_PER_TILE * GATHER_CHUNK == TILE_ROWS

    def w_index(s, j, e, nb):
        return (e[s], 0, jnp.where(nb[s] > 0, j, nj - 1))

    grid_spec = pltpu.PrefetchScalarGridSpec(
        num_scalar_prefetch=2,
        grid=(n_tiles, nj),
        in_specs=[pl.BlockSpec((1, 1, TILE_ROWS), lambda s, j, e, nb: (0, 0, 0), memory_space=pltpu.SMEM),
                  pl.BlockSpec((1, 1, TILE_ROWS), lambda s, j, e, nb: (jnp.minimum(s + 1, MAX_TILES - 1), 0, 0),
                               memory_space=pltpu.SMEM),
                  pl.BlockSpec(memory_space=pl.ANY),
                  pl.BlockSpec((None, D_MODEL, 2 * UP_TF), w_index),
                  pl.BlockSpec((None, 1, 2 * UP_TF), w_index)],
        out_specs=pl.BlockSpec((TILE_ROWS, UP_TF), lambda s, j, e, nb: (s, j)),
        scratch_shapes=[pltpu.VMEM((D_MODEL, 2 * UP_TF), BF16),
                        pltpu.VMEM((TILE_ROWS, D_MODEL), BF16),
                        pltpu.VMEM((TILE_ROWS, HALF), U32),
                        pltpu.SemaphoreType.DMA(())],
    )
    return pl.pallas_call(
        _up_kernel,
        grid_spec=grid_spec,
        out_shape=jax.ShapeDtypeStruct((n_tiles * TILE_ROWS, EXPERT_FF), BF16),
        compiler_params=_params(("arbitrary", "arbitrary")),
        name="expert_up",
    )(tile_e, tile_nb, tile_tok, tile_tok, hp, w_up, b_up.reshape(N_EXPERTS, 1, 2 * EXPERT_FF))


DOWN_TN = TOKEN_TILE * 128


def _down_kernel(e_ref, src_ref, nb_ref, a_ref, wl_ref, wh_ref, bl_ref, bh_ref, o_ref, wlb_ref, whb_ref):
    nb = nb_ref[pl.program_id(0)]

    def run(n_rows):
        if n_rows == 0:
            o_ref[...] = jnp.zeros((TILE_ROWS * TOKEN_TILE, 128), U32)
            return
        wlb_ref[...] = wl_ref[...].astype(BF16)
        whb_ref[...] = wh_ref[...].astype(BF16)
        for m0 in range(0, n_rows, UP_ROW_CHUNK):
            a = a_ref[m0:m0 + UP_ROW_CHUNK, :]
            y_lo = jnp.dot(a, wlb_ref[...], preferred_element_type=F32) + bl_ref[...]
            y_hi = jnp.dot(a, whb_ref[...], preferred_element_type=F32) + bh_ref[...]
            packed = _pack_halves(jnp.concatenate([y_lo, y_hi], axis=1))
            for c in range(TOKEN_TILE):
                o_ref[pl.ds(m0 * TOKEN_TILE + c, UP_ROW_CHUNK, stride=TOKEN_TILE), :] = (
                    packed[:, c * 128:(c + 1) * 128])
        if n_rows < TILE_ROWS:
            o_ref[n_rows * TOKEN_TILE:, :] = jnp.zeros(((TILE_ROWS - n_rows) * TOKEN_TILE, 128), U32)

    _run_row_variant(nb, run, DOWN_ROW_VARIANTS)


def _expert_down(act, w_down, b_down, tile_e, tile_src, tile_nb, n_tiles):
    nj = HALF // DOWN_TN

    def w_index(hi):
        return lambda s, j, e, src, nb: (e[s], 0, hi * nj + jnp.where(nb[s] > 0, j, nj - 1))

    grid_spec = pltpu.PrefetchScalarGridSpec(
        num_scalar_prefetch=3,
        grid=(n_tiles, nj),
        in_specs=[pl.BlockSpec((TILE_ROWS, EXPERT_FF), lambda s, j, e, src, nb: (src[s], 0)),
                  pl.BlockSpec((None, EXPERT_FF, DOWN_TN), w_index(0)),
                  pl.BlockSpec((None, EXPERT_FF, DOWN_TN), w_index(1)),
                  pl.BlockSpec((None, 1, DOWN_TN), w_index(0)),
                  pl.BlockSpec((None, 1, DOWN_TN), w_index(1))],
        out_specs=pl.BlockSpec((None, TILE_ROWS * TOKEN_TILE, 128), lambda s, j, e, src, nb: (j, s, 0)),
        scratch_shapes=[pltpu.VMEM((EXPERT_FF, DOWN_TN), BF16), pltpu.VMEM((EXPERT_FF, DOWN_TN), BF16)],
    )
    b3 = b_down.reshape(N_EXPERTS, 1, D_MODEL)
    return pl.pallas_call(
        _down_kernel,
        grid_spec=grid_spec,
        out_shape=jax.ShapeDtypeStruct((2, n_tiles * TILE_ROWS * TOKEN_TILE, 128), U32),
        compiler_params=_params(("arbitrary", "arbitrary")),
        name="expert_down",
    )(tile_e, tile_src, tile_nb, act, w_down, w_down, b3, b3)


COMBINE_TM = 64


def _combine_kernel(dest_ref, next_ref, y_hbm, x_ref, wt_ref, g_ref, o_ref, buf_ref, sems):
    i = pl.program_id(0)
    cur = i % 2

    def tile_copy(buf, t, k, half, slot):
        src = y_hbm.at[half, pl.ds(pl.multiple_of(slot * TOKEN_TILE, TOKEN_TILE), TOKEN_TILE), :]
        dst = buf_ref.at[buf, k, half, pl.ds(pl.multiple_of(t * TOKEN_TILE, TOKEN_TILE), TOKEN_TILE), :]
        return pltpu.make_async_copy(src, dst, sems.at[buf])

    def issue_all(table_ref, buf):
        def issue(t, carry):
            for k in range(TOP_K):
                slot = table_ref[0, 0, t * TOP_K + k]
                tile_copy(buf, t, k, 0, slot).start(priority=0)
                tile_copy(buf, t, k, 1, slot).start(priority=1)
            return carry
        lax.fori_loop(0, COMBINE_TM, issue, 0, unroll=4)

    def drain(t, carry):
        for k in range(TOP_K):
            tile_copy(cur, t, k, 0, 0).wait()
            tile_copy(cur, t, k, 1, 0).wait()
        return carry

    @pl.when(i == 0)
    def _():
        issue_all(dest_ref, 0)

    @pl.when(i + 1 < pl.num_programs(0))
    def _():
        issue_all(next_ref, 1 - cur)

    lax.fori_loop(0, COMBINE_TM, drain, 0, unroll=4)

    weights = [wt_ref[:, k:k + 1] for k in range(TOP_K)]
    for half in range(2):
        for c in range(TOKEN_TILE):
            col = (half * TOKEN_TILE + c) * 128
            acc_lo = jnp.zeros((COMBINE_TM, 128), F32)
            acc_hi = jnp.zeros((COMBINE_TM, 128), F32)
            for k in range(TOP_K):
                lo, hi = _unpack_halves(_load_token_tile_column(buf_ref, (cur, k), 0, COMBINE_TM, half, c))
                acc_lo = acc_lo + weights[k] * lo
                acc_hi = acc_hi + weights[k] * hi
            o_ref[:, col:col + 128] = x_ref[:, col:col + 128] + g_ref[:, col:col + 128] * acc_lo
            hcol = HALF + col
            o_ref[:, hcol:hcol + 128] = x_ref[:, hcol:hcol + 128] + g_ref[:, hcol:hcol + 128] * acc_hi


def _combine(y_slots, dest, x1, top_w, mod):
    nt = SEQ // COMBINE_TM
    return pl.pallas_call(
        _combine_kernel,
        grid=(nt,),
        in_specs=[pl.BlockSpec((1, 1, COMBINE_TM * TOP_K), lambda i: (i, 0, 0), memory_space=pltpu.SMEM),
                  pl.BlockSpec((1, 1, COMBINE_TM * TOP_K), lambda i: (jnp.minimum(i + 1, nt - 1), 0, 0),
                               memory_space=pltpu.SMEM),
                  pl.BlockSpec(memory_space=pl.ANY),
                  pl.BlockSpec((COMBINE_TM, D_MODEL), lambda i: (i, 0)),
                  pl.BlockSpec((COMBINE_TM, TOP_K), lambda i: (i, 0)),
                  pl.BlockSpec((1, D_MODEL), lambda i: (0, 5))],
        out_specs=pl.BlockSpec((COMBINE_TM, D_MODEL), lambda i: (i, 0)),
        out_shape=jax.ShapeDtypeStruct((SEQ, D_MODEL), F32),
        scratch_shapes=[pltpu.VMEM((2, TOP_K, 2, COMBINE_TM * TOKEN_TILE, 128), U32),
                        pltpu.SemaphoreType.DMA((2,))],
        compiler_params=_params(("arbitrary",)),
        name="moe_combine",
    )(dest.reshape(nt, 1, COMBINE_TM * TOP_K), dest.reshape(nt, 1, COMBINE_TM * TOP_K), y_slots, x1, top_w, mod)


def _layer(x, c, norm1_g, norm2_g, w_ada, b_ada, w_in, q_norm_g, k_norm_g, rel_bias, conv_w, conv_b,
           conv_norm_g, w_attn_out, w_conv_out, w_o, w_router, b_router, w_up, b_up, w_down, b_down):
    mod = _ada(c, w_ada, b_ada)
    h = _norm1(x, norm1_g, mod)

    outs, lses = [], []
    for group, (_, dil) in enumerate(ATTN_GROUPS):
        qkv = _qkv_proj(h, w_in, q_norm_g, k_norm_g, group, dil)
        o, lse = _band_attention(qkv, _bias_tiles(rel_bias, group, dil), dil)
        outs.append(o)
        lses.append(lse)
    attn = _merge_groups(outs, lses)

    u = _conv_module(_conv_glu_proj(h, w_in), conv_w, conv_b, conv_norm_g)
    gates = _gate_proj(h, w_in)
    merged = _branch_merge(attn, u, w_attn_out, w_conv_out, gates)
    x1 = _wo_residual(merged, w_o, x, mod)

    hp, top_idx, top_w = _norm2_router(x1, norm2_g, mod, w_router, b_router)
    dest, tile_tok, tile_e, tile_src, tile_nb, n_tiles = _routing_tables(top_idx)

    def experts(grid_tiles):
        def run():
            act = _expert_up(hp, tile_tok, w_up, b_up, tile_e, tile_nb, grid_tiles)
            y_slots = _expert_down(act, w_down, b_down, tile_e, tile_src, tile_nb, grid_tiles)
            return _combine(y_slots, dest, x1, top_w, mod)
        return run

    return lax.cond(n_tiles <= BALANCED_TILES, experts(BALANCED_TILES), experts(MAX_TILES))


def kernel(x, c, norm1_g, norm2_g, w_ada, b_ada, w_in, q_norm_g, k_norm_g, rel_bias, conv_w, conv_b,
           conv_norm_g, w_attn_out, w_conv_out, w_o, w_router, b_router, w_up, b_up, w_down, b_down):
    batch = x.shape[0]
    xs = x.reshape(batch * SEQ, D_MODEL)
    for layer in range(w_in.shape[0]):
        xs = _layer(xs, c, norm1_g[layer], norm2_g[layer], w_ada[layer], b_ada[layer], w_in[layer],
                    q_norm_g[layer], k_norm_g[layer], rel_bias, conv_w[layer], conv_b[layer],
                    conv_norm_g[layer], w_attn_out[layer], w_conv_out[layer], w_o[layer],
                    w_router[layer], b_router[layer], w_up[layer], b_up[layer], w_down[layer], b_down[layer])
    return xs.reshape(x.shape)
```

```python
import functools
import math

import numpy as np
import jax
import jax.numpy as jnp
from jax import lax
from jax.experimental import pallas as pl
from jax.experimental.pallas import tpu as pltpu

D_MODEL = 4096
SEQ = 8192
HEAD_DIM = 128
ATTN_GROUPS = ((128, 1), (512, 4), (2048, 16))
N_GROUPS = len(ATTN_GROUPS)
HEADS_PER_GROUP = 8
GROUP_WIDTH = HEADS_PER_GROUP * HEAD_DIM
ATTN_WIDTH = N_GROUPS * GROUP_WIDTH
CONV_WIDTH = D_MODEL // 2
CONV_KERNEL = 31
CONV_OFF = 3 * ATTN_WIDTH
GATE_OFF = CONV_OFF + 2 * CONV_WIDTH
IN_WIDTH = GATE_OFF + 2 * D_MODEL
REL_BUCKETS = 32
REL_MAX_DISTANCE = 1024
N_EXPERTS = 32
TOP_K = 4
EXPERT_FF = 1536
SWIGLU_ALPHA = 1.702
SWIGLU_LIMIT = 7.0
NORM_EPS = 1e-6
NEG_INF = -1e30
N_SIDE = 64

V7X_VMEM_LIMIT = 56 * 1024 * 1024

SLOT_BLOCK = 256
BLOCKS_PER_TILE = 5
N_ASSIGN = SEQ * TOP_K
TILE_ROWS = SLOT_BLOCK * BLOCKS_PER_TILE
UP_ROW_VARIANTS = (2, 3, 4, 5)
DOWN_ROW_VARIANTS = (2, 4, 5)
GATHER_CHUNK = 256
CHUNKS_PER_TILE = TILE_ROWS // GATHER_CHUNK
N_SLOT_BLOCKS = (N_ASSIGN + N_EXPERTS * (SLOT_BLOCK - 1)) // SLOT_BLOCK + 1
MAX_TILES = (N_SLOT_BLOCKS + N_EXPERTS * (BLOCKS_PER_TILE - 1)) // BLOCKS_PER_TILE
BALANCED_TILES = (MAX_TILES + 2 * N_EXPERTS) // 3 + 2
HALF = D_MODEL // 2

F32 = jnp.float32
BF16 = jnp.bfloat16
U32 = jnp.uint32


def _params(sem, vmem=V7X_VMEM_LIMIT):
    return pltpu.CompilerParams(dimension_semantics=sem, vmem_limit_bytes=vmem)


def _pack_halves(v):
    n = v.shape[-1] // 2
    lo = pltpu.bitcast(v[:, :n].astype(BF16).astype(F32), U32) >> 16
    hi = pltpu.bitcast(v[:, n:].astype(BF16).astype(F32), U32) & jnp.uint32(0xFFFF0000)
    return lo | hi


def _unpack_halves(p):
    lo = pltpu.bitcast(p << 16, F32)
    hi = pltpu.bitcast(p & jnp.uint32(0xFFFF0000), F32)
    return lo, hi


TOKEN_TILE = 8


def _store_token_tiles(ref, row0, rows, packed):
    for half in range(2):
        for c in range(TOKEN_TILE):
            col = (half * TOKEN_TILE + c) * 128
            ref[half, pl.ds(row0 * TOKEN_TILE + c, rows, stride=TOKEN_TILE), :] = packed[:, col:col + 128]


def _load_token_tile_column(ref, lead, row0, rows, half, c):
    return ref[(*lead, half, pl.ds(row0 * TOKEN_TILE + c, rows, stride=TOKEN_TILE), slice(None))]


def _ada_kernel(c_ref, w_ref, b_ref, o_ref, cs_ref):
    @pl.when(pl.program_id(0) == 0)
    def _():
        c = c_ref[...]
        cs_ref[...] = c * jax.nn.sigmoid(c)

    tn = o_ref.shape[-1]
    rows_per_step = 64

    def body(r, acc):
        rows = pl.ds(pl.multiple_of(r * rows_per_step, rows_per_step), rows_per_step)
        cs = cs_ref[rows, :]
        prod = w_ref[rows, :] * jnp.concatenate([cs] * (tn // 128), axis=1)
        return acc + prod.reshape(rows_per_step // 8, 8, tn).sum(axis=0)

    acc = lax.fori_loop(0, D_MODEL // rows_per_step, body, jnp.zeros((8, tn), F32))
    o_ref[...] = jnp.sum(acc, axis=0, keepdims=True) + b_ref[...]


def _ada(c, w_ada, b_ada):
    tn = 512
    n = w_ada.shape[-1]
    c_lanes = jnp.broadcast_to(c.reshape(D_MODEL, 1), (D_MODEL, 128))
    return pl.pallas_call(
        _ada_kernel,
        grid=(n // tn,),
        in_specs=[pl.BlockSpec((D_MODEL, 128), lambda j: (0, 0)),
                  pl.BlockSpec((D_MODEL, tn), lambda j: (0, j)),
                  pl.BlockSpec((1, tn), lambda j: (0, j))],
        out_specs=pl.BlockSpec((1, tn), lambda j: (0, j)),
        out_shape=jax.ShapeDtypeStruct((1, n), F32),
        scratch_shapes=[pltpu.VMEM((D_MODEL, 128), F32)],
        compiler_params=_params(("arbitrary",)),
        name="ada_mod",
    )(c_lanes, w_ada, b_ada.reshape(1, n))


def _modulated_norm(x, g, scale, shift):
    ms = jnp.mean(x * x, axis=-1, keepdims=True)
    y = x * lax.rsqrt(ms + NORM_EPS) * g
    return y * (1.0 + scale) + shift


def _norm1_kernel(x_ref, g_ref, sc_ref, sh_ref, o_ref):
    o_ref[...] = _modulated_norm(x_ref[...], g_ref[...], sc_ref[...], sh_ref[...]).astype(o_ref.dtype)


def _norm1(x, g, mod):
    tm = 256
    vec = lambda k: pl.BlockSpec((1, D_MODEL), lambda i, k=k: (0, k))
    return pl.pallas_call(
        _norm1_kernel,
        grid=(SEQ // tm,),
        in_specs=[pl.BlockSpec((tm, D_MODEL), lambda i: (i, 0)), vec(0), vec(1), vec(0)],
        out_specs=pl.BlockSpec((tm, D_MODEL), lambda i: (i, 0)),
        out_shape=jax.ShapeDtypeStruct((SEQ, D_MODEL), BF16),
        compiler_params=_params(("arbitrary",)),
        name="norm1",
    )(x, g.reshape(1, D_MODEL), mod, mod)


def _qkv_kernel(h_ref, w_ref, gq_ref, gk_ref, o_ref, wb_ref, acc_ref, *, dil):
    which = pl.program_id(0)
    tm = h_ref.shape[0]

    @pl.when(pl.program_id(1) == 0)
    def _():
        wb_ref[...] = w_ref[...].astype(BF16)

    acc = jnp.dot(h_ref[...], wb_ref[...], preferred_element_type=F32)

    is_v = which == 2
    gain = jnp.where(is_v, 1.0, jnp.where(which == 0, gq_ref[...] * HEAD_DIM ** -0.5, gk_ref[...]))
    for h in range(HEADS_PER_GROUP):
        sl = slice(h * HEAD_DIM, (h + 1) * HEAD_DIM)
        blk = acc[:, sl]
        ms = jnp.mean(blk * blk, axis=-1, keepdims=True)
        vals = blk * jnp.where(is_v, 1.0, lax.rsqrt(ms + NORM_EPS)) * gain
        if dil == 1:
            o_ref[0, :, sl] = vals.astype(BF16)
        else:
            acc_ref[h] = vals
            for r in range(dil):
                o_ref[r, :, sl] = acc_ref[h, pl.ds(r, tm // dil, stride=dil), :].astype(BF16)


def _qkv_proj(h, w_in, gq, gk, group, dil):
    tm = 512
    seg = SEQ // dil
    return pl.pallas_call(
        functools.partial(_qkv_kernel, dil=dil),
        grid=(3, SEQ // tm),
        in_specs=[pl.BlockSpec((tm, D_MODEL), lambda w, i: (i, 0)),
                  pl.BlockSpec((D_MODEL, GROUP_WIDTH), lambda w, i: (0, w * N_GROUPS + group),
                               pipeline_mode=pl.Buffered(1)),
                  pl.BlockSpec((1, HEAD_DIM), lambda w, i: (0, 0)),
                  pl.BlockSpec((1, HEAD_DIM), lambda w, i: (0, 0))],
        out_specs=pl.BlockSpec((None, dil, tm // dil, GROUP_WIDTH), lambda w, i: (w, 0, i, 0)),
        out_shape=jax.ShapeDtypeStruct((3, dil, seg, GROUP_WIDTH), BF16),
        scratch_shapes=[pltpu.VMEM((D_MODEL, GROUP_WIDTH), BF16), pltpu.VMEM((HEADS_PER_GROUP, tm, HEAD_DIM), F32)],
        compiler_params=_params(("arbitrary", "arbitrary")),
        name=f"qkv_proj_g{group}",
    )(h, w_in, gq.reshape(1, HEAD_DIM), gk.reshape(1, HEAD_DIM))


def _glu_kernel(h_ref, wa_ref, wg_ref, o_ref, wab_ref, wgb_ref):
    @pl.when(pl.program_id(1) == 0)
    def _():
        wab_ref[...] = wa_ref[...].astype(BF16)
        wgb_ref[...] = wg_ref[...].astype(BF16)

    h = h_ref[...]
    a = jnp.dot(h, wab_ref[...], preferred_element_type=F32)
    g = jnp.dot(h, wgb_ref[...], preferred_element_type=F32)
    o_ref[...] = (a * jax.nn.sigmoid(g)).astype(o_ref.dtype)


def _conv_glu_proj(h, w_in):
    tm, tn = 512, 512
    a0 = CONV_OFF // tn
    g0 = (CONV_OFF + CONV_WIDTH) // tn
    return pl.pallas_call(
        _glu_kernel,
        grid=(CONV_WIDTH // tn, SEQ // tm),
        in_specs=[pl.BlockSpec((tm, D_MODEL), lambda j, i: (i, 0)),
                  pl.BlockSpec((D_MODEL, tn), lambda j, i: (0, a0 + j)),
                  pl.BlockSpec((D_MODEL, tn), lambda j, i: (0, g0 + j))],
        out_specs=pl.BlockSpec((tm, tn), lambda j, i: (i, j)),
        out_shape=jax.ShapeDtypeStruct((SEQ, CONV_WIDTH), BF16),
        scratch_shapes=[pltpu.VMEM((D_MODEL, tn), BF16), pltpu.VMEM((D_MODEL, tn), BF16)],
        compiler_params=_params(("arbitrary", "arbitrary")),
        name="conv_glu_proj",
    )(h, w_in, w_in)


def _gate_kernel(h_ref, w_ref, o_ref, wb_ref):
    @pl.when(pl.program_id(1) == 0)
    def _():
        wb_ref[...] = w_ref[...].astype(BF16)

    acc = jnp.dot(h_ref[...], wb_ref[...], preferred_element_type=F32)
    o_ref[...] = jax.nn.sigmoid(acc).astype(o_ref.dtype)


def _gate_proj(h, w_in):
    tm, tn = 1024, 512
    c0 = GATE_OFF // tn
    return pl.pallas_call(
        _gate_kernel,
        grid=(2 * D_MODEL // tn, SEQ // tm),
        in_specs=[pl.BlockSpec((tm, D_MODEL), lambda j, i: (i, 0)),
                  pl.BlockSpec((D_MODEL, tn), lambda j, i: (0, c0 + j))],
        out_specs=pl.BlockSpec((tm, tn), lambda j, i: (i, j)),
        out_shape=jax.ShapeDtypeStruct((SEQ, 2 * D_MODEL), BF16),
        scratch_shapes=[pltpu.VMEM((D_MODEL, tn), BF16)],
        compiler_params=_params(("arbitrary", "arbitrary")),
        name="gate_proj",
    )(h, w_in)


ATTN_TQ = 128
ATTN_TK = ATTN_TQ + 2 * N_SIDE
LSE_LANES = 128


def _bucket_tile(dil):
    a = np.arange(ATTN_TQ)[:, None]
    j = np.arange(ATTN_TK)[None, :]
    steps = j - N_SIDE - a
    rel = steps * dil
    nb = REL_BUCKETS // 2
    max_exact = nb // 2
    n = np.abs(rel)
    side = np.where(rel > 0, nb, 0)
    nf = np.maximum(n, 1).astype(np.float32)
    large = max_exact + (np.log(nf / np.float32(max_exact)) / np.float32(math.log(REL_MAX_DISTANCE / max_exact))
                         * np.float32(nb - max_exact)).astype(np.int32)
    large = np.minimum(large, nb - 1)
    bucket = side + np.where(n < max_exact, n, large)
    return np.where(np.abs(steps) <= N_SIDE, bucket, -1).astype(np.int32)


def _bias_kernel(tbl_ref, idx_ref, o_ref, *, group):
    h = pl.program_id(0)
    idx = idx_ref[...]
    acc = jnp.full(idx.shape, NEG_INF, F32)
    for b in range(REL_BUCKETS):
        acc = jnp.where(idx == b, tbl_ref[b, group * HEADS_PER_GROUP + h], acc)
    o_ref[...] = acc


def _bias_tiles(rel_bias, group, dil):
    return pl.pallas_call(
        functools.partial(_bias_kernel, group=group),
        grid=(HEADS_PER_GROUP,),
        in_specs=[pl.BlockSpec(memory_space=pltpu.SMEM),
                  pl.BlockSpec((ATTN_TQ, ATTN_TK), lambda h: (0, 0))],
        out_specs=pl.BlockSpec((None, ATTN_TQ, ATTN_TK), lambda h: (h, 0, 0)),
        out_shape=jax.ShapeDtypeStruct((HEADS_PER_GROUP, ATTN_TQ, ATTN_TK), F32),
        compiler_params=_params(("arbitrary",)),
        name=f"attn_bias_g{group}",
    )(rel_bias, jnp.asarray(_bucket_tile(dil)))


def _attn_kernel(q_ref, kp_ref, kc_ref, kn_ref, vp_ref, vc_ref, vn_ref, bias_ref, o_ref, lse_ref, *, seg):
    i = pl.program_id(1)
    q = q_ref[...]
    k = jnp.concatenate([kp_ref[...], kc_ref[...], kn_ref[...]], axis=0)
    v = jnp.concatenate([vp_ref[...], vc_ref[...], vn_ref[...]], axis=0)
    kpos = i * ATTN_TQ - N_SIDE + lax.broadcasted_iota(jnp.int32, (1, ATTN_TK), 1)
    valid = (kpos >= 0) & (kpos < seg)
    lane = lax.broadcasted_iota(jnp.int32, (ATTN_TQ, LSE_LANES), 1)
    lse_all = jnp.zeros((ATTN_TQ, LSE_LANES), F32)
    for h in range(HEADS_PER_GROUP):
        sl = slice(h * HEAD_DIM, (h + 1) * HEAD_DIM)
        s = lax.dot_general(q[:, sl], k[:, sl], (((1,), (1,)), ((), ())), preferred_element_type=F32)
        s = jnp.where(valid, s + bias_ref[h], NEG_INF)
        m = jnp.max(s, axis=-1, keepdims=True)
        p = jnp.exp(s - m)
        l = jnp.sum(p, axis=-1, keepdims=True)
        o = jnp.dot(p.astype(BF16), v[:, sl], preferred_element_type=F32)
        o_ref[:, sl] = (o / l).astype(o_ref.dtype)
        lse_all = jnp.where(lane == h, m + jnp.log(l), lse_all)
    lse_ref[...] = lse_all


def _band_attention(qkv, bias, dil):
    seg = SEQ // dil
    sub = ATTN_TQ // N_SIDE
    last = seg // N_SIDE - 1

    def cur(which):
        return pl.BlockSpec((None, None, ATTN_TQ, GROUP_WIDTH), lambda r, i, w=which: (w, r, i, 0))

    def prev(which):
        return pl.BlockSpec((None, None, N_SIDE, GROUP_WIDTH),
                            lambda r, i, w=which: (w, r, jnp.maximum(i * sub - 1, 0), 0))

    def nxt(which):
        return pl.BlockSpec((None, None, N_SIDE, GROUP_WIDTH),
                            lambda r, i, w=which: (w, r, jnp.minimum((i + 1) * sub, last), 0))

    return pl.pallas_call(
        functools.partial(_attn_kernel, seg=seg),
        grid=(dil, seg // ATTN_TQ),
        in_specs=[cur(0), prev(1), cur(1), nxt(1), prev(2), cur(2), nxt(2),
                  pl.BlockSpec((HEADS_PER_GROUP, ATTN_TQ, ATTN_TK), lambda r, i: (0, 0, 0))],
        out_specs=[pl.BlockSpec((None, ATTN_TQ, GROUP_WIDTH), lambda r, i: (r, i, 0)),
                   pl.BlockSpec((None, ATTN_TQ, LSE_LANES), lambda r, i: (r, i, 0))],
        out_shape=[jax.ShapeDtypeStruct((dil, seg, GROUP_WIDTH), BF16),
                   jax.ShapeDtypeStruct((dil, seg, LSE_LANES), F32)],
        compiler_params=_params(("arbitrary", "arbitrary")),
        name=f"band_attn_d{dil}",
    )(qkv, qkv, qkv, qkv, qkv, qkv, qkv, bias)


def _merge_kernel(o0_ref, o1_ref, o2_ref, l0_ref, l1_ref, l2_ref, out_ref, buf_ref, lbuf_ref):
    tm = out_ref.shape[0]
    for gi, (o_ref, l_ref) in enumerate(((o1_ref, l1_ref), (o2_ref, l2_ref))):
        dil = o_ref.shape[0]
        for r in range(dil):
            rows = pl.ds(r, tm // dil, stride=dil)
            lbuf_ref[gi, rows, :] = l_ref[r]
            for h in range(HEADS_PER_GROUP):
                buf_ref[gi * HEADS_PER_GROUP + h, rows, :] = (
                    o_ref[r, :, h * HEAD_DIM:(h + 1) * HEAD_DIM].astype(F32))
    lse0, lse1, lse2 = l0_ref[0], lbuf_ref[0], lbuf_ref[1]
    top = jnp.maximum(jnp.maximum(lse0, lse1), lse2)
    w0, w1, w2 = jnp.exp(lse0 - top), jnp.exp(lse1 - top), jnp.exp(lse2 - top)
    den = w0 + w1 + w2
    for h in range(HEADS_PER_GROUP):
        sl = slice(h * HEAD_DIM, (h + 1) * HEAD_DIM)
        col = slice(h, h + 1)
        num = (w0[:, col] * o0_ref[0, :, sl].astype(F32) + w1[:, col] * buf_ref[h]
               + w2[:, col] * buf_ref[HEADS_PER_GROUP + h])
        out_ref[:, sl] = (num / den[:, col]).astype(out_ref.dtype)


def _merge_groups(outs, lses):
    tm = 256
    d1, d2 = ATTN_GROUPS[1][1], ATTN_GROUPS[2][1]

    def spec(dil, width):
        return pl.BlockSpec((dil, tm // dil, width), lambda i: (0, i, 0))

    return pl.pallas_call(
        _merge_kernel,
        grid=(SEQ // tm,),
        in_specs=[spec(1, GROUP_WIDTH), spec(d1, GROUP_WIDTH), spec(d2, GROUP_WIDTH),
                  spec(1, LSE_LANES), spec(d1, LSE_LANES), spec(d2, LSE_LANES)],
        out_specs=pl.BlockSpec((tm, GROUP_WIDTH), lambda i: (i, 0)),
        out_shape=jax.ShapeDtypeStruct((SEQ, GROUP_WIDTH), BF16),
        scratch_shapes=[pltpu.VMEM((2 * HEADS_PER_GROUP, tm, HEAD_DIM), F32), pltpu.VMEM((2, tm, LSE_LANES), F32)],
        compiler_params=_params(("arbitrary",)),
        name="attn_merge",
    )(*outs, *lses)


CONV_HALO = 16


def _conv_kernel(up_ref, uc_ref, un_ref, w_ref, b_ref, g_ref, o_ref, buf_ref, y_ref):
    i = pl.program_id(0)
    tm = uc_ref.shape[0]
    first = i == 0
    last = i == pl.num_programs(0) - 1
    buf_ref[0:CONV_HALO, :] = jnp.where(first, 0.0, up_ref[...].astype(F32))
    buf_ref[CONV_HALO:CONV_HALO + tm, :] = uc_ref[...].astype(F32)
    buf_ref[CONV_HALO + tm:, :] = jnp.where(last, 0.0, un_ref[...].astype(F32))
    pad = CONV_KERNEL // 2
    lanes = 256
    for c0 in range(0, CONV_WIDTH, lanes):
        cs = slice(c0, c0 + lanes)
        part = jnp.broadcast_to(b_ref[:, cs], (tm, lanes))
        first = CONV_HALO - pad
        for s in range(8):
            group = None
            for tap in range(CONV_KERNEL):
                off = first + tap
                if off % 8 != s:
                    continue
                term = w_ref[tap:tap + 1, cs] * buf_ref[pl.ds(off - s, tm + 8), cs]
                group = term if group is None else group + term
            part = part + group[s:s + tm, :]
        y_ref[:, cs] = part
    acc = y_ref[...]
    ms = jnp.mean(acc * acc, axis=-1, keepdims=True)
    y = acc * lax.rsqrt(ms + NORM_EPS) * g_ref[...]
    o_ref[...] = (y * jax.nn.sigmoid(y)).astype(o_ref.dtype)


def _conv_module(u, conv_w, conv_b, conv_norm_g):
    tm = 128
    sub = tm // CONV_HALO
    last = SEQ // CONV_HALO - 1
    return pl.pallas_call(
        _conv_kernel,
        grid=(SEQ // tm,),
        in_specs=[pl.BlockSpec((CONV_HALO, CONV_WIDTH), lambda i: (jnp.maximum(i * sub - 1, 0), 0)),
                  pl.BlockSpec((tm, CONV_WIDTH), lambda i: (i, 0)),
                  pl.BlockSpec((CONV_HALO, CONV_WIDTH), lambda i: (jnp.minimum((i + 1) * sub, last), 0)),
                  pl.BlockSpec((CONV_KERNEL, CONV_WIDTH), lambda i: (0, 0)),
                  pl.BlockSpec((1, CONV_WIDTH), lambda i: (0, 0)),
                  pl.BlockSpec((1, CONV_WIDTH), lambda i: (0, 0))],
        out_specs=pl.BlockSpec((tm, CONV_WIDTH), lambda i: (i, 0)),
        out_shape=jax.ShapeDtypeStruct((SEQ, CONV_WIDTH), BF16),
        scratch_shapes=[pltpu.VMEM((tm + 2 * CONV_HALO, CONV_WIDTH), F32), pltpu.VMEM((tm, CONV_WIDTH), F32)],
        compiler_params=_params(("arbitrary",)),
        name="conv_module",
    )(u, u, u, conv_w, conv_b.reshape(1, CONV_WIDTH), conv_norm_g.reshape(1, CONV_WIDTH))


def _branch_kernel(a_ref, u_ref, wa_ref, wc_ref, ga_ref, gc_ref, o_ref, wab_ref, wcb_ref):
    @pl.when(pl.program_id(1) == 0)
    def _():
        wab_ref[...] = wa_ref[...].astype(BF16)
        wcb_ref[...] = wc_ref[...].astype(BF16)

    attn = jnp.dot(a_ref[...], wab_ref[...], preferred_element_type=F32)
    conv = jnp.dot(u_ref[...], wcb_ref[...], preferred_element_type=F32)
    o_ref[...] = (ga_ref[...].astype(F32) * attn + gc_ref[...].astype(F32) * conv).astype(o_ref.dtype)


def _branch_merge(attn, conv, w_attn_out, w_conv_out, gates):
    tm, tn = 1024, 512
    nj = D_MODEL // tn
    return pl.pallas_call(
        _branch_kernel,
        grid=(nj, SEQ // tm),
        in_specs=[pl.BlockSpec((tm, GROUP_WIDTH), lambda j, i: (i, 0)),
                  pl.BlockSpec((tm, CONV_WIDTH), lambda j, i: (i, 0)),
                  pl.BlockSpec((GROUP_WIDTH, tn), lambda j, i: (0, j)),
                  pl.BlockSpec((CONV_WIDTH, tn), lambda j, i: (0, j)),
                  pl.BlockSpec((tm, tn), lambda j, i: (i, j)),
                  pl.BlockSpec((tm, tn), lambda j, i: (i, nj + j))],
        out_specs=pl.BlockSpec((tm, tn), lambda j, i: (i, j)),
        out_shape=jax.ShapeDtypeStruct((SEQ, D_MODEL), BF16),
        scratch_shapes=[pltpu.VMEM((GROUP_WIDTH, tn), BF16), pltpu.VMEM((CONV_WIDTH, tn), BF16)],
        compiler_params=_params(("arbitrary", "arbitrary")),
        name="branch_merge",
    )(attn, conv, w_attn_out, w_conv_out, gates, gates)


def _wo_kernel(m_ref, w_ref, x_ref, g_ref, o_ref, wb_ref):
    @pl.when(pl.program_id(1) == 0)
    def _():
        wb_ref[...] = w_ref[...].astype(BF16)

    y = jnp.dot(m_ref[...], wb_ref[...], preferred_element_type=F32)
    o_ref[...] = x_ref[...] + g_ref[...] * y


def _wo_residual(merged, w_o, x, mod):
    tm, tn = 1024, 512
    nj = D_MODEL // tn
    return pl.pallas_call(
        _wo_kernel,
        grid=(nj, SEQ // tm),
        in_specs=[pl.BlockSpec((tm, D_MODEL), lambda j, i: (i, 0)),
                  pl.BlockSpec((D_MODEL, tn), lambda j, i: (0, j)),
                  pl.BlockSpec((tm, tn), lambda j, i: (i, j)),
                  pl.BlockSpec((1, tn), lambda j, i: (0, 2 * nj + j))],
        out_specs=pl.BlockSpec((tm, tn), lambda j, i: (i, j)),
        out_shape=jax.ShapeDtypeStruct((SEQ, D_MODEL), F32),
        scratch_shapes=[pltpu.VMEM((D_MODEL, tn), BF16)],
        compiler_params=_params(("arbitrary", "arbitrary")),
        name="wo_residual",
    )(merged, w_o, x, mod)


def _router_kernel(x_ref, g_ref, sc_ref, sh_ref, wr_ref, br_ref, hp_ref, idx_ref, wt_ref):
    tm = x_ref.shape[0]
    h = _modulated_norm(x_ref[...], g_ref[...], sc_ref[...], sh_ref[...])
    hp_ref[...] = _pack_halves(h)

    h_hi = h.astype(BF16)
    h_lo = (h - h_hi.astype(F32)).astype(BF16)
    w = wr_ref[...]
    w_hi = w.astype(BF16)
    w_lo = (w - w_hi.astype(F32)).astype(BF16)
    logits = (jnp.dot(h_hi, w_hi, preferred_element_type=F32) + jnp.dot(h_lo, w_hi, preferred_element_type=F32)
              + jnp.dot(h_hi, w_lo, preferred_element_type=F32) + br_ref[...])

    lane = lax.broadcasted_iota(jnp.int32, (tm, N_EXPERTS), 1)
    work = logits
    vals, idxs = [], []
    for _ in range(TOP_K):
        m = jnp.max(work, axis=-1, keepdims=True)
        idx = jnp.min(jnp.where(work == m, lane, N_EXPERTS), axis=-1, keepdims=True)
        vals.append(m)
        idxs.append(idx)
        work = jnp.where(lane == idx, -jnp.inf, work)
    exps = [jnp.exp(v - vals[0]) for v in vals]
    den = exps[0] + exps[1] + exps[2] + exps[3]
    col = lax.broadcasted_iota(jnp.int32, (tm, TOP_K), 1)
    idx_out = jnp.zeros((tm, TOP_K), jnp.int32)
    wt_out = jnp.zeros((tm, TOP_K), F32)
    for k in range(TOP_K):
        idx_out = jnp.where(col == k, idxs[k], idx_out)
        wt_out = jnp.where(col == k, exps[k] / den, wt_out)
    idx_ref[...] = idx_out
    wt_ref[...] = wt_out


def _norm2_router(x1, g, mod, w_router, b_router):
    tm = 256
    vec = lambda k: pl.BlockSpec((1, D_MODEL), lambda i, k=k: (0, k))
    return pl.pallas_call(
        _router_kernel,
        grid=(SEQ // tm,),
        in_specs=[pl.BlockSpec((tm, D_MODEL), lambda i: (i, 0)), vec(0), vec(4), vec(3),
                  pl.BlockSpec((D_MODEL, N_EXPERTS), lambda i: (0, 0)),
                  pl.BlockSpec((1, N_EXPERTS), lambda i: (0, 0))],
        out_specs=[pl.BlockSpec((tm, HALF), lambda i: (i, 0)),
                   pl.BlockSpec((tm, TOP_K), lambda i: (i, 0)),
                   pl.BlockSpec((tm, TOP_K), lambda i: (i, 0))],
        out_shape=[jax.ShapeDtypeStruct((SEQ, HALF), U32),
                   jax.ShapeDtypeStruct((SEQ, TOP_K), jnp.int32),
                   jax.ShapeDtypeStruct((SEQ, TOP_K), F32)],
        compiler_params=_params(("arbitrary",)),
        name="norm2_router",
    )(x1, g.reshape(1, D_MODEL), mod, mod, w_router, b_router.reshape(1, N_EXPERTS))


def _routing_tables(top_idx):
    i32 = jnp.int32
    e_flat = top_idx.reshape(-1)
    onehot = (e_flat[:, None] == jnp.arange(N_EXPERTS, dtype=i32)[None, :]).astype(i32)
    csum = jnp.cumsum(onehot, axis=0)
    rank = jnp.sum(onehot * csum, axis=1) - 1
    counts = csum[-1]
    nblk = (counts + SLOT_BLOCK - 1) // SLOT_BLOCK
    blk_end = jnp.cumsum(nblk)
    blk_start = blk_end - nblk
    ntile = (nblk + BLOCKS_PER_TILE - 1) // BLOCKS_PER_TILE
    tile_end = jnp.cumsum(ntile)
    tile_start = tile_end - ntile
    n_tiles = tile_end[-1]
    n_blocks = blk_end[-1]

    dest = (jnp.sum(onehot * (tile_start * TILE_ROWS)[None, :], axis=1) + rank).astype(i32)
    packed = jnp.sum(onehot * (blk_start * SLOT_BLOCK)[None, :], axis=1) + rank
    tok = jnp.arange(N_ASSIGN, dtype=i32) // TOP_K
    slot_tok = jnp.zeros((N_SLOT_BLOCKS * SLOT_BLOCK,), i32).at[packed].set(tok, unique_indices=True)

    s_raw = jnp.arange(MAX_TILES, dtype=i32)
    s = jnp.minimum(s_raw, n_tiles - 1)
    tile_e = jnp.minimum(jnp.searchsorted(tile_end, s, side='right'), N_EXPERTS - 1).astype(i32)
    local = s - tile_start[tile_e]
    tile_nb = jnp.clip(nblk[tile_e] - BLOCKS_PER_TILE * local, 0, BLOCKS_PER_TILE)
    tile_nb = jnp.where(s_raw < n_tiles, tile_nb, 0).astype(i32)
    tile_b0 = blk_start[tile_e] + BLOCKS_PER_TILE * local
    tile_src = s.astype(i32)
    rows = tile_b0[:, None] * SLOT_BLOCK + jnp.arange(TILE_ROWS, dtype=i32)[None, :]
    tile_tok = slot_tok[jnp.minimum(rows, N_SLOT_BLOCKS * SLOT_BLOCK - 1)].reshape(MAX_TILES, 1, TILE_ROWS)
    return dest, tile_tok, tile_e, tile_src, tile_nb, n_tiles.astype(i32)


UP_TF = 256
UP_ROW_CHUNK = 256


def _deinterleave_matrix():
    r = lax.broadcasted_iota(jnp.int32, (256, 256), 0)
    c = lax.broadcasted_iota(jnp.int32, (256, 256), 1)
    src = jnp.where(c < 128, 2 * c, 2 * (c - 128) + 1)
    return jnp.where(r == src, 1.0, 0.0).astype(BF16)


def _up_kernel(e_ref, nb_ref, first_ref, next_ref, h_hbm, w_ref, b_ref, o_ref, wb_ref, xb_ref, g_ref, sem):
    s = pl.program_id(0)
    j = pl.program_id(1)
    nb = nb_ref[s]

    def row_copy(r, tok):
        return pltpu.make_async_copy(h_hbm.at[pl.ds(tok, 1), :], g_ref.at[pl.ds(r, 1), :], sem)

    def request_rows(table_ref, first, count):
        def issue(i, carry):
            for u in range(2):
                r = first + 2 * i + u
                row_copy(r, table_ref[0, 0, r]).start(priority=u)
            return carry
        lax.fori_loop(0, count // 2, issue, 0, unroll=4)

    def request_chunk(table_ref, c):
        request_rows(table_ref, c * GATHER_CHUNK, GATHER_CHUNK)

    def chunks_of(blocks):
        return (blocks * SLOT_BLOCK + GATHER_CHUNK - 1) // GATHER_CHUNK

    @pl.when((s == 0) & (j == 0))
    def _():
        for c in range(CHUNKS_PER_TILE):
            @pl.when(c < chunks_of(nb))
            def _(c=c):
                request_chunk(first_ref, c)

    @pl.when(j == 0)
    def _():
        def drain_row(r, carry):
            row_copy(0, 0).wait()
            return carry

        def drain_chunk(c, carry):
            return lax.fori_loop(0, GATHER_CHUNK, drain_row, carry, unroll=8)

        lax.fori_loop(0, chunks_of(nb), drain_chunk, 0)

        for c in range(CHUNKS_PER_TILE):
            rows = slice(c * GATHER_CHUNK, (c + 1) * GATHER_CHUNK)

            @pl.when(c < chunks_of(nb))
            def _(rows=rows):
                lo, hi = _unpack_halves(g_ref[rows, :])
                xb_ref[rows, :HALF] = lo.astype(BF16)
                xb_ref[rows, HALF:] = hi.astype(BF16)

            @pl.when(c >= chunks_of(nb))
            def _(rows=rows):
                xb_ref[rows, :] = jnp.zeros((GATHER_CHUNK, D_MODEL), BF16)

    nxt = jnp.minimum(s + 1, MAX_TILES - 1)

    @pl.when((j >= 1) & (s + 1 < pl.num_programs(0)) & (j - 1 < chunks_of(nb_ref[nxt])))
    def _():
        request_chunk(next_ref, j - 1)

    def run(n_rows):
        if n_rows == 0:
            o_ref[...] = jnp.zeros((TILE_ROWS, UP_TF), BF16)
            return
        wb_ref[...] = w_ref[...].astype(BF16)
        perm = _deinterleave_matrix()
        for m0 in range(0, n_rows, UP_ROW_CHUNK):
            rows = slice(m0, m0 + UP_ROW_CHUNK)
            gu = jnp.dot(xb_ref[rows, :], wb_ref[...], preferred_element_type=F32) + b_ref[...]
            for c in range(2 * UP_TF // 256):
                gl = jnp.dot(gu[:, c * 256:(c + 1) * 256].astype(BF16), perm, preferred_element_type=F32)
                glu = jnp.minimum(gl[:, :128], SWIGLU_LIMIT)
                lin = jnp.clip(gl[:, 128:], -SWIGLU_LIMIT, SWIGLU_LIMIT)
                act = glu * jax.nn.sigmoid(SWIGLU_ALPHA * glu) * (lin + 1.0)
                o_ref[rows, c * 128:(c + 1) * 128] = act.astype(BF16)
        if n_rows < TILE_ROWS:
            o_ref[n_rows:, :] = jnp.zeros((TILE_ROWS - n_rows, UP_TF), BF16)

    _run_row_variant(nb, run, UP_ROW_VARIANTS)


def _run_row_variant(nb, run, variants):
    assert variants[-1] == BLOCKS_PER_TILE
    lo = 0
    for blocks in variants:
        @pl.when((nb > lo) & (nb <= blocks))
        def _(blocks=blocks):
            run(blocks * SLOT_BLOCK)
        lo = blocks

    @pl.when(nb == 0)
    def _():
        run(0)


def _expert_up(hp, tile_tok, w_up, b_up, tile_e, tile_nb, n_tiles):
    nj = EXPERT_FF // UP_TF
    assert CHUNKS_PER_TILE <= nj - 1
    assert CHUNKS_PER_TILE * GATHER_CHUNK == TILE_ROWS

    def w_index(s, j, e, nb):
        return (e[s], 0, jnp.where(nb[s] > 0, j, nj - 1))

    grid_spec = pltpu.PrefetchScalarGridSpec(
        num_scalar_prefetch=2,
        grid=(n_tiles, nj),
        in_specs=[pl.BlockSpec((1, 1, TILE_ROWS), lambda s, j, e, nb: (0, 0, 0), memory_space=pltpu.SMEM),
                  pl.BlockSpec((1, 1, TILE_ROWS), lambda s, j, e, nb: (jnp.minimum(s + 1, MAX_TILES - 1), 0, 0),
                               memory_space=pltpu.SMEM),
                  pl.BlockSpec(memory_space=pl.ANY),
                  pl.BlockSpec((None, D_MODEL, 2 * UP_TF), w_index),
                  pl.BlockSpec((None, 1, 2 * UP_TF), w_index)],
        out_specs=pl.BlockSpec((TILE_ROWS, UP_TF), lambda s, j, e, nb: (s, j)),
        scratch_shapes=[pltpu.VMEM((D_MODEL, 2 * UP_TF), BF16),
                        pltpu.VMEM((TILE_ROWS, D_MODEL), BF16),
                        pltpu.VMEM((TILE_ROWS, HALF), U32),
                        pltpu.SemaphoreType.DMA(())],
    )
    return pl.pallas_call(
        _up_kernel,
        grid_spec=grid_spec,
        out_shape=jax.ShapeDtypeStruct((n_tiles * TILE_ROWS, EXPERT_FF), BF16),
        compiler_params=_params(("arbitrary", "arbitrary")),
        name="expert_up",
    )(tile_e, tile_nb, tile_tok, tile_tok, hp, w_up, b_up.reshape(N_EXPERTS, 1, 2 * EXPERT_FF))


DOWN_TN = TOKEN_TILE * 128


def _down_kernel(e_ref, src_ref, nb_ref, a_ref, wl_ref, wh_ref, bl_ref, bh_ref, o_ref, wlb_ref, whb_ref):
    nb = nb_ref[pl.program_id(0)]

    def run(n_rows):
        if n_rows == 0:
            o_ref[...] = jnp.zeros((TILE_ROWS * TOKEN_TILE, 128), U32)
            return
        wlb_ref[...] = wl_ref[...].astype(BF16)
        whb_ref[...] = wh_ref[...].astype(BF16)
        for m0 in range(0, n_rows, UP_ROW_CHUNK):
            a = a_ref[m0:m0 + UP_ROW_CHUNK, :]
            y_lo = jnp.dot(a, wlb_ref[...], preferred_element_type=F32) + bl_ref[...]
            y_hi = jnp.dot(a, whb_ref[...], preferred_element_type=F32) + bh_ref[...]
            packed = _pack_halves(jnp.concatenate([y_lo, y_hi], axis=1))
            for c in range(TOKEN_TILE):
                o_ref[pl.ds(m0 * TOKEN_TILE + c, UP_ROW_CHUNK, stride=TOKEN_TILE), :] = (
                    packed[:, c * 128:(c + 1) * 128])
        if n_rows < TILE_ROWS:
            o_ref[n_rows * TOKEN_TILE:, :] = jnp.zeros(((TILE_ROWS - n_rows) * TOKEN_TILE, 128), U32)

    _run_row_variant(nb, run, DOWN_ROW_VARIANTS)


def _expert_down(act, w_down, b_down, tile_e, tile_src, tile_nb, n_tiles):
    nj = HALF // DOWN_TN

    def w_index(hi):
        return lambda s, j, e, src, nb: (e[s], 0, hi * nj + jnp.where(nb[s] > 0, j, nj - 1))

    grid_spec = pltpu.PrefetchScalarGridSpec(
        num_scalar_prefetch=3,
        grid=(n_tiles, nj),
        in_specs=[pl.BlockSpec((TILE_ROWS, EXPERT_FF), lambda s, j, e, src, nb: (src[s], 0)),
                  pl.BlockSpec((None, EXPERT_FF, DOWN_TN), w_index(0)),
                  pl.BlockSpec((None, EXPERT_FF, DOWN_TN), w_index(1)),
                  pl.BlockSpec((None, 1, DOWN_TN), w_index(0)),
                  pl.BlockSpec((None, 1, DOWN_TN), w_index(1))],
        out_specs=pl.BlockSpec((None, TILE_ROWS * TOKEN_TILE, 128), lambda s, j, e, src, nb: (j, s, 0)),
        scratch_shapes=[pltpu.VMEM((EXPERT_FF, DOWN_TN), BF16), pltpu.VMEM((EXPERT_FF, DOWN_TN), BF16)],
    )
    b3 = b_down.reshape(N_EXPERTS, 1, D_MODEL)
    return pl.pallas_call(
        _down_kernel,
        grid_spec=grid_spec,
        out_shape=jax.ShapeDtypeStruct((2, n_tiles * TILE_ROWS * TOKEN_TILE, 128), U32),
        compiler_params=_params(("arbitrary", "arbitrary")),
        name="expert_down",
    )(tile_e, tile_src, tile_nb, act, w_down, w_down, b3, b3)


COMBINE_TM = 64


def _combine_kernel(dest_ref, next_ref, y_hbm, x_ref, wt_ref, g_ref, o_ref, buf_ref, sems):
    i = pl.program_id(0)
    cur = i % 2

    def tile_copy(buf, t, k, half, slot):
        src = y_hbm.at[half, pl.ds(pl.multiple_of(slot * TOKEN_TILE, TOKEN_TILE), TOKEN_TILE), :]
        dst = buf_ref.at[buf, k, half, pl.ds(pl.multiple_of(t * TOKEN_TILE, TOKEN_TILE), TOKEN_TILE), :]
        return pltpu.make_async_copy(src, dst, sems.at[buf])

    def issue_all(table_ref, buf):
        def issue(t, carry):
            for k in range(TOP_K):
                slot = table_ref[0, 0, t * TOP_K + k]
                tile_copy(buf, t, k, 0, slot).start(priority=0)
                tile_copy(buf, t, k, 1, slot).start(priority=1)
            return carry
        lax.fori_loop(0, COMBINE_TM, issue, 0, unroll=4)

    def drain(t, carry):
        for k in range(TOP_K):
            tile_copy(cur, t, k, 0, 0).wait()
            tile_copy(cur, t, k, 1, 0).wait()
        return carry

    @pl.when(i == 0)
    def _():
        issue_all(dest_ref, 0)

    @pl.when(i + 1 < pl.num_programs(0))
    def _():
        issue_all(next_ref, 1 - cur)

    lax.fori_loop(0, COMBINE_TM, drain, 0, unroll=4)

    weights = [wt_ref[:, k:k + 1] for k in range(TOP_K)]
    for half in range(2):
        for c in range(TOKEN_TILE):
            col = (half * TOKEN_TILE + c) * 128
            acc_lo = jnp.zeros((COMBINE_TM, 128), F32)
            acc_hi = jnp.zeros((COMBINE_TM, 128), F32)
            for k in range(TOP_K):
                lo, hi = _unpack_halves(_load_token_tile_column(buf_ref, (cur, k), 0, COMBINE_TM, half, c))
                acc_lo = acc_lo + weights[k] * lo
                acc_hi = acc_hi + weights[k] * hi
            o_ref[:, col:col + 128] = x_ref[:, col:col + 128] + g_ref[:, col:col + 128] * acc_lo
            hcol = HALF + col
            o_ref[:, hcol:hcol + 128] = x_ref[:, hcol:hcol + 128] + g_ref[:, hcol:hcol + 128] * acc_hi


def _combine(y_slots, dest, x1, top_w, mod):
    nt = SEQ // COMBINE_TM
    return pl.pallas_call(
        _combine_kernel,
        grid=(nt,),
        in_specs=[pl.BlockSpec((1, 1, COMBINE_TM * TOP_K), lambda i: (i, 0, 0), memory_space=pltpu.SMEM),
                  pl.BlockSpec((1, 1, COMBINE_TM * TOP_K), lambda i: (jnp.minimum(i + 1, nt - 1), 0, 0),
                               memory_space=pltpu.SMEM),
                  pl.BlockSpec(memory_space=pl.ANY),
                  pl.BlockSpec((COMBINE_TM, D_MODEL), lambda i: (i, 0)),
                  pl.BlockSpec((COMBINE_TM, TOP_K), lambda i: (i, 0)),
                  pl.BlockSpec((1, D_MODEL), lambda i: (0, 5))],
        out_specs=pl.BlockSpec((COMBINE_TM, D_MODEL), lambda i: (i, 0)),
        out_shape=jax.ShapeDtypeStruct((SEQ, D_MODEL), F32),
        scratch_shapes=[pltpu.VMEM((2, TOP_K, 2, COMBINE_TM * TOKEN_TILE, 128), U32),
                        pltpu.SemaphoreType.DMA((2,))],
        compiler_params=_params(("arbitrary",)),
        name="moe_combine",
    )(dest.reshape(nt, 1, COMBINE_TM * TOP_K), dest.reshape(nt, 1, COMBINE_TM * TOP_K), y_slots, x1, top_w, mod)


def _layer(x, c, norm1_g, norm2_g, w_ada, b_ada, w_in, q_norm_g, k_norm_g, rel_bias, conv_w, conv_b,
           conv_norm_g, w_attn_out, w_conv_out, w_o, w_router, b_router, w_up, b_up, w_down, b_down):
    mod = _ada(c, w_ada, b_ada)
    h = _norm1(x, norm1_g, mod)

    outs, lses = [], []
    for group, (_, dil) in enumerate(ATTN_GROUPS):
        qkv = _qkv_proj(h, w_in, q_norm_g, k_norm_g, group, dil)
        o, lse = _band_attention(qkv, _bias_tiles(rel_bias, group, dil), dil)
        outs.append(o)
        lses.append(lse)
    attn = _merge_groups(outs, lses)

    u = _conv_module(_conv_glu_proj(h, w_in), conv_w, conv_b, conv_norm_g)
    gates = _gate_proj(h, w_in)
    merged = _branch_merge(attn, u, w_attn_out, w_conv_out, gates)
    x1 = _wo_residual(merged, w_o, x, mod)

    hp, top_idx, top_w = _norm2_router(x1, norm2_g, mod, w_router, b_router)
    dest, tile_tok, tile_e, tile_src, tile_nb, n_tiles = _routing_tables(top_idx)

    def experts(grid_tiles):
        def run():
            act = _expert_up(hp, tile_tok, w_up, b_up, tile_e, tile_nb, grid_tiles)
            y_slots = _expert_down(act, w_down, b_down, tile_e, tile_src, tile_nb, grid_tiles)
            return _combine(y_slots, dest, x1, top_w, mod)
        return run

    return lax.cond(n_tiles <= BALANCED_TILES, experts(BALANCED_TILES), experts(MAX_TILES))


def kernel(x, c, norm1_g, norm2_g, w_ada, b_ada, w_in, q_norm_g, k_norm_g, rel_bias, conv_w, conv_b,
           conv_norm_g, w_attn_out, w_conv_out, w_o, w_router, b_router, w_up, b_up, w_down, b_down):
    batch = x.shape[0]
    xs = x.reshape(batch * SEQ, D_MODEL)
    for layer in range(w_in.shape[0]):
        xs = _layer(xs, c, norm1_g[layer], norm2_g[layer], w_ada[layer], b_ada[layer], w_in[layer],
                    q_norm_g[layer], k_norm_g[layer], rel_bias, conv_w[layer], conv_b[layer],
                    conv_norm_g[layer], w_attn_out[layer], w_conv_out[layer], w_o[layer],
                    w_router[layer], b_router[layer], w_up[layer], b_up[layer], w_down[layer], b_down[layer])
    return xs.reshape(x.shape)
```
